```python
import math
import jax, jax.numpy as jnp
from jax import lax
import numpy as np

D_MODEL = 1024
BATCH = 32
SEQ = 256
DEPTH = 1
DEC_BATCH = 8
DEC_SEQ = 2048
PAST_LEN = 256

GRID_W = 64
N_HEADS = 4
HEAD_DIM = 256
D_MLSTM = N_HEADS * HEAD_DIM
D_HYENA = D_MODEL
HYENA_ORDER = 2
SHORT_CONV = 3
FILT_BANDS = 16
FILT_EMB = 1 + 2 * FILT_BANDS
FILT_WIDTH = 64
HYENA_MIN_DECAY = math.log(1e-2) / 1.5
HYENA_MAX_DECAY = math.log(1e-2) / 0.3
D_FF = 4 * D_MODEL
CHUNK = 128
RMS_EPS = 1e-6
SPLIT_SIZES = (D_MLSTM, D_MLSTM, D_MLSTM, D_MLSTM, 4 * N_HEADS, 3 * D_HYENA, D_MODEL, D_MODEL)
IN_COLS = 4 * D_MLSTM + 4 * N_HEADS + 3 * D_HYENA + 2 * D_MODEL

kernel_name = "hybrid_mlstm_hyena_diffusion_step"


def rmsnorm(x, w):
    xf = x.astype(jnp.float32)
    y = xf * lax.rsqrt(jnp.mean(xf * xf, axis=-1, keepdims=True) + RMS_EPS)
    return (y * w.astype(jnp.float32)).astype(x.dtype)


def grid_pos_embed(n_tokens):
    rows = n_tokens // GRID_W
    quarter = D_MODEL // 4
    omega = 1.0 / (10000.0 ** (jnp.arange(quarter, dtype=jnp.float32) / quarter))

    def axis_embed(pos):
        a = pos[:, None] * omega[None, :]
        return jnp.concatenate([jnp.sin(a), jnp.cos(a)], axis=-1)

    er = axis_embed(jnp.arange(rows, dtype=jnp.float32))
    ec = axis_embed(jnp.arange(GRID_W, dtype=jnp.float32))
    half = D_MODEL // 2
    pos = jnp.concatenate([jnp.broadcast_to(er[:, None, :], (rows, GRID_W, half)),
                           jnp.broadcast_to(ec[None, :, :], (rows, GRID_W, half))], axis=-1)
    return pos.reshape(rows * GRID_W, D_MODEL)


def mlstm_scan(q, k, v, li, lf, C0, n0, m0):
    B, S, H, DK = q.shape
    nc = S // CHUNK

    def to_chunks(a):
        a = a.reshape((B, nc, CHUNK, H) + a.shape[3:])
        return jnp.moveaxis(a, (1, 3), (0, 2))

    tril = jnp.tril(jnp.ones((CHUNK, CHUNK), dtype=bool))

    def step(carry, inp):
        C, n, m = carry
        qc, kc, vc, lic, lfc = inp
        b = jnp.cumsum(lfc, axis=-1)
        d = jnp.where(tril, b[..., :, None] - b[..., None, :] + lic[..., None, :], -jnp.inf)
        m_inter = b + m[..., None]
        m_comb = jnp.maximum(m_inter, jnp.max(d, axis=-1))
        s = jnp.einsum('bhjd,bhsd->bhjs', qc, kc) * jnp.exp(d - m_comb[..., None])
        w_inter = jnp.exp(m_inter - m_comb)
        num = w_inter[..., None] * jnp.einsum('bhjk,bhkv->bhjv', qc, C) + jnp.einsum('bhjs,bhsv->bhjv', s, vc)
        den = w_inter * jnp.einsum('bhjk,bhk->bhj', qc, n) + jnp.sum(s, axis=-1)
        h = num / jnp.maximum(jnp.abs(den), jnp.exp(-m_comb))[..., None]
        b_last = b[..., -1]
        a = b_last[..., None] - b + lic
        m_new = jnp.maximum(b_last + m, jnp.max(a, axis=-1))
        w_state = jnp.exp(a - m_new[..., None])
        decay = jnp.exp(b_last + m - m_new)
        C_new = decay[..., None, None] * C + jnp.einsum('bhsk,bhsv->bhkv', kc * w_state[..., None], vc)
        n_new = decay[..., None] * n + jnp.einsum('bhs,bhsk->bhk', w_state, kc)
        return (C_new, n_new, m_new), h

    xs = (to_chunks(q), to_chunks(k), to_chunks(v), to_chunks(li), to_chunks(lf))
    (C, n, m), h = lax.scan(step, (C0, n0, m0), xs)
    h = jnp.moveaxis(h, (0, 2), (1, 3)).reshape(B, S, H, v.shape[-1])
    return h, C, n, m


def mlstm_bidir(q, k, v, i_f, f_f, i_b, f_b, C0, n0, m0):
    hf, Cf, nf, mf = mlstm_scan(q, k, v, i_f, jax.nn.log_sigmoid(f_f), C0[:, 0], n0[:, 0], m0[:, 0])
    fl = lambda a: jnp.flip(a, axis=1)
    hb, Cb, nb, mb = mlstm_scan(fl(q), fl(k), fl(v), fl(i_b), fl(jax.nn.log_sigmoid(f_b)),
                                C0[:, 1], n0[:, 1], m0[:, 1])
    return (hf + fl(hb), jnp.stack([Cf, Cb], axis=1), jnp.stack([nf, nb], axis=1),
            jnp.stack([mf, mb], axis=1))


def hyena_filters(L, w1, b1, fr1, w2, b2, fr2, w3):
    f32 = jnp.float32
    t_idx = jnp.arange(L, dtype=f32)
    t = jnp.linspace(0.0, 1.0, L, dtype=f32)
    bands = jnp.arange(1, FILT_BANDS + 1, dtype=f32)
    ang = (2.0 * math.pi / L) * t_idx[:, None] * bands[None, :]
    z = jnp.concatenate([t[:, None], jnp.cos(ang), jnp.sin(ang)], axis=-1)
    h = jnp.sin(fr1.astype(f32) * (z @ w1.astype(f32) + b1.astype(f32)))
    h = jnp.sin(fr2.astype(f32) * (h @ w2.astype(f32) + b2.astype(f32)))
    h = (h @ w3.astype(f32)).reshape(L, HYENA_ORDER, D_HYENA)
    centre = L // 2
    dist = jnp.abs(t_idx - centre) / centre
    deltas = jnp.abs(jnp.linspace(HYENA_MIN_DECAY, HYENA_MAX_DECAY, D_HYENA, dtype=f32))
    h = h * jnp.exp(-dist[:, None, None] * deltas[None, None, :])
    return h / (jnp.sum(jnp.abs(h), axis=0, keepdims=True) + 1e-6)


def long_conv(z, h):
    L = z.shape[1]
    n = 2 * L
    y = jnp.fft.irfft(jnp.fft.rfft(z, n=n, axis=1) * jnp.fft.rfft(h, n=n, axis=0)[None], n=n, axis=1)
    return y[:, L // 2: L // 2 + L]


def short_conv(u, w, b):
    L = u.shape[1]
    pad = SHORT_CONV // 2
    up = jnp.pad(u, ((0, 0), (pad, SHORT_CONV - 1 - pad), (0, 0)))
    out = b
    for j in range(SHORT_CONV):
        out = out + up[:, j:j + L] * w[j]
    return out


def hyena(u, conv_w, conv_b, w1, b1, fr1, w2, b2, fr2, w3, skip):
    L = u.shape[1]
    uc = short_conv(u.astype(jnp.float32), conv_w.astype(jnp.float32), conv_b.astype(jnp.float32))
    x1, x2, z = jnp.split(uc, 3, axis=-1)
    h = hyena_filters(L, w1, b1, fr1, w2, b2, fr2, w3)
    skip = skip.astype(jnp.float32)
    for o, gate in enumerate((x1, x2)):
        z = gate * (long_conv(z, h[:, o]) + skip[o] * z)
    return z


def layer(x, cond, C0, n0, m0, w_ada, b_ada, norm1_w, w_in, b_in, hy_conv_w, hy_conv_b,
          filt_w1, filt_b1, filt_freq1, filt_w2, filt_b2, filt_freq2, filt_w3, hy_skip,
          mlstm_norm_w, w_br_m, w_br_h, w_out, norm2_w, w_mlp1, b_mlp1, w_mlp2, b_mlp2):
    B, S, _ = x.shape
    f32 = jnp.float32
    mod = jax.nn.silu(cond) @ w_ada + b_ada
    sh1, sc1, g1, sh2, sc2, g2 = jnp.split(mod[:, None, :], 6, axis=-1)
    hn = rmsnorm(x, norm1_w) * (1 + sc1) + sh1
    proj = hn @ w_in + b_in
    idx = [int(i) for i in np.cumsum(SPLIT_SIZES)[:-1]]
    q, k, v, o, gates, u_h, gm, gh = jnp.split(proj, idx, axis=-1)
    q = q.astype(f32).reshape(B, S, N_HEADS, HEAD_DIM) * (HEAD_DIM ** -0.5)
    k = k.astype(f32).reshape(B, S, N_HEADS, HEAD_DIM)
    v = v.astype(f32).reshape(B, S, N_HEADS, HEAD_DIM)
    gates = gates.astype(f32).reshape(B, S, 4, N_HEADS)
    h, C, n, m = mlstm_bidir(q, k, v, gates[:, :, 0], gates[:, :, 1], gates[:, :, 2], gates[:, :, 3],
                             C0.astype(f32), n0.astype(f32), m0.astype(f32))
    h = h * lax.rsqrt(jnp.mean(h * h, axis=-1, keepdims=True) + RMS_EPS)
    h = h.reshape(B, S, D_MLSTM) * mlstm_norm_w.astype(f32) * jax.nn.sigmoid(o.astype(f32))
    h = h.astype(x.dtype)
    yh = hyena(u_h, hy_conv_w, hy_conv_b, filt_w1, filt_b1, filt_freq1, filt_w2, filt_b2,
               filt_freq2, filt_w3, hy_skip).astype(x.dtype)
    merged = jax.nn.sigmoid(gm) * (h @ w_br_m) + jax.nn.sigmoid(gh) * (yh @ w_br_h)
    x = x + g1 * (merged @ w_out)
    hn2 = rmsnorm(x, norm2_w) * (1 + sc2) + sh2
    ff = jnp.square(jax.nn.relu(hn2 @ w_mlp1 + b_mlp1)) @ w_mlp2 + b_mlp2
    x = x + g2 * ff
    return x, C, n, m


def setup_inputs(seed: int = 0) -> dict:
    key = jax.random.key(seed)
    ks = jax.random.split(key, 32)
    nrm = lambda k, shape, s=1.0: s * jax.random.normal(k, shape, dtype=jnp.float32)
    gate_off = 4 * D_MLSTM
    forget_bias = jnp.linspace(3.0, 6.0, N_HEADS, dtype=jnp.float32)
    b_in = nrm(ks[10], (DEPTH, IN_COLS), 0.01)
    b_in = b_in.at[:, gate_off + N_HEADS: gate_off + 2 * N_HEADS].add(forget_bias)
    b_in = b_in.at[:, gate_off + 3 * N_HEADS: gate_off + 4 * N_HEADS].add(forget_bias)
    return {
        "x_prompt": nrm(ks[0], (BATCH, SEQ, D_MODEL)),
        "x_sample": nrm(ks[1], (DEC_BATCH, DEC_SEQ, D_MODEL)),
        "state_mlstm_C": nrm(ks[2], (DEC_BATCH, DEPTH, 2, N_HEADS, HEAD_DIM, HEAD_DIM), 0.3),
        "state_mlstm_n": nrm(ks[3], (DEC_BATCH, DEPTH, 2, N_HEADS, HEAD_DIM), 0.3),
        "state_mlstm_m": nrm(ks[4], (DEC_BATCH, DEPTH, 2, N_HEADS), 0.5),
        "c": nrm(ks[5], (DEC_BATCH, D_MODEL)),
        "c_ctx": nrm(ks[6], (D_MODEL,)),
        "w_ada": nrm(ks[7], (DEPTH, D_MODEL, 6 * D_MODEL), 0.5 * D_MODEL ** -0.5),
        "b_ada": nrm(ks[8], (DEPTH, 6 * D_MODEL), 0.01),
        "norm1_w": 1.0 + nrm(ks[9], (DEPTH, D_MODEL), 0.05),
        "w_in": nrm(ks[11], (DEPTH, D_MODEL, IN_COLS), D_MODEL ** -0.5),
        "b_in": b_in,
        "hy_conv_w": nrm(ks[12], (DEPTH, SHORT_CONV, 3 * D_HYENA), SHORT_CONV ** -0.5),
        "hy_conv_b": nrm(ks[13], (DEPTH, 3 * D_HYENA), 0.01),
        "filt_w1": nrm(ks[14], (DEPTH, FILT_EMB, FILT_WIDTH), FILT_EMB ** -0.5),
        "filt_b1": nrm(ks[15], (DEPTH, FILT_WIDTH), 0.01),
        "filt_freq1": 1.0 + nrm(ks[16], (DEPTH, FILT_WIDTH), 0.1),
        "filt_w2": nrm(ks[17], (DEPTH, FILT_WIDTH, FILT_WIDTH), FILT_WIDTH ** -0.5),
        "filt_b2": nrm(ks[18], (DEPTH, FILT_WIDTH), 0.01),
        "filt_freq2": 1.0 + nrm(ks[19], (DEPTH, FILT_WIDTH), 0.1),
        "filt_w3": nrm(ks[20], (DEPTH, FILT_WIDTH, HYENA_ORDER * D_HYENA), FILT_WIDTH ** -0.5),
        "hy_skip": nrm(ks[21], (DEPTH, HYENA_ORDER, D_HYENA)),
        "mlstm_norm_w": 1.0 + nrm(ks[22], (DEPTH, D_MLSTM), 0.05),
        "w_br_m": nrm(ks[23], (DEPTH, D_MLSTM, D_MODEL), D_MLSTM ** -0.5),
        "w_br_h": nrm(ks[24], (DEPTH, D_HYENA, D_MODEL), D_HYENA ** -0.5),
        "w_out": nrm(ks[25], (DEPTH, D_MODEL, D_MODEL), D_MODEL ** -0.5),
        "norm2_w": 1.0 + nrm(ks[26], (DEPTH, D_MODEL), 0.05),
        "w_mlp1": nrm(ks[27], (DEPTH, D_MODEL, D_FF), D_MODEL ** -0.5),
        "b_mlp1": nrm(ks[28], (DEPTH, D_FF), 0.01),
        "w_mlp2": nrm(ks[29], (DEPTH, D_FF, D_MODEL), D_FF ** -0.5),
        "b_mlp2": nrm(ks[30], (DEPTH, D_MODEL), 0.01),
        "final_norm_w": 1.0 + nrm(ks[31], (D_MODEL,), 0.05),
    }


def reference(x_prompt, x_sample, state_mlstm_C, state_mlstm_n, state_mlstm_m, c, c_ctx,
              w_ada, b_ada, norm1_w, w_in, b_in, hy_conv_w, hy_conv_b, filt_w1, filt_b1,
              filt_freq1, filt_w2, filt_b2, filt_freq2, filt_w3, hy_skip, mlstm_norm_w,
              w_br_m, w_br_h, w_out, norm2_w, w_mlp1, b_mlp1, w_mlp2, b_mlp2, final_norm_w):
    layer_params = (w_ada, b_ada, norm1_w, w_in, b_in, hy_conv_w, hy_conv_b, filt_w1, filt_b1,
                    filt_freq1, filt_w2, filt_b2, filt_freq2, filt_w3, hy_skip, mlstm_norm_w,
                    w_br_m, w_br_h, w_out, norm2_w, w_mlp1, b_mlp1, w_mlp2, b_mlp2)
    f32 = jnp.float32
    bp = x_prompt.shape[0]
    zC = jnp.zeros((bp, 2, N_HEADS, HEAD_DIM, HEAD_DIM), f32)
    zn = jnp.zeros((bp, 2, N_HEADS, HEAD_DIM), f32)
    zm = jnp.zeros((bp, 2, N_HEADS), f32)
    xp = x_prompt
    Cs, ns, ms = [], [], []
    for l in range(DEPTH):
        lw = [p[l] for p in layer_params]
        xp, Cl, nl, ml = layer(xp, c_ctx[None, :], zC, zn, zm, *lw)
        Cs.append(Cl)
        ns.append(nl)
        ms.append(ml)
    y_prompt = rmsnorm(xp, final_norm_w)
    new_state_C = jnp.stack(Cs, axis=1)
    new_state_n = jnp.stack(ns, axis=1)
    new_state_m = jnp.stack(ms, axis=1)
    xs = x_sample + grid_pos_embed(x_sample.shape[1]).astype(x_sample.dtype)[None]
    for l in range(DEPTH):
        lw = [p[l] for p in layer_params]
        xs, _, _, _ = layer(xs, c, state_mlstm_C[:, l], state_mlstm_n[:, l], state_mlstm_m[:, l], *lw)
    y_sample = rmsnorm(xs, final_norm_w)
    return (y_prompt, y_sample, new_state_C, new_state_n, new_state_m)
```

```python
import functools
import math

import numpy as np
import jax
import jax.numpy as jnp
from jax import lax
from jax.experimental import pallas as pl
from jax.experimental.pallas import tpu as pltpu

F32 = jnp.float32
BF16 = jnp.bfloat16
HIGHEST = lax.Precision.HIGHEST

D_MODEL = 1024
N_HEADS = 4
HEAD_DIM = 256
D_FF = 4 * D_MODEL
GRID_W = 64
FILT_BANDS = 16
FILT_WIDTH = 64
HYENA_MIN_DECAY = math.log(1e-2) / 1.5
HYENA_MAX_DECAY = math.log(1e-2) / 0.3
CHUNK = 128
RMS_EPS = 1e-6
N_GATES = 4 * N_HEADS
QKVO_COLS = 4 * D_MODEL
REST_COLS = 5 * D_MODEL

V7X_VMEM_BYTES = 64 * 1024 * 1024
VMEM_LIMIT = V7X_VMEM_BYTES - 8 * 1024 * 1024

TOKEN_TILE = 256
HYENA_LANES = 256
FILTER_LANES = 256
SPEC_ROWS = 32


def _params(sem):
    return pltpu.CompilerParams(dimension_semantics=sem, vmem_limit_bytes=VMEM_LIMIT)


def _const_spec(shape):
    nd = len(shape)
    return pl.BlockSpec(shape, lambda *_: (0,) * nd, pipeline_mode=pl.Buffered(1))


def _sigmoid(x):
    return 1.0 / (1.0 + jnp.exp(-x))


def _rms(x):
    return x * lax.rsqrt(jnp.mean(x * x, axis=-1, keepdims=True) + RMS_EPS)


def _dot(a, b):
    return jnp.dot(a, b, preferred_element_type=F32)


def _mod_kernel(c_ref, w_ref, b_ref, o_ref):
    c = c_ref[...]
    s = c * _sigmoid(c)
    o_ref[...] = jnp.dot(s, w_ref[...], precision=HIGHEST, preferred_element_type=F32) + b_ref[...]


def _modulation(cond, w_ada, b_ada):
    rows = cond.shape[0]
    n = w_ada.shape[1]
    tn = 1536
    return pl.pallas_call(
        _mod_kernel,
        grid=(n // tn,),
        in_specs=[pl.BlockSpec((rows, D_MODEL), lambda j: (0, 0)),
                  pl.BlockSpec((D_MODEL, tn), lambda j: (0, j)),
                  pl.BlockSpec((1, tn), lambda j: (0, j))],
        out_specs=pl.BlockSpec((rows, tn), lambda j: (0, j)),
        out_shape=jax.ShapeDtypeStruct((rows, n), F32),
        compiler_params=_params(("arbitrary",)),
        name="modulation",
    )(cond, w_ada, b_ada.reshape(1, n))


def _pos_kernel(o_ref, *, rows):
    quarter = D_MODEL // 4
    half = D_MODEL // 2
    k = lax.broadcasted_iota(jnp.int32, (1, quarter), 1).astype(F32)
    omega = jnp.exp(k * (-math.log(10000.0) / quarter))

    def axis_embed(n):
        p = lax.broadcasted_iota(jnp.int32, (n, 1), 0).astype(F32)
        a = p * omega
        return jnp.concatenate([jnp.sin(a), jnp.cos(a)], axis=-1)

    er = axis_embed(rows)
    ec = axis_embed(GRID_W)
    o_ref[:, :, 0:half] = jnp.broadcast_to(er[:, None, :], (rows, GRID_W, half))
    o_ref[:, :, half:D_MODEL] = jnp.broadcast_to(ec[None, :, :], (rows, GRID_W, half))


def _pos_table(n_tokens):
    rows = n_tokens // GRID_W
    out = pl.pallas_call(
        functools.partial(_pos_kernel, rows=rows),
        out_shape=jax.ShapeDtypeStruct((rows, GRID_W, D_MODEL), F32),
        compiler_params=pltpu.CompilerParams(vmem_limit_bytes=VMEM_LIMIT),
        name="pos_table",
    )()
    return out.reshape(n_tokens, D_MODEL)


def _dft_mats(p):
    n = 2 * p
    idx = np.arange(p, dtype=np.float64)
    ang = 2.0 * np.pi * np.outer(idx, idx) / n
    alt = np.where(np.arange(p) % 2 == 0, 1.0, -1.0)
    fwd = np.zeros((n, p))
    fwd[:p] = np.cos(ang)
    fwd[p] = alt
    fwd[p + 1:] = -np.sin(ang[1:])
    inv = np.zeros((p, n))
    inv[:, :p] = 2.0 * np.cos(ang) / n
    inv[:, 0] = 1.0 / n
    inv[:, p] = alt / n
    inv[:, p + 1:] = -2.0 * np.sin(ang[:, 1:]) / n
    sign = np.concatenate([alt, alt])
    sign[p] = 1.0
    return fwd.astype(np.float32), inv.astype(np.float32), sign.astype(np.float32).reshape(n, 1)


def _filter_kernel(w1t_ref, w1c_ref, w1s_ref, b1_ref, fr1_ref, w2_ref, b2_ref, fr2_ref, w3_ref,
                   fhi_ref, flo_ref, sign_ref, g_ref, feat_ref, *, seq, p):
    nb = seq // p
    tn = w3_ref.shape[1]
    step = pl.program_id(0)
    idx = lax.broadcasted_iota(jnp.int32, (seq, 1), 0).astype(F32)

    @pl.when(step == 0)
    def _():
        t = idx / float(seq - 1)
        bands = (lax.broadcasted_iota(jnp.int32, (1, FILT_BANDS), 1) + 1).astype(F32)
        ang = ((2.0 * math.pi / seq) * idx) * bands
        pre = (t * w1t_ref[...]
               + jnp.dot(jnp.cos(ang), w1c_ref[...], precision=HIGHEST, preferred_element_type=F32)
               + jnp.dot(jnp.sin(ang), w1s_ref[...], precision=HIGHEST, preferred_element_type=F32)
               + b1_ref[...])
        h1 = jnp.sin(fr1_ref[...] * pre)
        h2 = jnp.sin(fr2_ref[...] * (jnp.dot(h1, w2_ref[...], precision=HIGHEST, preferred_element_type=F32)
                                     + b2_ref[...]))
        feat_ref[...] = h2

    hh = jnp.dot(feat_ref[...], w3_ref[...], precision=HIGHEST, preferred_element_type=F32)
    d0 = lax.rem(step * tn, D_MODEL)
    d = (d0 + lax.broadcasted_iota(jnp.int32, (1, tn), 1)).astype(F32)
    delta = jnp.abs(HYENA_MIN_DECAY + (HYENA_MAX_DECAY - HYENA_MIN_DECAY) * d / float(D_MODEL - 1))
    centre = seq // 2
    dist = jnp.abs(idx - float(centre)) / float(centre)
    hh = hh * jnp.exp(-dist * delta)
    hh = hh / (jnp.sum(jnp.abs(hh), axis=0, keepdims=True) + 1e-6)

    fhi = fhi_ref[...]
    flo = flo_ref[...]
    spec = []
    for k in range(nb):
        hb = hh[k * p:(k + 1) * p]
        hi = hb.astype(BF16)
        lo = (hb - hi.astype(F32)).astype(BF16)
        spec.append(_dot(fhi, hi) + _dot(fhi, lo) + _dot(flo, hi))
    sign = sign_ref[...]
    for k in range(nb + 1):
        if k == 0:
            g = spec[0]
        elif k == nb:
            g = sign * spec[nb - 1]
        else:
            g = spec[k] + sign * spec[k - 1]
        g_ref[k] = g


def _filter_spectra(seq, p, w1, b1, fr1, w2, b2, fr2, w3, fwd32, sign):
    tn = FILTER_LANES
    ncols = w3.shape[1]
    nb = seq // p
    fhi = fwd32.astype(BF16)
    flo = (fwd32 - fhi.astype(F32)).astype(BF16)
    args = (w1[0:1], w1[1:1 + FILT_BANDS], w1[1 + FILT_BANDS:], b1.reshape(1, -1), fr1.reshape(1, -1),
            w2, b2.reshape(1, -1), fr2.reshape(1, -1))
    in_specs = [_const_spec(a.shape) for a in args]
    in_specs += [pl.BlockSpec((FILT_WIDTH, tn), lambda j: (0, j)),
                 _const_spec(fhi.shape), _const_spec(flo.shape), _const_spec(sign.shape)]
    return pl.pallas_call(
        functools.partial(_filter_kernel, seq=seq, p=p),
        grid=(ncols // tn,),
        in_specs=in_specs,
        out_specs=pl.BlockSpec((nb + 1, 2 * p, tn), lambda j: (0, 0, j)),
        out_shape=jax.ShapeDtypeStruct((nb + 1, 2 * p, ncols), F32),
        scratch_shapes=[pltpu.VMEM((seq, FILT_WIDTH), F32)],
        compiler_params=_params(("arbitrary",)),
        name=f"filter_spectra_{seq}",
    )(*args, w3, fhi, flo, sign)


def _inproj_kernel(*refs, has_pos):
    if has_pos:
        x_ref, pos_ref = refs[:2]
        refs = refs[2:]
    else:
        x_ref = refs[0]
        refs = refs[1:]
    (mod_ref, n1w_ref, wa_ref, ba_ref, wg_ref, bg_ref, wr_ref, br_ref,
     qkv_ref, o_ref, gates_ref, uh_ref, gmh_ref) = refs
    x = x_ref[0]
    if has_pos:
        x = x + pos_ref[...]
    sh1 = mod_ref[0, :, 0:D_MODEL]
    sc1 = mod_ref[0, :, D_MODEL:2 * D_MODEL]
    hn = (_rms(x) * n1w_ref[...]) * (1.0 + sc1) + sh1
    hb = hn.astype(BF16)
    hlo = (hn - hb.astype(F32)).astype(BF16)
    for c in range(4):
        sl = slice(c * D_MODEL, (c + 1) * D_MODEL)
        r = _dot(hb, wa_ref[:, sl]) + ba_ref[:, sl]
        if c < 3:
            qkv_ref[0, :, sl] = r.astype(BF16)
        else:
            o_ref[0] = r
    for c in range(5):
        sl = slice(c * D_MODEL, (c + 1) * D_MODEL)
        r = _dot(hb, wr_ref[:, sl]) + br_ref[:, sl]
        if c < 3:
            uh_ref[0, :, sl] = r
        else:
            gmh_ref[0, :, (c - 3) * D_MODEL:(c - 2) * D_MODEL] = r
    p1 = _dot(hb, wg_ref[...])
    p2 = _dot(hlo, wg_ref[...])
    gates_ref[0] = p1[:, 0:N_GATES] + p1[:, N_GATES:2 * N_GATES] + p2[:, 0:N_GATES] + bg_ref[...]


def _in_projection(x, pos, mod3, mod_row, n1w, wa, ba, wg, bg, wr, br):
    b, s, _ = x.shape
    tm = TOKEN_TILE
    has_pos = pos is not None
    in_specs = [pl.BlockSpec((1, tm, D_MODEL), lambda i, j: (i, j, 0))]
    args = [x]
    if has_pos:
        in_specs.append(pl.BlockSpec((tm, D_MODEL), lambda i, j: (j, 0)))
        args.append(pos)
    in_specs += [pl.BlockSpec((1, 1, 2 * D_MODEL), lambda i, j: (mod_row(i), 0, 0)),
                 _const_spec(n1w.shape), _const_spec(wa.shape), _const_spec(ba.shape),
                 _const_spec(wg.shape), _const_spec(bg.shape), _const_spec(wr.shape), _const_spec(br.shape)]
    args += [mod3, n1w, wa, ba, wg, bg, wr, br]
    out_shape = (jax.ShapeDtypeStruct((b, s, 3 * D_MODEL), BF16),
                 jax.ShapeDtypeStruct((b, s, D_MODEL), F32),
                 jax.ShapeDtypeStruct((b, s, N_GATES), F32),
                 jax.ShapeDtypeStruct((b, s, 3 * D_MODEL), F32),
                 jax.ShapeDtypeStruct((b, s, 2 * D_MODEL), F32))
    out_specs = tuple(pl.BlockSpec((1, tm, sh.shape[2]), lambda i, j: (i, j, 0)) for sh in out_shape)
    return pl.pallas_call(
        functools.partial(_inproj_kernel, has_pos=has_pos),
        grid=(b, s // tm),
        in_specs=in_specs,
        out_specs=out_specs,
        out_shape=out_shape,
        compiler_params=_params(("arbitrary", "arbitrary")),
        name="in_projection_pos" if has_pos else "in_projection",
    )(*args)


def _log_sigmoid(x):
    return jnp.minimum(x, 0.0) - jnp.log(1.0 + jnp.exp(-jnp.abs(x)))


def _mlstm_kernel(*refs, seq, has_init, emit_state):
    q_ref, k_ref, v_ref, gcol_ref, grow_ref = refs[:5]
    refs = refs[5:]
    if has_init:
        c0_ref, n0_ref, m0_ref = refs[:3]
        refs = refs[3:]
    h_ref = refs[0]
    refs = refs[1:]
    if emit_state:
        c_out, n_out, m_out = refs[:3]
        refs = refs[3:]
    c_s, n_s = refs
    t = CHUNK
    nc = seq // t
    scale = HEAD_DIM ** -0.5

    for dr in range(2):
        if has_init:
            c_s[dr] = c0_ref[0, 0, dr, 0]
            n_s[dr] = n0_ref[0, 0, dr:dr + 1, :]
        else:
            c_s[dr] = jnp.zeros((HEAD_DIM, HEAD_DIM), F32)
            n_s[dr] = jnp.zeros((1, HEAD_DIM), F32)
    h_ref[...] = jnp.zeros(h_ref.shape, F32)

    row = lax.broadcasted_iota(jnp.int32, (t, t), 0)
    col = lax.broadcasted_iota(jnp.int32, (t, t), 1)
    lower = col <= row
    upper = col >= row

    def chunk_step(c, dr, m):
        r0 = pl.multiple_of(c * t, t)
        q = q_ref[0, pl.ds(r0, t), :]
        k = k_ref[0, pl.ds(r0, t), :]
        v = v_ref[0, pl.ds(r0, t), :]
        gc = gcol_ref[0, 0, pl.ds(r0, t), :]
        gr = grow_ref[0, 0, c]
        li_col = gc[:, 2 * dr:2 * dr + 1]
        lf_col = _log_sigmoid(gc[:, 2 * dr + 1:2 * dr + 2])
        li_row = gr[2 * dr:2 * dr + 1, :]
        lf_row = _log_sigmoid(gr[2 * dr + 1:2 * dr + 2, :])
        mask = lower if dr == 0 else upper
        mask_t = upper if dr == 0 else lower
        b_col = jnp.sum(jnp.where(mask, lf_row, 0.0), axis=1, keepdims=True)
        b_row = jnp.sum(jnp.where(mask_t, lf_col, 0.0), axis=0, keepdims=True)
        b_last = jnp.sum(lf_row, axis=1, keepdims=True)
        d = jnp.where(mask, b_col - b_row + li_row, -jnp.inf)
        m_inter = b_col + m
        m_comb = jnp.maximum(m_inter, jnp.max(d, axis=1, keepdims=True))
        pw = jnp.exp(d - m_comb)
        s = lax.dot_general(q, k, (((1,), (1,)), ((), ())), preferred_element_type=F32) * scale * pw
        w_inter = jnp.exp(m_inter - m_comb)
        c_prev = c_s[dr]
        n_prev = n_s[dr]
        qc = _dot(q, c_prev.astype(BF16)) * scale
        sv = _dot(s.astype(BF16), v)
        num = w_inter * qc + sv
        qn = jnp.sum(q.astype(F32) * n_prev, axis=1, keepdims=True) * scale
        den = w_inter * qn + jnp.sum(s, axis=1, keepdims=True)
        h = num / jnp.maximum(jnp.abs(den), jnp.exp(-m_comb))
        h_ref[0, pl.ds(r0, t), :] += h
        a = b_last - b_col + li_col
        m_new = jnp.maximum(b_last + m, jnp.max(a, axis=0, keepdims=True))
        w_state = jnp.exp(a - m_new)
        decay = jnp.exp(b_last + m - m_new)
        kw = k.astype(F32) * w_state
        kv = lax.dot_general(kw.astype(BF16), v, (((0,), (0,)), ((), ())), preferred_element_type=F32)
        c_s[dr] = decay * c_prev + kv
        n_s[dr] = decay * n_prev + jnp.sum(kw, axis=0, keepdims=True)
        return m_new

    def body(i, carry):
        mf, mb = carry
        mf = chunk_step(i, 0, mf)
        mb = chunk_step(nc - 1 - i, 1, mb)
        return mf, mb

    if has_init:
        m_init = (m0_ref[0, 0, 0:1, :], m0_ref[0, 0, 1:2, :])
    else:
        m_init = (jnp.zeros((1, 1), F32), jnp.zeros((1, 1), F32))
    mf, mb = lax.fori_loop(0, nc, body, m_init)

    if emit_state:
        for dr, m in ((0, mf), (1, mb)):
            c_out[0, 0, dr, 0] = c_s[dr]
            n_out[0, 0, dr:dr + 1, :] = n_s[dr]
            m_out[0, 0, dr:dr + 1, :] = jnp.broadcast_to(m, (1, 128))


def _mlstm(qkv, gcol, grow, init, emit_state):
    b, s, _ = qkv.shape
    nc = s // CHUNK
    has_init = init is not None
    in_specs = [pl.BlockSpec((1, s, HEAD_DIM), lambda i, h: (i, 0, h)),
                pl.BlockSpec((1, s, HEAD_DIM), lambda i, h: (i, 0, N_HEADS + h)),
                pl.BlockSpec((1, s, HEAD_DIM), lambda i, h: (i, 0, 2 * N_HEADS + h)),
                pl.BlockSpec((1, 1, s, 4), lambda i, h: (i, h, 0, 0)),
                pl.BlockSpec((1, 1, nc, 4, CHUNK), lambda i, h: (i, h, 0, 0, 0))]
    args = [qkv, qkv, qkv, gcol, grow]
    if has_init:
        c0, n0, m0 = init
        in_specs += [pl.BlockSpec((1, 1, 2, 1, HEAD_DIM, HEAD_DIM), lambda i, h: (i, 0, 0, h, 0, 0)),
                     pl.BlockSpec((1, 1, 2, HEAD_DIM), lambda i, h: (i, h, 0, 0)),
                     pl.BlockSpec((1, 1, 2, 1), lambda i, h: (i, h, 0, 0))]
        args += [c0, n0, m0]
    out_shape = [jax.ShapeDtypeStruct((b, s, D_MODEL), F32)]
    out_specs = [pl.BlockSpec((1, s, HEAD_DIM), lambda i, h: (i, 0, h))]
    if emit_state:
        out_shape += [jax.ShapeDtypeStruct((b, 1, 2, N_HEADS, HEAD_DIM, HEAD_DIM), F32),
                      jax.ShapeDtypeStruct((b, N_HEADS, 2, HEAD_DIM), F32),
                      jax.ShapeDtypeStruct((b, N_HEADS, 2, 128), F32)]
        out_specs += [pl.BlockSpec((1, 1, 2, 1, HEAD_DIM, HEAD_DIM), lambda i, h: (i, 0, 0, h, 0, 0)),
                      pl.BlockSpec((1, 1, 2, HEAD_DIM), lambda i, h: (i, h, 0, 0)),
                      pl.BlockSpec((1, 1, 2, 128), lambda i, h: (i, h, 0, 0))]
    return pl.pallas_call(
        functools.partial(_mlstm_kernel, seq=s, has_init=has_init, emit_state=emit_state),
        grid=(b, N_HEADS),
        in_specs=in_specs,
        out_specs=tuple(out_specs),
        out_shape=tuple(out_shape),
        scratch_shapes=[pltpu.VMEM((2, HEAD_DIM, HEAD_DIM), F32), pltpu.VMEM((2, 1, HEAD_DIM), F32)],
        compiler_params=_params(("arbitrary", "arbitrary")),
        name=f"mlstm_{s}",
    )(*args)


def _hyena_kernel(x1_ref, x2_ref, v_ref, w1_ref, w2_ref, wv_ref, b1_ref, b2_ref, bv_ref, skip_ref,
                  g0_ref, g1_ref, fwd_ref, inv_ref, o_ref, z_s, gate1_s, gate2_s, zf_s, yf_s, *, seq, p):
    nb = seq // p
    half = nb // 2
    bb = x1_ref.shape[0]
    first = lax.broadcasted_iota(jnp.int32, (seq, 1), 0) == 0
    last = lax.broadcasted_iota(jnp.int32, (seq, 1), 0) == seq - 1

    def short_conv(u, w_ref, b_ref):
        prev = jnp.where(first, 0.0, pltpu.roll(u, 1, 0))
        nxt = jnp.where(last, 0.0, pltpu.roll(u, seq - 1, 0))
        return b_ref[...] + prev * w_ref[0:1, :] + u * w_ref[1:2, :] + nxt * w_ref[2:3, :]

    fwd = fwd_ref[...]
    inv = inv_ref[...]
    for bi in range(bb):
        gate1_s[...] = short_conv(x1_ref[bi], w1_ref, b1_ref)
        gate2_s[...] = short_conv(x2_ref[bi], w2_ref, b2_ref)
        z_s[...] = short_conv(v_ref[bi], wv_ref, bv_ref)
        for order, (g_ref, gate_s) in enumerate(((g0_ref, gate1_s), (g1_ref, gate2_s))):
            for j in range(nb):
                zf_s[j] = _dot(fwd, z_s[j * p:(j + 1) * p, :].astype(BF16))
            skip = skip_ref[order:order + 1, :]
            for i in range(nb):
                terms = [(j, i - j + half) for j in range(nb) if 0 <= i - j + half <= nb]

                def spec_mac(r, carry, terms=terms):
                    rr = pl.multiple_of(r * SPEC_ROWS, SPEC_ROWS)
                    re = jnp.zeros((SPEC_ROWS, z_s.shape[1]), F32)
                    im = jnp.zeros((SPEC_ROWS, z_s.shape[1]), F32)
                    for j, kk in terms:
                        zre = zf_s[j, pl.ds(rr, SPEC_ROWS), :]
                        zim = zf_s[j, pl.ds(p + rr, SPEC_ROWS), :]
                        gre = g_ref[kk, pl.ds(rr, SPEC_ROWS), :]
                        gim = g_ref[kk, pl.ds(p + rr, SPEC_ROWS), :]
                        re = re + (zre * gre - zim * gim)
                        im = im + (zre * gim + zim * gre)
                    yf_s[pl.ds(rr, SPEC_ROWS), :] = re
                    yf_s[pl.ds(p + rr, SPEC_ROWS), :] = im
                    return carry

                lax.fori_loop(0, p // SPEC_ROWS, spec_mac, 0)
                dc = jnp.zeros((1, z_s.shape[1]), F32)
                ny = jnp.zeros((1, z_s.shape[1]), F32)
                for j, kk in terms:
                    dc = dc + zf_s[j, 0:1, :] * g_ref[kk, 0:1, :]
                    ny = ny + zf_s[j, p:p + 1, :] * g_ref[kk, p:p + 1, :]
                yf_s[0:1, :] = dc
                yf_s[p:p + 1, :] = ny
                y = _dot(inv, yf_s[...].astype(BF16))
                rows = slice(i * p, (i + 1) * p)
                z_s[rows, :] = gate_s[rows, :] * (y + skip * z_s[rows, :])
        o_ref[bi] = z_s[...]


def _hyena(uh, conv_w, conv_b, skip, spectra, fwd, inv, p, bb):
    b, s, _ = uh.shape
    dc = HYENA_LANES
    nct = D_MODEL // dc
    nb = s // p
    conv_b = conv_b.reshape(1, -1)

    def part(k):
        return pl.BlockSpec((bb, s, dc), lambda c, i, k=k: (i, 0, k * nct + c))

    def wpart(k, rows):
        return pl.BlockSpec((rows, dc), lambda c, i, k=k: (0, k * nct + c))

    def gpart(order):
        return pl.BlockSpec((nb + 1, 2 * p, dc), lambda c, i, order=order: (0, 0, order * nct + c),
                            pipeline_mode=pl.Buffered(1))

    in_specs = [part(0), part(1), part(2), wpart(0, 3), wpart(1, 3), wpart(2, 3),
                wpart(0, 1), wpart(1, 1), wpart(2, 1),
                pl.BlockSpec((2, dc), lambda c, i: (0, c)),
                gpart(0), gpart(1), _const_spec(fwd.shape), _const_spec(inv.shape)]
    return pl.pallas_call(
        functools.partial(_hyena_kernel, seq=s, p=p),
        grid=(nct, b // bb),
        in_specs=in_specs,
        out_specs=pl.BlockSpec((bb, s, dc), lambda c, i: (i, 0, c)),
        out_shape=jax.ShapeDtypeStruct((b, s, D_MODEL), F32),
        scratch_shapes=[pltpu.VMEM((s, dc), F32), pltpu.VMEM((s, dc), F32), pltpu.VMEM((s, dc), F32),
                        pltpu.VMEM((nb, 2 * p, dc), F32), pltpu.VMEM((2 * p, dc), F32)],
        compiler_params=_params(("arbitrary", "arbitrary")),
        name=f"hyena_{s}",
    )(uh, uh, uh, conv_w, conv_w, conv_w, conv_b, conv_b, conv_b, skip, spectra, spectra, fwd, inv)


def _tail_kernel(*refs, has_pos):
    if has_pos:
        x_ref, pos_ref = refs[:2]
        refs = refs[2:]
    else:
        x_ref = refs[0]
        refs = refs[1:]
    (h_ref, o_ref, gm_ref, gh_ref, yh_ref, modb_ref, modc_ref, mnw_ref, n2w_ref, fnw_ref,
     wbm_ref, wbh_ref, wout_ref, w1_ref, b1_ref, w2_ref, b2_ref, y_ref) = refs
    x = x_ref[0]
    if has_pos:
        x = x + pos_ref[...]
    g1 = modb_ref[0, :, 0:D_MODEL]
    sh2 = modb_ref[0, :, D_MODEL:2 * D_MODEL]
    sc2 = modc_ref[0, :, 0:D_MODEL]
    g2 = modc_ref[0, :, D_MODEL:2 * D_MODEL]
    h = h_ref[0]
    heads = [_rms(h[:, hd * HEAD_DIM:(hd + 1) * HEAD_DIM]) for hd in range(N_HEADS)]
    hm = jnp.concatenate(heads, axis=-1) * mnw_ref[...] * _sigmoid(o_ref[0])
    merged = (_sigmoid(gm_ref[0]) * _dot(hm.astype(BF16), wbm_ref[...])
              + _sigmoid(gh_ref[0]) * _dot(yh_ref[0].astype(BF16), wbh_ref[...]))
    x1 = x + g1 * _dot(merged.astype(BF16), wout_ref[...])
    hn2 = ((_rms(x1) * n2w_ref[...]) * (1.0 + sc2) + sh2).astype(BF16)
    ff = b2_ref[...]
    for kc in range(D_FF // D_MODEL):
        sl = slice(kc * D_MODEL, (kc + 1) * D_MODEL)
        a = jnp.maximum(_dot(hn2, w1_ref[:, sl]) + b1_ref[:, sl], 0.0)
        ff = ff + _dot((a * a).astype(BF16), w2_ref[sl, :])
    x2 = x1 + g2 * ff
    y_ref[0] = _rms(x2) * fnw_ref[...]


def _tail(x, pos, h, o, gmh, yh, mod3, mod_row, mnw, n2w, fnw, wbm, wbh, wout, w1, b1, w2, b2):
    b, s, _ = x.shape
    tm = TOKEN_TILE
    has_pos = pos is not None
    tok = pl.BlockSpec((1, tm, D_MODEL), lambda i, j: (i, j, 0))
    in_specs = [tok]
    args = [x]
    if has_pos:
        in_specs.append(pl.BlockSpec((tm, D_MODEL), lambda i, j: (j, 0)))
        args.append(pos)
    in_specs += [tok, tok, tok, pl.BlockSpec((1, tm, D_MODEL), lambda i, j: (i, j, 1)), tok,
                 pl.BlockSpec((1, 1, 2 * D_MODEL), lambda i, j: (mod_row(i), 0, 1)),
                 pl.BlockSpec((1, 1, 2 * D_MODEL), lambda i, j: (mod_row(i), 0, 2))]
    args += [h, o, gmh, gmh, yh, mod3, mod3]
    consts = [mnw, n2w, fnw, wbm, wbh, wout, w1, b1, w2, b2]
    in_specs += [_const_spec(a.shape) for a in consts]
    args += consts
    return pl.pallas_call(
        functools.partial(_tail_kernel, has_pos=has_pos),
        grid=(b, s // tm),
        in_specs=in_specs,
        out_specs=tok,
        out_shape=jax.ShapeDtypeStruct((b, s, D_MODEL), F32),
        compiler_params=_params(("arbitrary", "arbitrary")),
        name="tail_pos" if has_pos else "tail",
    )(*args)


def _gate_layouts(gates):
    b, s, _ = gates.shape
    g = gates.reshape(b, s, 4, N_HEADS)
    gcol = jnp.transpose(g, (0, 3, 1, 2))
    grow = jnp.transpose(g.reshape(b, s // CHUNK, CHUNK, 4, N_HEADS), (0, 4, 1, 3, 2))
    return gcol, grow


def kernel(x_prompt, x_sample, state_mlstm_C, state_mlstm_n, state_mlstm_m, c, c_ctx, w_ada, b_ada, norm1_w,
           w_in, b_in, hy_conv_w, hy_conv_b, filt_w1, filt_b1, filt_freq1, filt_w2, filt_b2, filt_freq2,
           filt_w3, hy_skip, mlstm_norm_w, w_br_m, w_br_h, w_out, norm2_w, w_mlp1, b_mlp1, w_mlp2, b_mlp2,
           final_norm_w):
    depth = w_ada.shape[0]
    assert depth == 1, "single-layer configuration"
    l = 0
    dec_b, dec_s, _ = x_sample.shape
    ctx_s = x_prompt.shape[1]

    n_rows = -(-(1 + dec_b) // 8) * 8
    cond = jnp.concatenate([c_ctx[None, :], c, jnp.zeros((n_rows - 1 - dec_b, D_MODEL), F32)], axis=0)
    mod = _modulation(cond, w_ada[l], b_ada[l])
    mod3 = mod.reshape(n_rows, 1, 6 * D_MODEL)

    w = w_in[l]
    bias = b_in[l]
    wa = w[:, :QKVO_COLS].astype(BF16)
    ba = bias[:QKVO_COLS].reshape(1, -1)
    wg32 = w[:, QKVO_COLS:QKVO_COLS + N_GATES]
    wg_hi = wg32.astype(BF16)
    wg = jnp.concatenate([wg_hi, (wg32 - wg_hi.astype(F32)).astype(BF16)], axis=1)
    bg = bias[QKVO_COLS:QKVO_COLS + N_GATES].reshape(1, -1)
    wr = w[:, QKVO_COLS + N_GATES:].astype(BF16)
    br = bias[QKVO_COLS + N_GATES:].reshape(1, -1)
    n1w = norm1_w[l].reshape(1, -1)
    tail_w = (mlstm_norm_w[l].reshape(1, -1), norm2_w[l].reshape(1, -1), final_norm_w.reshape(1, -1),
              w_br_m[l].astype(BF16), w_br_h[l].astype(BF16), w_out[l].astype(BF16),
              w_mlp1[l].astype(BF16), b_mlp1[l].reshape(1, -1), w_mlp2[l].astype(BF16), b_mlp2[l].reshape(1, -1))
    filt = (filt_w1[l], filt_b1[l], filt_freq1[l], filt_w2[l], filt_b2[l], filt_freq2[l], filt_w3[l])

    pos = _pos_table(dec_s)

    def run(x, pos_tab, mod_row, init, emit_state, p, bb):
        s = x.shape[1]
        fwd32, inv32, sign = (jnp.asarray(a) for a in _dft_mats(p))
        spectra = _filter_spectra(s, p, *filt, fwd32, sign)
        qkv, o, gates, uh, gmh = _in_projection(x, pos_tab, mod3, mod_row, n1w, wa, ba, wg, bg, wr, br)
        gcol, grow = _gate_layouts(gates)
        ml = _mlstm(qkv, gcol, grow, init, emit_state)
        yh = _hyena(uh, hy_conv_w[l], hy_conv_b[l], hy_skip[l], spectra,
                    fwd32.astype(BF16), inv32.astype(BF16), p, bb)
        y = _tail(x, pos_tab, ml[0], o, gmh, yh, mod3, mod_row, *tail_w)
        return y, ml[1:]

    y_prompt, (st_c, st_n, st_m) = run(x_prompt, None, lambda i: 0, None, True, ctx_s // 2, 4)
    init = (state_mlstm_C,
            jnp.transpose(state_mlstm_n[:, l], (0, 2, 1, 3)),
            jnp.transpose(state_mlstm_m[:, l], (0, 2, 1))[..., None])
    y_sample, _ = run(x_sample, pos, lambda i: i + 1, init, False, dec_s // 4, 1)

    new_state_n = jnp.transpose(st_n, (0, 2, 1, 3))[:, None]
    new_state_m = jnp.transpose(st_m[..., 0], (0, 2, 1))[:, None]
    return (y_prompt, y_sample, st_c, new_state_n, new_state_m)
```

```python
import functools
import math

import numpy as np
import jax
import jax.numpy as jnp
from jax import lax
from jax.experimental import pallas as pl
from jax.experimental.pallas import tpu as pltpu

F32 = jnp.float32
BF16 = jnp.bfloat16
HIGHEST = lax.Precision.HIGHEST

D_MODEL = 1024
N_HEADS = 4
HEAD_DIM = 256
D_FF = 4 * D_MODEL
GRID_W = 64
FILT_BANDS = 16
FILT_WIDTH = 64
HYENA_MIN_DECAY = math.log(1e-2) / 1.5
HYENA_MAX_DECAY = math.log(1e-2) / 0.3
CHUNK = 128
RMS_EPS = 1e-6
N_GATES = 4 * N_HEADS
QKVO_COLS = 4 * D_MODEL
REST_COLS = 5 * D_MODEL

V7X_VMEM_BYTES = 64 * 1024 * 1024
VMEM_LIMIT = V7X_VMEM_BYTES - 8 * 1024 * 1024

TOKEN_TILE = 256
HYENA_LANES = 256
FILTER_LANES = 256
SPEC_ROWS = 32


def _params(sem):
    return pltpu.CompilerParams(dimension_semantics=sem, vmem_limit_bytes=VMEM_LIMIT)


def _const_spec(shape):
    nd = len(shape)
    return pl.BlockSpec(shape, lambda *_: (0,) * nd, pipeline_mode=pl.Buffered(1))


def _sigmoid(x):
    return 1.0 / (1.0 + jnp.exp(-x))


def _rms(x):
    return x * lax.rsqrt(jnp.mean(x * x, axis=-1, keepdims=True) + RMS_EPS)


def _dot(a, b):
    return jnp.dot(a, b, preferred_element_type=F32)


def _mod_kernel(c_ref, w_ref, b_ref, o_ref):
    c = c_ref[...]
    s = c * _sigmoid(c)
    o_ref[...] = jnp.dot(s, w_ref[...], precision=HIGHEST, preferred_element_type=F32) + b_ref[...]


def _modulation(cond, w_ada, b_ada):
    rows = cond.shape[0]
    n = w_ada.shape[1]
    tn = 1536
    return pl.pallas_call(
        _mod_kernel,
        grid=(n // tn,),
        in_specs=[pl.BlockSpec((rows, D_MODEL), lambda j: (0, 0)),
                  pl.BlockSpec((D_MODEL, tn), lambda j: (0, j)),
                  pl.BlockSpec((1, tn), lambda j: (0, j))],
        out_specs=pl.BlockSpec((rows, tn), lambda j: (0, j)),
        out_shape=jax.ShapeDtypeStruct((rows, n), F32),
        compiler_params=_params(("arbitrary",)),
        name="modulation",
    )(cond, w_ada, b_ada.reshape(1, n))


def _pos_kernel(o_ref, *, rows):
    quarter = D_MODEL // 4
    half = D_MODEL // 2
    k = lax.broadcasted_iota(jnp.int32, (1, quarter), 1).astype(F32)
    omega = jnp.exp(k * (-math.log(10000.0) / quarter))

    def axis_embed(n):
        p = lax.broadcasted_iota(jnp.int32, (n, 1), 0).astype(F32)
        a = p * omega
        return jnp.concatenate([jnp.sin(a), jnp.cos(a)], axis=-1)

    er = axis_embed(rows)
    ec = axis_embed(GRID_W)
    o_ref[:, :, 0:half] = jnp.broadcast_to(er[:, None, :], (rows, GRID_W, half))
    o_ref[:, :, half:D_MODEL] = jnp.broadcast_to(ec[None, :, :], (rows, GRID_W, half))


def _pos_table(n_tokens):
    rows = n_tokens // GRID_W
    out = pl.pallas_call(
        functools.partial(_pos_kernel, rows=rows),
        out_shape=jax.ShapeDtypeStruct((rows, GRID_W, D_MODEL), F32),
        compiler_params=pltpu.CompilerParams(vmem_limit_bytes=VMEM_LIMIT),
        name="pos_table",
    )()
    return out.reshape(n_tokens, D_MODEL)


def _dft_mats(p):
    n = 2 * p
    idx = np.arange(p, dtype=np.float64)
    ang = 2.0 * np.pi * np.outer(idx, idx) / n
    alt = np.where(np.arange(p) % 2 == 0, 1.0, -1.0)
    fwd = np.zeros((n, p))
    fwd[:p] = np.cos(ang)
    fwd[p] = alt
    fwd[p + 1:] = -np.sin(ang[1:])
    inv = np.zeros((p, n))
    inv[:, :p] = 2.0 * np.cos(ang) / n
    inv[:, 0] = 1.0 / n
    inv[:, p] = alt / n
    inv[:, p + 1:] = -2.0 * np.sin(ang[:, 1:]) / n
    sign = np.concatenate([alt, alt])
    sign[p] = 1.0
    return fwd.astype(np.float32), inv.astype(np.float32), sign.astype(np.float32).reshape(n, 1)


def _filter_kernel(w1t_ref, w1c_ref, w1s_ref, b1_ref, fr1_ref, w2_ref, b2_ref, fr2_ref, w3_ref,
                   fhi_ref, flo_ref, sign_ref, g_ref, feat_ref, *, seq, p, kk0):
    nb = seq // p
    tn = w3_ref.shape[1]
    step = pl.program_id(0)
    idx = lax.broadcasted_iota(jnp.int32, (seq, 1), 0).astype(F32)

    @pl.when(step == 0)
    def _():
        t = idx / float(seq - 1)
        bands = (lax.broadcasted_iota(jnp.int32, (1, FILT_BANDS), 1) + 1).astype(F32)
        ang = ((2.0 * math.pi / seq) * idx) * bands
        pre = (t * w1t_ref[...]
               + jnp.dot(jnp.cos(ang), w1c_ref[...], precision=HIGHEST, preferred_element_type=F32)
               + jnp.dot(jnp.sin(ang), w1s_ref[...], precision=HIGHEST, preferred_element_type=F32)
               + b1_ref[...])
        h1 = jnp.sin(fr1_ref[...] * pre)
        h2 = jnp.sin(fr2_ref[...] * (jnp.dot(h1, w2_ref[...], precision=HIGHEST, preferred_element_type=F32)
                                     + b2_ref[...]))
        feat_ref[...] = h2

    hh = jnp.dot(feat_ref[...], w3_ref[...], precision=HIGHEST, preferred_element_type=F32)
    d0 = lax.rem(step * tn, D_MODEL)
    d = (d0 + lax.broadcasted_iota(jnp.int32, (1, tn), 1)).astype(F32)
    delta = jnp.abs(HYENA_MIN_DECAY + (HYENA_MAX_DECAY - HYENA_MIN_DECAY) * d / float(D_MODEL - 1))
    centre = seq // 2
    dist = jnp.abs(idx - float(centre)) / float(centre)
    hh = hh * jnp.exp(-dist * delta)
    hh = hh / (jnp.sum(jnp.abs(hh), axis=0, keepdims=True) + 1e-6)

    fhi = fhi_ref[...]
    flo = flo_ref[...]
    off = (seq // 2) % p
    if off:
        zeros_hi = jnp.zeros((p - off, tn), F32)
        zeros_lo = jnp.zeros((off, tn), F32)
        blocks = [jnp.concatenate([zeros_hi, hh[0:off]], axis=0)]
        blocks += [hh[off + (k - 1) * p:off + k * p] for k in range(1, nb)]
        blocks += [jnp.concatenate([hh[off + (nb - 1) * p:seq], zeros_lo], axis=0)]
    else:
        blocks = [hh[k * p:(k + 1) * p] for k in range(nb)]
    spec = []
    for hb in blocks:
        hi = hb.astype(BF16)
        lo = (hb - hi.astype(F32)).astype(BF16)
        spec.append(_dot(fhi, hi) + _dot(fhi, lo) + _dot(flo, hi))
    sign = sign_ref[...]
    for k in range(g_ref.shape[0]):
        kk = kk0 + k
        if kk == 0:
            g = spec[0]
        elif kk == len(spec):
            g = sign * spec[kk - 1]
        else:
            g = spec[kk] + sign * spec[kk - 1]
        g_ref[k] = g


def _spectra_layout(seq, p):
    nb = seq // p
    centre = seq // 2
    nblk = nb + (1 if centre % p else 0)
    shift = centre // p + (1 if centre % p else 0)
    kk_lo = max(shift - (nb - 1), 0)
    kk_hi = min(shift + (nb - 1), nblk)
    return shift, kk_lo, kk_hi - kk_lo + 1


def _filter_spectra(seq, p, w1, b1, fr1, w2, b2, fr2, w3, fwd32, sign):
    tn = FILTER_LANES
    ncols = w3.shape[1]
    _, kk0, ng = _spectra_layout(seq, p)
    fhi = fwd32.astype(BF16)
    flo = (fwd32 - fhi.astype(F32)).astype(BF16)
    args = (w1[0:1], w1[1:1 + FILT_BANDS], w1[1 + FILT_BANDS:], b1.reshape(1, -1), fr1.reshape(1, -1),
            w2, b2.reshape(1, -1), fr2.reshape(1, -1))
    in_specs = [_const_spec(a.shape) for a in args]
    in_specs += [pl.BlockSpec((FILT_WIDTH, tn), lambda j: (0, j)),
                 _const_spec(fhi.shape), _const_spec(flo.shape), _const_spec(sign.shape)]
    return pl.pallas_call(
        functools.partial(_filter_kernel, seq=seq, p=p, kk0=kk0),
        grid=(ncols // tn,),
        in_specs=in_specs,
        out_specs=pl.BlockSpec((ng, 2 * p, tn), lambda j: (0, 0, j)),
        out_shape=jax.ShapeDtypeStruct((ng, 2 * p, ncols), F32),
        scratch_shapes=[pltpu.VMEM((seq, FILT_WIDTH), F32)],
        compiler_params=_params(("arbitrary",)),
        name=f"filter_spectra_{seq}",
    )(*args, w3, fhi, flo, sign)


def _inproj_kernel(*refs, has_pos):
    if has_pos:
        x_ref, pos_ref = refs[:2]
        refs = refs[2:]
    else:
        x_ref = refs[0]
        refs = refs[1:]
    (mod_ref, n1w_ref, wa_ref, ba_ref, wg_ref, bg_ref, wr_ref, br_ref,
     qkv_ref, o_ref, gates_ref, uh_ref, gmh_ref) = refs
    x = x_ref[0]
    if has_pos:
        x = x + pos_ref[...]
    sh1 = mod_ref[0, :, 0:D_MODEL]
    sc1 = mod_ref[0, :, D_MODEL:2 * D_MODEL]
    hn = (_rms(x) * n1w_ref[...]) * (1.0 + sc1) + sh1
    hb = hn.astype(BF16)
    hlo = (hn - hb.astype(F32)).astype(BF16)
    for c in range(4):
        sl = slice(c * D_MODEL, (c + 1) * D_MODEL)
        r = _dot(hb, wa_ref[:, sl]) + ba_ref[:, sl]
        if c < 3:
            qkv_ref[0, :, sl] = r.astype(BF16)
        else:
            o_ref[0] = r
    for c in range(5):
        sl = slice(c * D_MODEL, (c + 1) * D_MODEL)
        r = _dot(hb, wr_ref[:, sl]) + br_ref[:, sl]
        if c < 3:
            uh_ref[0, :, sl] = r
        else:
            gmh_ref[0, :, (c - 3) * D_MODEL:(c - 2) * D_MODEL] = r
    p1 = _dot(hb, wg_ref[...])
    p2 = _dot(hlo, wg_ref[...])
    gates_ref[0] = p1[:, 0:N_GATES] + p1[:, N_GATES:2 * N_GATES] + p2[:, 0:N_GATES] + bg_ref[...]


def _in_projection(x, pos, mod3, mod_row, n1w, wa, ba, wg, bg, wr, br):
    b, s, _ = x.shape
    tm = TOKEN_TILE
    has_pos = pos is not None
    in_specs = [pl.BlockSpec((1, tm, D_MODEL), lambda i, j: (i, j, 0))]
    args = [x]
    if has_pos:
        in_specs.append(pl.BlockSpec((tm, D_MODEL), lambda i, j: (j, 0)))
        args.append(pos)
    in_specs += [pl.BlockSpec((1, 1, 2 * D_MODEL), lambda i, j: (mod_row(i), 0, 0)),
                 _const_spec(n1w.shape), _const_spec(wa.shape), _const_spec(ba.shape),
                 _const_spec(wg.shape), _const_spec(bg.shape), _const_spec(wr.shape), _const_spec(br.shape)]
    args += [mod3, n1w, wa, ba, wg, bg, wr, br]
    out_shape = (jax.ShapeDtypeStruct((b, s, 3 * D_MODEL), BF16),
                 jax.ShapeDtypeStruct((b, s, D_MODEL), F32),
                 jax.ShapeDtypeStruct((b, s, N_GATES), F32),
                 jax.ShapeDtypeStruct((b, s, 3 * D_MODEL), F32),
                 jax.ShapeDtypeStruct((b, s, 2 * D_MODEL), F32))
    out_specs = tuple(pl.BlockSpec((1, tm, sh.shape[2]), lambda i, j: (i, j, 0)) for sh in out_shape)
    return pl.pallas_call(
        functools.partial(_inproj_kernel, has_pos=has_pos),
        grid=(b, s // tm),
        in_specs=in_specs,
        out_specs=out_specs,
        out_shape=out_shape,
        compiler_params=_params(("arbitrary", "arbitrary")),
        name="in_projection_pos" if has_pos else "in_projection",
    )(*args)


def _log_sigmoid(x):
    return jnp.minimum(x, 0.0) - jnp.log(1.0 + jnp.exp(-jnp.abs(x)))


def _mlstm_kernel(*refs, seq, has_init, emit_state):
    q_ref, k_ref, v_ref, gcol_ref, grow_ref = refs[:5]
    refs = refs[5:]
    if has_init:
        c0_ref, n0_ref, m0_ref = refs[:3]
        refs = refs[3:]
    h_ref = refs[0]
    refs = refs[1:]
    if emit_state:
        c_out, n_out, m_out = refs[:3]
        refs = refs[3:]
    c_s, n_s = refs
    t = CHUNK
    nc = seq // t
    scale = HEAD_DIM ** -0.5

    for dr in range(2):
        if has_init:
            c_s[dr] = c0_ref[0, 0, dr, 0]
            n_s[dr] = n0_ref[0, 0, dr:dr + 1, :]
        else:
            c_s[dr] = jnp.zeros((HEAD_DIM, HEAD_DIM), F32)
            n_s[dr] = jnp.zeros((1, HEAD_DIM), F32)
    h_ref[...] = jnp.zeros(h_ref.shape, F32)

    row = lax.broadcasted_iota(jnp.int32, (t, t), 0)
    col = lax.broadcasted_iota(jnp.int32, (t, t), 1)
    lower = col <= row
    upper = col >= row

    def chunk_step(c, dr, m):
        r0 = pl.multiple_of(c * t, t)
        q = q_ref[0, pl.ds(r0, t), :]
        k = k_ref[0, pl.ds(r0, t), :]
        v = v_ref[0, pl.ds(r0, t), :]
        gc = gcol_ref[0, 0, pl.ds(r0, t), :]
        gr = grow_ref[0, 0, c]
        li_col = gc[:, 2 * dr:2 * dr + 1]
        lf_col = _log_sigmoid(gc[:, 2 * dr + 1:2 * dr + 2])
        li_row = gr[2 * dr:2 * dr + 1, :]
        lf_row = _log_sigmoid(gr[2 * dr + 1:2 * dr + 2, :])
        mask = lower if dr == 0 else upper
        mask_t = upper if dr == 0 else lower
        b_col = jnp.sum(jnp.where(mask, lf_row, 0.0), axis=1, keepdims=True)
        b_row = jnp.sum(jnp.where(mask_t, lf_col, 0.0), axis=0, keepdims=True)
        b_last = jnp.sum(lf_row, axis=1, keepdims=True)
        d = jnp.where(mask, b_col - b_row + li_row, -jnp.inf)
        m_inter = b_col + m
        m_comb = jnp.maximum(m_inter, jnp.max(d, axis=1, keepdims=True))
        pw = jnp.exp(d - m_comb)
        s = lax.dot_general(q, k, (((1,), (1,)), ((), ())), preferred_element_type=F32) * scale * pw
        w_inter = jnp.exp(m_inter - m_comb)
        c_prev = c_s[dr]
        n_prev = n_s[dr]
        qc = _dot(q, c_prev.astype(BF16)) * scale
        sv = _dot(s.astype(BF16), v)
        num = w_inter * qc + sv
        qn = jnp.sum(q.astype(F32) * n_prev, axis=1, keepdims=True) * scale
        den = w_inter * qn + jnp.sum(s, axis=1, keepdims=True)
        h = num / jnp.maximum(jnp.abs(den), jnp.exp(-m_comb))
        h_ref[0, pl.ds(r0, t), :] += h
        a = b_last - b_col + li_col
        m_new = jnp.maximum(b_last + m, jnp.max(a, axis=0, keepdims=True))
        w_state = jnp.exp(a - m_new)
        decay = jnp.exp(b_last + m - m_new)
        kw = k.astype(F32) * w_state
        kv = lax.dot_general(kw.astype(BF16), v, (((0,), (0,)), ((), ())), preferred_element_type=F32)
        c_s[dr] = decay * c_prev + kv
        n_s[dr] = decay * n_prev + jnp.sum(kw, axis=0, keepdims=True)
        return m_new

    def body(i, carry):
        mf, mb = carry
        mf = chunk_step(i, 0, mf)
        mb = chunk_step(nc - 1 - i, 1, mb)
        return mf, mb

    if has_init:
        m_init = (m0_ref[0, 0, 0:1, :], m0_ref[0, 0, 1:2, :])
    else:
        m_init = (jnp.zeros((1, 1), F32), jnp.zeros((1, 1), F32))
    mf, mb = lax.fori_loop(0, nc, body, m_init)

    if emit_state:
        for dr, m in ((0, mf), (1, mb)):
            c_out[0, 0, dr, 0] = c_s[dr]
            n_out[0, 0, dr:dr + 1, :] = n_s[dr]
            m_out[0, 0, dr:dr + 1, :] = jnp.broadcast_to(m, (1, 128))


def _mlstm(qkv, gcol, grow, init, emit_state):
    b, s, _ = qkv.shape
    nc = s // CHUNK
    has_init = init is not None
    in_specs = [pl.BlockSpec((1, s, HEAD_DIM), lambda i, h: (i, 0, h)),
                pl.BlockSpec((1, s, HEAD_DIM), lambda i, h: (i, 0, N_HEADS + h)),
                pl.BlockSpec((1, s, HEAD_DIM), lambda i, h: (i, 0, 2 * N_HEADS + h)),
                pl.BlockSpec((1, 1, s, 4), lambda i, h: (i, h, 0, 0)),
                pl.BlockSpec((1, 1, nc, 4, CHUNK), lambda i, h: (i, h, 0, 0, 0))]
    args = [qkv, qkv, qkv, gcol, grow]
    if has_init:
        c0, n0, m0 = init
        in_specs += [pl.BlockSpec((1, 1, 2, 1, HEAD_DIM, HEAD_DIM), lambda i, h: (i, 0, 0, h, 0, 0)),
                     pl.BlockSpec((1, 1, 2, HEAD_DIM), lambda i, h: (i, h, 0, 0)),
                     pl.BlockSpec((1, 1, 2, 1), lambda i, h: (i, h, 0, 0))]
        args += [c0, n0, m0]
    out_shape = [jax.ShapeDtypeStruct((b, s, D_MODEL), F32)]
    out_specs = [pl.BlockSpec((1, s, HEAD_DIM), lambda i, h: (i, 0, h))]
    if emit_state:
        out_shape += [jax.ShapeDtypeStruct((b, 1, 2, N_HEADS, HEAD_DIM, HEAD_DIM), F32),
                      jax.ShapeDtypeStruct((b, N_HEADS, 2, HEAD_DIM), F32),
                      jax.ShapeDtypeStruct((b, N_HEADS, 2, 128), F32)]
        out_specs += [pl.BlockSpec((1, 1, 2, 1, HEAD_DIM, HEAD_DIM), lambda i, h: (i, 0, 0, h, 0, 0)),
                      pl.BlockSpec((1, 1, 2, HEAD_DIM), lambda i, h: (i, h, 0, 0)),
                      pl.BlockSpec((1, 1, 2, 128), lambda i, h: (i, h, 0, 0))]
    return pl.pallas_call(
        functools.partial(_mlstm_kernel, seq=s, has_init=has_init, emit_state=emit_state),
        grid=(b, N_HEADS),
        in_specs=in_specs,
        out_specs=tuple(out_specs),
        out_shape=tuple(out_shape),
        scratch_shapes=[pltpu.VMEM((2, HEAD_DIM, HEAD_DIM), F32), pltpu.VMEM((2, 1, HEAD_DIM), F32)],
        compiler_params=_params(("arbitrary", "arbitrary")),
        name=f"mlstm_{s}",
    )(*args)


def _loop(n, body, static_max=4):
    if n <= static_max:
        for i in range(n):
            body(i)
    else:
        def step(i, carry):
            body(i)
            return carry
        lax.fori_loop(0, n, step, 0)


def _hyena_kernel(x1_ref, x2_ref, v_ref, w1_ref, w2_ref, wv_ref, b1_ref, b2_ref, bv_ref, skip_ref,
                  g0_ref, g1_ref, fwd_ref, inv_ref, o_ref, zf_s, yf_s, *, seq, p, shift, kk0):
    nb = seq // p
    ng = g0_ref.shape[0]
    bb = x1_ref.shape[0]
    lanes = o_ref.shape[-1]
    sub = lax.broadcasted_iota(jnp.int32, (8, 1), 0)

    def conv_rows(src_ref, bi, r, w_ref, b_ref):
        if isinstance(r, int):
            rr = r * p
            up = src_ref[bi, rr - 8:rr, :][7:8, :] if rr > 0 else jnp.zeros((1, lanes), F32)
            dn = src_ref[bi, rr + p:rr + p + 8, :][0:1, :] if rr + p < seq else jnp.zeros((1, lanes), F32)
        else:
            rr = pl.multiple_of(r * p, p)
            above = pl.multiple_of(jnp.maximum(rr - 8, 0), 8)
            below = pl.multiple_of(jnp.minimum(rr + p, seq - 8), 8)
            up = jnp.where(rr > 0, src_ref[bi, pl.ds(above, 8), :][7:8, :], 0.0)
            dn = jnp.where(rr + p < seq, src_ref[bi, pl.ds(below, 8), :][0:1, :], 0.0)
        cur = src_ref[bi, pl.ds(rr, p), :]
        prev = pltpu.roll(cur, 1, 0)
        prev = jnp.concatenate([jnp.where(sub == 0, up, prev[0:8]), prev[8:]], axis=0)
        nxt = pltpu.roll(cur, p - 1, 0)
        nxt = jnp.concatenate([nxt[:p - 8], jnp.where(sub == 7, dn, nxt[p - 8:])], axis=0)
        return b_ref[...] + prev * w_ref[0:1, :] + cur * w_ref[1:2, :] + nxt * w_ref[2:3, :]

    fwd = fwd_ref[...]
    inv = inv_ref[...]
    terms = [[(j, i - j + shift - kk0) for j in range(nb) if 0 <= i - j + shift - kk0 < ng] for i in range(nb)]
    for bi in range(bb):
        def conv_block(j, bi=bi):
            rows = pl.ds(j * p, p) if isinstance(j, int) else pl.ds(pl.multiple_of(j * p, p), p)
            z = conv_rows(v_ref, bi, j, wv_ref, bv_ref)
            o_ref[bi, rows, :] = z
            zf_s[bi, j] = _dot(fwd, z.astype(BF16))

        _loop(nb, conv_block)
        for order, (g_ref, x_ref, w_ref, b_ref) in enumerate(((g0_ref, x1_ref, w1_ref, b1_ref),
                                                               (g1_ref, x2_ref, w2_ref, b2_ref))):
            if order > 0:
                for j in range(nb):
                    zf_s[bi, j] = _dot(fwd, o_ref[bi, j * p:(j + 1) * p, :].astype(BF16))

            def spec_mac(r, bi=bi, g_ref=g_ref):
                if isinstance(r, int):
                    rr = r * SPEC_ROWS
                else:
                    rr = pl.multiple_of(r * SPEC_ROWS, SPEC_ROWS)
                for i in range(nb):
                    re = jnp.zeros((SPEC_ROWS, lanes), F32)
                    im = jnp.zeros((SPEC_ROWS, lanes), F32)
                    for j, kk in terms[i]:
                        zre = zf_s[bi, j, pl.ds(rr, SPEC_ROWS), :]
                        zim = zf_s[bi, j, pl.ds(p + rr, SPEC_ROWS), :]
                        gre = g_ref[kk, pl.ds(rr, SPEC_ROWS), :]
                        gim = g_ref[kk, pl.ds(p + rr, SPEC_ROWS), :]
                        re = re + (zre * gre - zim * gim)
                        im = im + (zre * gim + zim * gre)
                    yf_s[bi, i, pl.ds(rr, SPEC_ROWS), :] = re
                    yf_s[bi, i, pl.ds(p + rr, SPEC_ROWS), :] = im

            _loop(p // SPEC_ROWS, spec_mac, static_max=8)
            skip = skip_ref[order:order + 1, :]
            for i in range(nb):
                dc = jnp.zeros((1, lanes), F32)
                ny = jnp.zeros((1, lanes), F32)
                for j, kk in terms[i]:
                    dc = dc + zf_s[bi, j, 0:1, :] * g_ref[kk, 0:1, :]
                    ny = ny + zf_s[bi, j, p:p + 1, :] * g_ref[kk, p:p + 1, :]
                yf_s[bi, i, 0:1, :] = dc
                yf_s[bi, i, p:p + 1, :] = ny
            for i in range(nb):
                y = _dot(inv, yf_s[bi, i].astype(BF16))
                rows = slice(i * p, (i + 1) * p)
                gate = conv_rows(x_ref, bi, i, w_ref, b_ref)
                o_ref[bi, rows, :] = gate * (y + skip * o_ref[bi, rows, :])


def _hyena(uh, conv_w, conv_b, skip, spectra, fwd, inv, p, bb):
    b, s, _ = uh.shape
    dc = HYENA_LANES
    nct = D_MODEL // dc
    nb = s // p
    shift, kk0, ng = _spectra_layout(s, p)
    conv_b = conv_b.reshape(1, -1)

    def part(k):
        return pl.BlockSpec((bb, s, dc), lambda c, i, k=k: (i, 0, k * nct + c))

    def wpart(k, rows):
        return pl.BlockSpec((rows, dc), lambda c, i, k=k: (0, k * nct + c))

    def gpart(order):
        return pl.BlockSpec((ng, 2 * p, dc), lambda c, i, order=order: (0, 0, order * nct + c),
                            pipeline_mode=pl.Buffered(1))

    in_specs = [part(0), part(1), part(2), wpart(0, 3), wpart(1, 3), wpart(2, 3),
                wpart(0, 1), wpart(1, 1), wpart(2, 1),
                pl.BlockSpec((2, dc), lambda c, i: (0, c)),
                gpart(0), gpart(1), _const_spec(fwd.shape), _const_spec(inv.shape)]
    return pl.pallas_call(
        functools.partial(_hyena_kernel, seq=s, p=p, shift=shift, kk0=kk0),
        grid=(nct, b // bb),
        in_specs=in_specs,
        out_specs=pl.BlockSpec((bb, s, dc), lambda c, i: (i, 0, c)),
        out_shape=jax.ShapeDtypeStruct((b, s, D_MODEL), F32),
        scratch_shapes=[pltpu.VMEM((bb, nb, 2 * p, dc), F32), pltpu.VMEM((bb, nb, 2 * p, dc), F32)],
        compiler_params=_params(("arbitrary", "arbitrary")),
        name=f"hyena_{s}",
    )(uh, uh, uh, conv_w, conv_w, conv_w, conv_b, conv_b, conv_b, skip, spectra, spectra, fwd, inv)


def _tail_kernel(*refs, has_pos):
    if has_pos:
        x_ref, pos_ref = refs[:2]
        refs = refs[2:]
    else:
        x_ref = refs[0]
        refs = refs[1:]
    (h_ref, o_ref, gm_ref, gh_ref, yh_ref, modb_ref, modc_ref, mnw_ref, n2w_ref, fnw_ref,
     wbm_ref, wbh_ref, wout_ref, w1_ref, b1_ref, w2_ref, b2_ref, y_ref) = refs
    x = x_ref[0]
    if has_pos:
        x = x + pos_ref[...]
    g1 = modb_ref[0, :, 0:D_MODEL]
    sh2 = modb_ref[0, :, D_MODEL:2 * D_MODEL]
    sc2 = modc_ref[0, :, 0:D_MODEL]
    g2 = modc_ref[0, :, D_MODEL:2 * D_MODEL]
    h = h_ref[0]
    heads = [_rms(h[:, hd * HEAD_DIM:(hd + 1) * HEAD_DIM]) for hd in range(N_HEADS)]
    hm = jnp.concatenate(heads, axis=-1) * mnw_ref[...] * _sigmoid(o_ref[0])
    merged = (_sigmoid(gm_ref[0]) * _dot(hm.astype(BF16), wbm_ref[...])
              + _sigmoid(gh_ref[0]) * _dot(yh_ref[0].astype(BF16), wbh_ref[...]))
    x1 = x + g1 * _dot(merged.astype(BF16), wout_ref[...])
    hn2 = ((_rms(x1) * n2w_ref[...]) * (1.0 + sc2) + sh2).astype(BF16)
    ff = b2_ref[...]
    for kc in range(D_FF // D_MODEL):
        sl = slice(kc * D_MODEL, (kc + 1) * D_MODEL)
        a = jnp.maximum(_dot(hn2, w1_ref[:, sl]) + b1_ref[:, sl], 0.0)
        ff = ff + _dot((a * a).astype(BF16), w2_ref[sl, :])
    x2 = x1 + g2 * ff
    y_ref[0] = _rms(x2) * fnw_ref[...]


def _tail(x, pos, h, o, gmh, yh, mod3, mod_row, mnw, n2w, fnw, wbm, wbh, wout, w1, b1, w2, b2):
    b, s, _ = x.shape
    tm = TOKEN_TILE
    has_pos = pos is not None
    tok = pl.BlockSpec((1, tm, D_MODEL), lambda i, j: (i, j, 0))
    in_specs = [tok]
    args = [x]
    if has_pos:
        in_specs.append(pl.BlockSpec((tm, D_MODEL), lambda i, j: (j, 0)))
        args.append(pos)
    in_specs += [tok, tok, tok, pl.BlockSpec((1, tm, D_MODEL), lambda i, j: (i, j, 1)), tok,
                 pl.BlockSpec((1, 1, 2 * D_MODEL), lambda i, j: (mod_row(i), 0, 1)),
                 pl.BlockSpec((1, 1, 2 * D_MODEL), lambda i, j: (mod_row(i), 0, 2))]
    args += [h, o, gmh, gmh, yh, mod3, mod3]
    consts = [mnw, n2w, fnw, wbm, wbh, wout, w1, b1, w2, b2]
    in_specs += [_const_spec(a.shape) for a in consts]
    args += consts
    return pl.pallas_call(
        functools.partial(_tail_kernel, has_pos=has_pos),
        grid=(b, s // tm),
        in_specs=in_specs,
        out_specs=tok,
        out_shape=jax.ShapeDtypeStruct((b, s, D_MODEL), F32),
        compiler_params=_params(("arbitrary", "arbitrary")),
        name="tail_pos" if has_pos else "tail",
    )(*args)


def _gate_layouts(gates):
    b, s, _ = gates.shape
    g = gates.reshape(b, s, 4, N_HEADS)
    gcol = jnp.transpose(g, (0, 3, 1, 2))
    grow = jnp.transpose(g.reshape(b, s // CHUNK, CHUNK, 4, N_HEADS), (0, 4, 1, 3, 2))
    return gcol, grow


def kernel(x_prompt, x_sample, state_mlstm_C, state_mlstm_n, state_mlstm_m, c, c_ctx, w_ada, b_ada, norm1_w,
           w_in, b_in, hy_conv_w, hy_conv_b, filt_w1, filt_b1, filt_freq1, filt_w2, filt_b2, filt_freq2,
           filt_w3, hy_skip, mlstm_norm_w, w_br_m, w_br_h, w_out, norm2_w, w_mlp1, b_mlp1, w_mlp2, b_mlp2,
           final_norm_w):
    depth = w_ada.shape[0]
    assert depth == 1, "single-layer configuration"
    l = 0
    dec_b, dec_s, _ = x_sample.shape
    ctx_s = x_prompt.shape[1]

    n_rows = -(-(1 + dec_b) // 8) * 8
    cond = jnp.concatenate([c_ctx[None, :], c, jnp.zeros((n_rows - 1 - dec_b, D_MODEL), F32)], axis=0)
    mod = _modulation(cond, w_ada[l], b_ada[l])
    mod3 = mod.reshape(n_rows, 1, 6 * D_MODEL)

    w = w_in[l]
    bias = b_in[l]
    wa = w[:, :QKVO_COLS].astype(BF16)
    ba = bias[:QKVO_COLS].reshape(1, -1)
    wg32 = w[:, QKVO_COLS:QKVO_COLS + N_GATES]
    wg_hi = wg32.astype(BF16)
    wg = jnp.concatenate([wg_hi, (wg32 - wg_hi.astype(F32)).astype(BF16)], axis=1)
    bg = bias[QKVO_COLS:QKVO_COLS + N_GATES].reshape(1, -1)
    wr = w[:, QKVO_COLS + N_GATES:].astype(BF16)
    br = bias[QKVO_COLS + N_GATES:].reshape(1, -1)
    n1w = norm1_w[l].reshape(1, -1)
    tail_w = (mlstm_norm_w[l].reshape(1, -1), norm2_w[l].reshape(1, -1), final_norm_w.reshape(1, -1),
              w_br_m[l].astype(BF16), w_br_h[l].astype(BF16), w_out[l].astype(BF16),
              w_mlp1[l].astype(BF16), b_mlp1[l].reshape(1, -1), w_mlp2[l].astype(BF16), b_mlp2[l].reshape(1, -1))
    filt = (filt_w1[l], filt_b1[l], filt_freq1[l], filt_w2[l], filt_b2[l], filt_freq2[l], filt_w3[l])

    pos = _pos_table(dec_s)

    def run(x, pos_tab, mod_row, init, emit_state, p, bb):
        s = x.shape[1]
        fwd32, inv32, sign = (jnp.asarray(a) for a in _dft_mats(p))
        spectra = _filter_spectra(s, p, *filt, fwd32, sign)
        qkv, o, gates, uh, gmh = _in_projection(x, pos_tab, mod3, mod_row, n1w, wa, ba, wg, bg, wr, br)
        gcol, grow = _gate_layouts(gates)
        ml = _mlstm(qkv, gcol, grow, init, emit_state)
        yh = _hyena(uh, hy_conv_w[l], hy_conv_b[l], hy_skip[l], spectra,
                    fwd32.astype(BF16), inv32.astype(BF16), p, bb)
        y = _tail(x, pos_tab, ml[0], o, gmh, yh, mod3, mod_row, *tail_w)
        return y, ml[1:]

    y_prompt, (st_c, st_n, st_m) = run(x_prompt, None, lambda i: 0, None, True, ctx_s, 4)
    init = (state_mlstm_C,
            jnp.transpose(state_mlstm_n[:, l], (0, 2, 1, 3)),
            jnp.transpose(state_mlstm_m[:, l], (0, 2, 1))[..., None])
    y_sample, _ = run(x_sample, pos, lambda i: i + 1, init, False, dec_s // 4, 1)

    new_state_n = jnp.transpose(st_n, (0, 2, 1, 3))[:, None]
    new_state_m = jnp.transpose(st_m[..., 0], (0, 2, 1))[:, None]
    return (y_prompt, y_sample, st_c, new_state_n, new_state_m)
```

```python
import functools
import math

import numpy as np
import jax
import jax.numpy as jnp
from jax import lax
from jax.experimental import pallas as pl
from jax.experimental.pallas import tpu as pltpu

F32 = jnp.float32
BF16 = jnp.bfloat16
HIGHEST = lax.Precision.HIGHEST

D_MODEL = 1024
N_HEADS = 4
HEAD_DIM = 256
D_FF = 4 * D_MODEL
GRID_W = 64
FILT_BANDS = 16
FILT_WIDTH = 64
HYENA_MIN_DECAY = math.log(1e-2) / 1.5
HYENA_MAX_DECAY = math.log(1e-2) / 0.3
CHUNK = 128
RMS_EPS = 1e-6
N_GATES = 4 * N_HEADS
QKVO_COLS = 4 * D_MODEL
REST_COLS = 5 * D_MODEL

V7X_VMEM_BYTES = 64 * 1024 * 1024
VMEM_LIMIT = V7X_VMEM_BYTES - 8 * 1024 * 1024

TOKEN_TILE = 256
HYENA_LANES = 256
FILTER_LANES = 256
SPEC_ROWS = 32


def _params(sem):
    return pltpu.CompilerParams(dimension_semantics=sem, vmem_limit_bytes=VMEM_LIMIT)


def _const_spec(shape):
    nd = len(shape)
    return pl.BlockSpec(shape, lambda *_: (0,) * nd, pipeline_mode=pl.Buffered(1))


def _sigmoid(x):
    return 1.0 / (1.0 + jnp.exp(-x))


def _rms(x):
    return x * lax.rsqrt(jnp.mean(x * x, axis=-1, keepdims=True) + RMS_EPS)


def _dot(a, b):
    return jnp.dot(a, b, preferred_element_type=F32)


def _mod_kernel(c_ref, w_ref, b_ref, o_ref):
    c = c_ref[...]
    s = c * _sigmoid(c)
    o_ref[...] = jnp.dot(s, w_ref[...], precision=HIGHEST, preferred_element_type=F32) + b_ref[...]


def _modulation(cond, w_ada, b_ada):
    rows = cond.shape[0]
    n = w_ada.shape[1]
    tn = 1536
    return pl.pallas_call(
        _mod_kernel,
        grid=(n // tn,),
        in_specs=[pl.BlockSpec((rows, D_MODEL), lambda j: (0, 0)),
                  pl.BlockSpec((D_MODEL, tn), lambda j: (0, j)),
                  pl.BlockSpec((1, tn), lambda j: (0, j))],
        out_specs=pl.BlockSpec((rows, tn), lambda j: (0, j)),
        out_shape=jax.ShapeDtypeStruct((rows, n), F32),
        compiler_params=_params(("arbitrary",)),
        name="modulation",
    )(cond, w_ada, b_ada.reshape(1, n))


def _pos_kernel(o_ref, *, rows):
    quarter = D_MODEL // 4
    half = D_MODEL // 2
    k = lax.broadcasted_iota(jnp.int32, (1, quarter), 1).astype(F32)
    omega = jnp.exp(k * (-math.log(10000.0) / quarter))

    def axis_embed(n):
        p = lax.broadcasted_iota(jnp.int32, (n, 1), 0).astype(F32)
        a = p * omega
        return jnp.concatenate([jnp.sin(a), jnp.cos(a)], axis=-1)

    er = axis_embed(rows)
    ec = axis_embed(GRID_W)
    o_ref[:, :, 0:half] = jnp.broadcast_to(er[:, None, :], (rows, GRID_W, half))
    o_ref[:, :, half:D_MODEL] = jnp.broadcast_to(ec[None, :, :], (rows, GRID_W, half))


def _pos_table(n_tokens):
    rows = n_tokens // GRID_W
    out = pl.pallas_call(
        functools.partial(_pos_kernel, rows=rows),
        out_shape=jax.ShapeDtypeStruct((rows, GRID_W, D_MODEL), F32),
        compiler_params=pltpu.CompilerParams(vmem_limit_bytes=VMEM_LIMIT),
        name="pos_table",
    )()
    return out.reshape(n_tokens, D_MODEL)


def _dft_mats(p):
    n = 2 * p
    idx = np.arange(p, dtype=np.float64)
    ang = 2.0 * np.pi * np.outer(idx, idx) / n
    alt = np.where(np.arange(p) % 2 == 0, 1.0, -1.0)
    fwd = np.zeros((n, p))
    fwd[:p] = np.cos(ang)
    fwd[p] = alt
    fwd[p + 1:] = -np.sin(ang[1:])
    inv = np.zeros((p, n))
    inv[:, :p] = 2.0 * np.cos(ang) / n
    inv[:, 0] = 1.0 / n
    inv[:, p] = alt / n
    inv[:, p + 1:] = -2.0 * np.sin(ang[:, 1:]) / n
    sign = np.concatenate([alt, alt])
    sign[p] = 1.0
    return fwd.astype(np.float32), inv.astype(np.float32), sign.astype(np.float32).reshape(n, 1)


def _filter_kernel(w1t_ref, w1c_ref, w1s_ref, b1_ref, fr1_ref, w2_ref, b2_ref, fr2_ref, w3_ref,
                   fhi_ref, flo_ref, sign_ref, g_ref, feat_ref, *, seq, p, kk0):
    nb = seq // p
    tn = w3_ref.shape[1]
    step = pl.program_id(0)
    idx = lax.broadcasted_iota(jnp.int32, (seq, 1), 0).astype(F32)

    @pl.when(step == 0)
    def _():
        t = idx / float(seq - 1)
        bands = (lax.broadcasted_iota(jnp.int32, (1, FILT_BANDS), 1) + 1).astype(F32)
        ang = ((2.0 * math.pi / seq) * idx) * bands
        pre = (t * w1t_ref[...]
               + jnp.dot(jnp.cos(ang), w1c_ref[...], precision=HIGHEST, preferred_element_type=F32)
               + jnp.dot(jnp.sin(ang), w1s_ref[...], precision=HIGHEST, preferred_element_type=F32)
               + b1_ref[...])
        h1 = jnp.sin(fr1_ref[...] * pre)
        h2 = jnp.sin(fr2_ref[...] * (jnp.dot(h1, w2_ref[...], precision=HIGHEST, preferred_element_type=F32)
                                     + b2_ref[...]))
        feat_ref[...] = h2

    hh = jnp.dot(feat_ref[...], w3_ref[...], precision=HIGHEST, preferred_element_type=F32)
    d0 = lax.rem(step * tn, D_MODEL)
    d = (d0 + lax.broadcasted_iota(jnp.int32, (1, tn), 1)).astype(F32)
    delta = jnp.abs(HYENA_MIN_DECAY + (HYENA_MAX_DECAY - HYENA_MIN_DECAY) * d / float(D_MODEL - 1))
    centre = seq // 2
    dist = jnp.abs(idx - float(centre)) / float(centre)
    hh = hh * jnp.exp(-dist * delta)
    hh = hh / (jnp.sum(jnp.abs(hh), axis=0, keepdims=True) + 1e-6)

    fhi = fhi_ref[...]
    flo = flo_ref[...]
    off = (seq // 2) % p
    if off:
        zeros_hi = jnp.zeros((p - off, tn), F32)
        zeros_lo = jnp.zeros((off, tn), F32)
        blocks = [jnp.concatenate([zeros_hi, hh[0:off]], axis=0)]
        blocks += [hh[off + (k - 1) * p:off + k * p] for k in range(1, nb)]
        blocks += [jnp.concatenate([hh[off + (nb - 1) * p:seq], zeros_lo], axis=0)]
    else:
        blocks = [hh[k * p:(k + 1) * p] for k in range(nb)]
    spec = []
    for hb in blocks:
        hi = hb.astype(BF16)
        lo = (hb - hi.astype(F32)).astype(BF16)
        spec.append(_dot(fhi, hi) + _dot(fhi, lo) + _dot(flo, hi))
    sign = sign_ref[...]
    for k in range(g_ref.shape[0]):
        kk = kk0 + k
        if kk == 0:
            g = spec[0]
        elif kk == len(spec):
            g = sign * spec[kk - 1]
        else:
            g = spec[kk] + sign * spec[kk - 1]
        g_ref[k] = g


def _spectra_layout(seq, p):
    nb = seq // p
    centre = seq // 2
    nblk = nb + (1 if centre % p else 0)
    shift = centre // p + (1 if centre % p else 0)
    kk_lo = max(shift - (nb - 1), 0)
    kk_hi = min(shift + (nb - 1), nblk)
    return shift, kk_lo, kk_hi - kk_lo + 1


def _filter_spectra(seq, p, w1, b1, fr1, w2, b2, fr2, w3, fwd32, sign):
    tn = FILTER_LANES
    ncols = w3.shape[1]
    _, kk0, ng = _spectra_layout(seq, p)
    fhi = fwd32.astype(BF16)
    flo = (fwd32 - fhi.astype(F32)).astype(BF16)
    args = (w1[0:1], w1[1:1 + FILT_BANDS], w1[1 + FILT_BANDS:], b1.reshape(1, -1), fr1.reshape(1, -1),
            w2, b2.reshape(1, -1), fr2.reshape(1, -1))
    in_specs = [_const_spec(a.shape) for a in args]
    in_specs += [pl.BlockSpec((FILT_WIDTH, tn), lambda j: (0, j)),
                 _const_spec(fhi.shape), _const_spec(flo.shape), _const_spec(sign.shape)]
    return pl.pallas_call(
        functools.partial(_filter_kernel, seq=seq, p=p, kk0=kk0),
        grid=(ncols // tn,),
        in_specs=in_specs,
        out_specs=pl.BlockSpec((ng, 2 * p, tn), lambda j: (0, 0, j)),
        out_shape=jax.ShapeDtypeStruct((ng, 2 * p, ncols), F32),
        scratch_shapes=[pltpu.VMEM((seq, FILT_WIDTH), F32)],
        compiler_params=_params(("arbitrary",)),
        name=f"filter_spectra_{seq}",
    )(*args, w3, fhi, flo, sign)


def _inproj_kernel(*refs, has_pos):
    if has_pos:
        x_ref, pos_ref = refs[:2]
        refs = refs[2:]
    else:
        x_ref = refs[0]
        refs = refs[1:]
    (mod_ref, n1w_ref, wa_ref, ba_ref, wg_ref, bg_ref, wr_ref, br_ref,
     qkv_ref, o_ref, gates_ref, uh_ref, gmh_ref) = refs
    x = x_ref[0]
    if has_pos:
        x = x + pos_ref[...]
    sh1 = mod_ref[0, :, 0:D_MODEL]
    sc1 = mod_ref[0, :, D_MODEL:2 * D_MODEL]
    hn = (_rms(x) * n1w_ref[...]) * (1.0 + sc1) + sh1
    hb = hn.astype(BF16)
    hlo = (hn - hb.astype(F32)).astype(BF16)
    for c in range(4):
        sl = slice(c * D_MODEL, (c + 1) * D_MODEL)
        r = _dot(hb, wa_ref[:, sl]) + ba_ref[:, sl]
        if c < 3:
            qkv_ref[0, :, sl] = r.astype(BF16)
        else:
            o_ref[0] = r
    for c in range(5):
        sl = slice(c * D_MODEL, (c + 1) * D_MODEL)
        r = _dot(hb, wr_ref[:, sl]) + br_ref[:, sl]
        if c < 3:
            uh_ref[0, :, sl] = r
        else:
            gmh_ref[0, :, (c - 3) * D_MODEL:(c - 2) * D_MODEL] = r
    p1 = _dot(hb, wg_ref[...])
    p2 = _dot(hlo, wg_ref[...])
    gates_ref[0] = p1[:, 0:N_GATES] + p1[:, N_GATES:2 * N_GATES] + p2[:, 0:N_GATES] + bg_ref[...]


def _in_projection(x, pos, mod3, mod_row, n1w, wa, ba, wg, bg, wr, br):
    b, s, _ = x.shape
    tm = TOKEN_TILE
    has_pos = pos is not None
    in_specs = [pl.BlockSpec((1, tm, D_MODEL), lambda i, j: (i, j, 0))]
    args = [x]
    if has_pos:
        in_specs.append(pl.BlockSpec((tm, D_MODEL), lambda i, j: (j, 0)))
        args.append(pos)
    in_specs += [pl.BlockSpec((1, 1, 2 * D_MODEL), lambda i, j: (mod_row(i), 0, 0)),
                 _const_spec(n1w.shape), _const_spec(wa.shape), _const_spec(ba.shape),
                 _const_spec(wg.shape), _const_spec(bg.shape), _const_spec(wr.shape), _const_spec(br.shape)]
    args += [mod3, n1w, wa, ba, wg, bg, wr, br]
    out_shape = (jax.ShapeDtypeStruct((b, s, 3 * D_MODEL), BF16),
                 jax.ShapeDtypeStruct((b, s, D_MODEL), F32),
                 jax.ShapeDtypeStruct((b, s, N_GATES), F32),
                 jax.ShapeDtypeStruct((b, s, 3 * D_MODEL), F32),
                 jax.ShapeDtypeStruct((b, s, 2 * D_MODEL), F32))
    out_specs = tuple(pl.BlockSpec((1, tm, sh.shape[2]), lambda i, j: (i, j, 0)) for sh in out_shape)
    return pl.pallas_call(
        functools.partial(_inproj_kernel, has_pos=has_pos),
        grid=(b, s // tm),
        in_specs=in_specs,
        out_specs=out_specs,
        out_shape=out_shape,
        compiler_params=_params(("arbitrary", "arbitrary")),
        name="in_projection_pos" if has_pos else "in_projection",
    )(*args)


def _log_sigmoid(x):
    return jnp.minimum(x, 0.0) - jnp.log(1.0 + jnp.exp(-jnp.abs(x)))


def _split3(x):
    hi = x.astype(BF16).astype(F32)
    mid = (x - hi).astype(BF16).astype(F32)
    lo = (x - hi - mid).astype(BF16).astype(F32)
    return hi, mid, lo


ROW_E, ROW_B, ROW_M_INTER, ROW_W_STATE, ROW_DECAY, N_ROW_KINDS = 0, 3, 6, 7, 8, 9


def _mlstm_kernel(*refs, seq, has_init, emit_state):
    q_ref, k_ref, v_ref, grow_ref = refs[:4]
    refs = refs[4:]
    if has_init:
        c0_ref, n0_ref, m0_ref = refs[:3]
        refs = refs[3:]
    h_ref = refs[0]
    refs = refs[1:]
    if emit_state:
        c_out, n_out, m_out = refs[:3]
        refs = refs[3:]
    c_s, n_s, rows_s = refs
    t = CHUNK
    nc = seq // t
    scale = HEAD_DIM ** -0.5

    r_io = lax.broadcasted_iota(jnp.int32, (t, t), 0)
    c_io = lax.broadcasted_iota(jnp.int32, (t, t), 1)
    chunk_id = lax.broadcasted_iota(jnp.int32, (nc, 1), 0)

    m_final = []
    for dr in range(2):
        if has_init:
            c_s[dr] = c0_ref[0, 0, dr, 0]
            n_s[dr] = n0_ref[0, 0, dr:dr + 1, :]
            m = m0_ref[0, 0, dr:dr + 1, :]
        else:
            c_s[dr] = jnp.zeros((HEAD_DIM, HEAD_DIM), F32)
            n_s[dr] = jnp.zeros((1, HEAD_DIM), F32)
            m = jnp.zeros((1, 1), F32)
        li = grow_ref[0, 0, 2 * dr]
        lf = _log_sigmoid(grow_ref[0, 0, 2 * dr + 1])
        tri = ((r_io <= c_io) if dr == 0 else (r_io >= c_io)).astype(BF16)
        b = sum(_dot(part.astype(BF16), tri) for part in _split3(lf))
        b_last = jnp.sum(lf, axis=1, keepdims=True)
        a = b_last - b + li
        a_max = jnp.max(a, axis=1, keepdims=True)
        m_before = jnp.zeros((nc, 1), F32)
        m_after = jnp.zeros((nc, 1), F32)
        for c in (range(nc) if dr == 0 else reversed(range(nc))):
            m_new = jnp.maximum(b_last[c:c + 1, :] + m, a_max[c:c + 1, :])
            m_before = jnp.where(chunk_id == c, m, m_before)
            m_after = jnp.where(chunk_id == c, m_new, m_after)
            m = m_new
        m_final.append(m)
        kinds = (*_split3(li - b), *_split3(b), b + m_before, jnp.exp(a - m_after),
                 jnp.broadcast_to(jnp.exp(b_last + m_before - m_after), (nc, t)))
        for idx, x in enumerate(kinds):
            rows_s[dr, idx] = x

    sub8 = lax.broadcasted_iota(jnp.int32, (8, t), 0)
    sub16 = lax.broadcasted_iota(jnp.int32, (16, 1), 0)

    def pick3(base, dr, c, lo, ones_lo, ones_hi):
        out = jnp.where((sub8 >= ones_lo) & (sub8 < ones_hi), 1.0, 0.0)
        for i in range(3):
            out = jnp.where(sub8 == lo + i, rows_s[dr, base + i, pl.ds(c, 1), :], out)
        return out

    def hi_lo_rows(x):
        hi = x.astype(BF16).astype(F32)
        lo = x - hi
        return jnp.where(sub16 == 0, hi, jnp.where(sub16 == 1, lo, 0.0)).astype(BF16)

    def chunk_step(c, dr, accumulate):
        r0 = c * t if isinstance(c, int) else pl.multiple_of(c * t, t)
        q = q_ref[0, pl.ds(r0, t), :]
        k = k_ref[0, pl.ds(r0, t), :]
        v = v_ref[0, pl.ds(r0, t), :]
        d_t = lax.dot_general(pick3(ROW_E, dr, c, 0, 3, 6), pick3(ROW_B, dr, c, 3, 0, 3),
                              (((0,), (0,)), ((), ())), preferred_element_type=F32)
        allowed = (r_io <= c_io) if dr == 0 else (r_io >= c_io)
        d_t = jnp.where(allowed, d_t, -jnp.inf)
        m_inter = rows_s[dr, ROW_M_INTER, pl.ds(c, 1), :]
        m_comb = jnp.maximum(m_inter, jnp.max(d_t, axis=0, keepdims=True))
        pw = jnp.exp(d_t - m_comb)
        w_inter = jnp.exp(m_inter - m_comb)
        c_prev = c_s[dr]
        n_prev = n_s[dr]
        kq = lax.dot_general(jnp.concatenate([k, hi_lo_rows(n_prev)], axis=0), q,
                             (((1,), (1,)), ((), ())), preferred_element_type=F32)
        sp = kq[0:t] * pw
        qn = kq[t:t + 1] + kq[t + 1:t + 2]
        den = scale * (w_inter * qn + jnp.sum(sp, axis=0, keepdims=True))
        inv_den = scale / jnp.maximum(jnp.abs(den), jnp.exp(-m_comb))
        inter_col = jnp.transpose(jnp.where(sub8 == 0, w_inter * inv_den, 0.0))[:, 0:1]
        sv = lax.dot_general((sp * inv_den).astype(BF16), v, (((0,), (0,)), ((), ())),
                             preferred_element_type=F32)
        h = sv + inter_col * _dot(q, c_prev.astype(BF16))
        if accumulate:
            h_ref[0, pl.ds(r0, t), :] += h
        else:
            h_ref[0, pl.ds(r0, t), :] = h
        w_state = rows_s[dr, ROW_W_STATE, pl.ds(c, 1), :]
        decay = rows_s[dr, ROW_DECAY, pl.ds(c, 1), :][:, 0:1]
        kw = (k.T.astype(F32) * w_state).astype(BF16)
        c_s[dr] = decay * c_prev + _dot(kw, v)
        nk = _dot(hi_lo_rows(w_state), k)
        n_s[dr] = decay * n_prev + nk[0:1] + nk[1:2]

    def both(i, accumulate):
        chunk_step(i, 0, accumulate)
        chunk_step(nc - 1 - i, 1, accumulate)

    half = nc // 2
    if nc <= 16:
        for i in range(nc):
            both(i, i >= half)
    else:
        lax.fori_loop(0, half, lambda i, carry: (both(i, False), carry)[1], 0)
        lax.fori_loop(half, nc, lambda i, carry: (both(i, True), carry)[1], 0)

    if emit_state:
        for dr in range(2):
            c_out[0, 0, dr, 0] = c_s[dr]
            n_out[0, 0, dr:dr + 1, :] = n_s[dr]
            m_out[0, 0, dr:dr + 1, :] = jnp.broadcast_to(m_final[dr], (1, 128))


def _mlstm(qkv, grow, init, emit_state):
    b, s, _ = qkv.shape
    nc = s // CHUNK
    has_init = init is not None
    in_specs = [pl.BlockSpec((1, s, HEAD_DIM), lambda i, h: (i, 0, h)),
                pl.BlockSpec((1, s, HEAD_DIM), lambda i, h: (i, 0, N_HEADS + h)),
                pl.BlockSpec((1, s, HEAD_DIM), lambda i, h: (i, 0, 2 * N_HEADS + h)),
                pl.BlockSpec((1, 1, 4, nc, CHUNK), lambda i, h: (i, h, 0, 0, 0))]
    args = [qkv, qkv, qkv, grow]
    if has_init:
        c0, n0, m0 = init
        in_specs += [pl.BlockSpec((1, 1, 2, 1, HEAD_DIM, HEAD_DIM), lambda i, h: (i, 0, 0, h, 0, 0)),
                     pl.BlockSpec((1, 1, 2, HEAD_DIM), lambda i, h: (i, h, 0, 0)),
                     pl.BlockSpec((1, 1, 2, 1), lambda i, h: (i, h, 0, 0))]
        args += [c0, n0, m0]
    out_shape = [jax.ShapeDtypeStruct((b, s, D_MODEL), F32)]
    out_specs = [pl.BlockSpec((1, s, HEAD_DIM), lambda i, h: (i, 0, h))]
    if emit_state:
        out_shape += [jax.ShapeDtypeStruct((b, 1, 2, N_HEADS, HEAD_DIM, HEAD_DIM), F32),
                      jax.ShapeDtypeStruct((b, N_HEADS, 2, HEAD_DIM), F32),
                      jax.ShapeDtypeStruct((b, N_HEADS, 2, 128), F32)]
        out_specs += [pl.BlockSpec((1, 1, 2, 1, HEAD_DIM, HEAD_DIM), lambda i, h: (i, 0, 0, h, 0, 0)),
                      pl.BlockSpec((1, 1, 2, HEAD_DIM), lambda i, h: (i, h, 0, 0)),
                      pl.BlockSpec((1, 1, 2, 128), lambda i, h: (i, h, 0, 0))]
    return pl.pallas_call(
        functools.partial(_mlstm_kernel, seq=s, has_init=has_init, emit_state=emit_state),
        grid=(b, N_HEADS),
        in_specs=in_specs,
        out_specs=tuple(out_specs),
        out_shape=tuple(out_shape),
        scratch_shapes=[pltpu.VMEM((2, HEAD_DIM, HEAD_DIM), F32), pltpu.VMEM((2, 1, HEAD_DIM), F32),
                        pltpu.VMEM((2, N_ROW_KINDS, nc, CHUNK), F32)],
        compiler_params=_params(("arbitrary", "arbitrary")),
        name=f"mlstm_{s}",
    )(*args)


def _loop(n, body, static_max=4):
    if n <= static_max:
        for i in range(n):
            body(i)
    else:
        def step(i, carry):
            body(i)
            return carry
        lax.fori_loop(0, n, step, 0)


def _hyena_kernel(x1_ref, x2_ref, v_ref, w1_ref, w2_ref, wv_ref, b1_ref, b2_ref, bv_ref, skip_ref,
                  g0_ref, g1_ref, fwd_ref, inv_ref, o_ref, zf_s, yf_s, *, seq, p, shift, kk0):
    nb = seq // p
    ng = g0_ref.shape[0]
    bb = x1_ref.shape[0]
    lanes = o_ref.shape[-1]
    sub = lax.broadcasted_iota(jnp.int32, (8, 1), 0)

    def conv_rows(src_ref, bi, r, w_ref, b_ref):
        if isinstance(r, int):
            rr = r * p
            up = src_ref[bi, rr - 8:rr, :][7:8, :] if rr > 0 else jnp.zeros((1, lanes), F32)
            dn = src_ref[bi, rr + p:rr + p + 8, :][0:1, :] if rr + p < seq else jnp.zeros((1, lanes), F32)
        else:
            rr = pl.multiple_of(r * p, p)
            above = pl.multiple_of(jnp.maximum(rr - 8, 0), 8)
            below = pl.multiple_of(jnp.minimum(rr + p, seq - 8), 8)
            up = jnp.where(rr > 0, src_ref[bi, pl.ds(above, 8), :][7:8, :], 0.0)
            dn = jnp.where(rr + p < seq, src_ref[bi, pl.ds(below, 8), :][0:1, :], 0.0)
        cur = src_ref[bi, pl.ds(rr, p), :]
        prev = pltpu.roll(cur, 1, 0)
        prev = jnp.concatenate([jnp.where(sub == 0, up, prev[0:8]), prev[8:]], axis=0)
        nxt = pltpu.roll(cur, p - 1, 0)
        nxt = jnp.concatenate([nxt[:p - 8], jnp.where(sub == 7, dn, nxt[p - 8:])], axis=0)
        return b_ref[...] + prev * w_ref[0:1, :] + cur * w_ref[1:2, :] + nxt * w_ref[2:3, :]

    fwd = fwd_ref[...]
    inv = inv_ref[...]
    terms = [[(j, i - j + shift - kk0) for j in range(nb) if 0 <= i - j + shift - kk0 < ng] for i in range(nb)]
    for bi in range(bb):
        def conv_block(j, bi=bi):
            rows = pl.ds(j * p, p) if isinstance(j, int) else pl.ds(pl.multiple_of(j * p, p), p)
            z = conv_rows(v_ref, bi, j, wv_ref, bv_ref)
            o_ref[bi, rows, :] = z
            zf_s[bi, j] = _dot(fwd, z.astype(BF16))

        _loop(nb, conv_block)
        for order, (g_ref, x_ref, w_ref, b_ref) in enumerate(((g0_ref, x1_ref, w1_ref, b1_ref),
                                                               (g1_ref, x2_ref, w2_ref, b2_ref))):
            if order > 0:
                for j in range(nb):
                    zf_s[bi, j] = _dot(fwd, o_ref[bi, j * p:(j + 1) * p, :].astype(BF16))

            def spec_mac(r, bi=bi, g_ref=g_ref):
                if isinstance(r, int):
                    rr = r * SPEC_ROWS
                else:
                    rr = pl.multiple_of(r * SPEC_ROWS, SPEC_ROWS)
                for i in range(nb):
                    re = jnp.zeros((SPEC_ROWS, lanes), F32)
                    im = jnp.zeros((SPEC_ROWS, lanes), F32)
                    for j, kk in terms[i]:
                        zre = zf_s[bi, j, pl.ds(rr, SPEC_ROWS), :]
                        zim = zf_s[bi, j, pl.ds(p + rr, SPEC_ROWS), :]
                        gre = g_ref[kk, pl.ds(rr, SPEC_ROWS), :]
                        gim = g_ref[kk, pl.ds(p + rr, SPEC_ROWS), :]
                        re = re + (zre * gre - zim * gim)
                        im = im + (zre * gim + zim * gre)
                    yf_s[bi, i, pl.ds(rr, SPEC_ROWS), :] = re
                    yf_s[bi, i, pl.ds(p + rr, SPEC_ROWS), :] = im

            _loop(p // SPEC_ROWS, spec_mac, static_max=8)
            skip = skip_ref[order:order + 1, :]
            for i in range(nb):
                dc = jnp.zeros((1, lanes), F32)
                ny = jnp.zeros((1, lanes), F32)
                for j, kk in terms[i]:
                    dc = dc + zf_s[bi, j, 0:1, :] * g_ref[kk, 0:1, :]
                    ny = ny + zf_s[bi, j, p:p + 1, :] * g_ref[kk, p:p + 1, :]
                yf_s[bi, i, 0:1, :] = dc
                yf_s[bi, i, p:p + 1, :] = ny
            for i in range(nb):
                y = _dot(inv, yf_s[bi, i].astype(BF16))
                rows = slice(i * p, (i + 1) * p)
                gate = conv_rows(x_ref, bi, i, w_ref, b_ref)
                o_ref[bi, rows, :] = gate * (y + skip * o_ref[bi, rows, :])


def _hyena(uh, conv_w, conv_b, skip, spectra, fwd, inv, p, bb):
    b, s, _ = uh.shape
    dc = HYENA_LANES
    nct = D_MODEL // dc
    nb = s // p
    shift, kk0, ng = _spectra_layout(s, p)
    conv_b = conv_b.reshape(1, -1)

    def part(k):
        return pl.BlockSpec((bb, s, dc), lambda c, i, k=k: (i, 0, k * nct + c))

    def wpart(k, rows):
        return pl.BlockSpec((rows, dc), lambda c, i, k=k: (0, k * nct + c))

    def gpart(order):
        return pl.BlockSpec((ng, 2 * p, dc), lambda c, i, order=order: (0, 0, order * nct + c),
                            pipeline_mode=pl.Buffered(1))

    in_specs = [part(0), part(1), part(2), wpart(0, 3), wpart(1, 3), wpart(2, 3),
                wpart(0, 1), wpart(1, 1), wpart(2, 1),
                pl.BlockSpec((2, dc), lambda c, i: (0, c)),
                gpart(0), gpart(1), _const_spec(fwd.shape), _const_spec(inv.shape)]
    return pl.pallas_call(
        functools.partial(_hyena_kernel, seq=s, p=p, shift=shift, kk0=kk0),
        grid=(nct, b // bb),
        in_specs=in_specs,
        out_specs=pl.BlockSpec((bb, s, dc), lambda c, i: (i, 0, c)),
        out_shape=jax.ShapeDtypeStruct((b, s, D_MODEL), F32),
        scratch_shapes=[pltpu.VMEM((bb, nb, 2 * p, dc), F32), pltpu.VMEM((bb, nb, 2 * p, dc), F32)],
        compiler_params=_params(("arbitrary", "arbitrary")),
        name=f"hyena_{s}",
    )(uh, uh, uh, conv_w, conv_w, conv_w, conv_b, conv_b, conv_b, skip, spectra, spectra, fwd, inv)


def _tail_kernel(*refs, has_pos):
    if has_pos:
        x_ref, pos_ref = refs[:2]
        refs = refs[2:]
    else:
        x_ref = refs[0]
        refs = refs[1:]
    (h_ref, o_ref, gm_ref, gh_ref, yh_ref, modb_ref, modc_ref, mnw_ref, n2w_ref, fnw_ref,
     wbm_ref, wbh_ref, wout_ref, w1_ref, b1_ref, w2_ref, b2_ref, y_ref) = refs
    x = x_ref[0]
    if has_pos:
        x = x + pos_ref[...]
    g1 = modb_ref[0, :, 0:D_MODEL]
    sh2 = modb_ref[0, :, D_MODEL:2 * D_MODEL]
    sc2 = modc_ref[0, :, 0:D_MODEL]
    g2 = modc_ref[0, :, D_MODEL:2 * D_MODEL]
    h = h_ref[0]
    heads = [_rms(h[:, hd * HEAD_DIM:(hd + 1) * HEAD_DIM]) for hd in range(N_HEADS)]
    hm = jnp.concatenate(heads, axis=-1) * mnw_ref[...] * _sigmoid(o_ref[0])
    merged = (_sigmoid(gm_ref[0]) * _dot(hm.astype(BF16), wbm_ref[...])
              + _sigmoid(gh_ref[0]) * _dot(yh_ref[0].astype(BF16), wbh_ref[...]))
    x1 = x + g1 * _dot(merged.astype(BF16), wout_ref[...])
    hn2 = ((_rms(x1) * n2w_ref[...]) * (1.0 + sc2) + sh2).astype(BF16)
    ff = b2_ref[...]
    for kc in range(D_FF // D_MODEL):
        sl = slice(kc * D_MODEL, (kc + 1) * D_MODEL)
        a = jnp.maximum(_dot(hn2, w1_ref[:, sl]) + b1_ref[:, sl], 0.0)
        ff = ff + _dot((a * a).astype(BF16), w2_ref[sl, :])
    x2 = x1 + g2 * ff
    y_ref[0] = _rms(x2) * fnw_ref[...]


def _tail(x, pos, h, o, gmh, yh, mod3, mod_row, mnw, n2w, fnw, wbm, wbh, wout, w1, b1, w2, b2):
    b, s, _ = x.shape
    tm = TOKEN_TILE
    has_pos = pos is not None
    tok = pl.BlockSpec((1, tm, D_MODEL), lambda i, j: (i, j, 0))
    in_specs = [tok]
    args = [x]
    if has_pos:
        in_specs.append(pl.BlockSpec((tm, D_MODEL), lambda i, j: (j, 0)))
        args.append(pos)
    in_specs += [tok, tok, tok, pl.BlockSpec((1, tm, D_MODEL), lambda i, j: (i, j, 1)), tok,
                 pl.BlockSpec((1, 1, 2 * D_MODEL), lambda i, j: (mod_row(i), 0, 1)),
                 pl.BlockSpec((1, 1, 2 * D_MODEL), lambda i, j: (mod_row(i), 0, 2))]
    args += [h, o, gmh, gmh, yh, mod3, mod3]
    consts = [mnw, n2w, fnw, wbm, wbh, wout, w1, b1, w2, b2]
    in_specs += [_const_spec(a.shape) for a in consts]
    args += consts
    return pl.pallas_call(
        functools.partial(_tail_kernel, has_pos=has_pos),
        grid=(b, s // tm),
        in_specs=in_specs,
        out_specs=tok,
        out_shape=jax.ShapeDtypeStruct((b, s, D_MODEL), F32),
        compiler_params=_params(("arbitrary", "arbitrary")),
        name="tail_pos" if has_pos else "tail",
    )(*args)


def _gate_rows(gates):
    b, s, _ = gates.shape
    g = gates.reshape(b, s // CHUNK, CHUNK, 4, N_HEADS)
    return jnp.transpose(g, (0, 4, 3, 1, 2))


def kernel(x_prompt, x_sample, state_mlstm_C, state_mlstm_n, state_mlstm_m, c, c_ctx, w_ada, b_ada, norm1_w,
           w_in, b_in, hy_conv_w, hy_conv_b, filt_w1, filt_b1, filt_freq1, filt_w2, filt_b2, filt_freq2,
           filt_w3, hy_skip, mlstm_norm_w, w_br_m, w_br_h, w_out, norm2_w, w_mlp1, b_mlp1, w_mlp2, b_mlp2,
           final_norm_w):
    depth = w_ada.shape[0]
    assert depth == 1, "single-layer configuration"
    l = 0
    dec_b, dec_s, _ = x_sample.shape
    ctx_s = x_prompt.shape[1]

    n_rows = -(-(1 + dec_b) // 8) * 8
    cond = jnp.concatenate([c_ctx[None, :], c, jnp.zeros((n_rows - 1 - dec_b, D_MODEL), F32)], axis=0)
    mod = _modulation(cond, w_ada[l], b_ada[l])
    mod3 = mod.reshape(n_rows, 1, 6 * D_MODEL)

    w = w_in[l]
    bias = b_in[l]
    wa = w[:, :QKVO_COLS].astype(BF16)
    ba = bias[:QKVO_COLS].reshape(1, -1)
    wg32 = w[:, QKVO_COLS:QKVO_COLS + N_GATES]
    wg_hi = wg32.astype(BF16)
    wg = jnp.concatenate([wg_hi, (wg32 - wg_hi.astype(F32)).astype(BF16)], axis=1)
    bg = bias[QKVO_COLS:QKVO_COLS + N_GATES].reshape(1, -1)
    wr = w[:, QKVO_COLS + N_GATES:].astype(BF16)
    br = bias[QKVO_COLS + N_GATES:].reshape(1, -1)
    n1w = norm1_w[l].reshape(1, -1)
    tail_w = (mlstm_norm_w[l].reshape(1, -1), norm2_w[l].reshape(1, -1), final_norm_w.reshape(1, -1),
              w_br_m[l].astype(BF16), w_br_h[l].astype(BF16), w_out[l].astype(BF16),
              w_mlp1[l].astype(BF16), b_mlp1[l].reshape(1, -1), w_mlp2[l].astype(BF16), b_mlp2[l].reshape(1, -1))
    filt = (filt_w1[l], filt_b1[l], filt_freq1[l], filt_w2[l], filt_b2[l], filt_freq2[l], filt_w3[l])

    pos = _pos_table(dec_s)

    def run(x, pos_tab, mod_row, init, emit_state, p, bb):
        s = x.shape[1]
        fwd32, inv32, sign = (jnp.asarray(a) for a in _dft_mats(p))
        spectra = _filter_spectra(s, p, *filt, fwd32, sign)
        qkv, o, gates, uh, gmh = _in_projection(x, pos_tab, mod3, mod_row, n1w, wa, ba, wg, bg, wr, br)
        ml = _mlstm(qkv, _gate_rows(gates), init, emit_state)
        yh = _hyena(uh, hy_conv_w[l], hy_conv_b[l], hy_skip[l], spectra,
                    fwd32.astype(BF16), inv32.astype(BF16), p, bb)
        y = _tail(x, pos_tab, ml[0], o, gmh, yh, mod3, mod_row, *tail_w)
        return y, ml[1:]

    y_prompt, (st_c, st_n, st_m) = run(x_prompt, None, lambda i: 0, None, True, ctx_s, 4)
    init = (state_mlstm_C,
            jnp.transpose(state_mlstm_n[:, l], (0, 2, 1, 3)),
            jnp.transpose(state_mlstm_m[:, l], (0, 2, 1))[..., None])
    y_sample, _ = run(x_sample, pos, lambda i: i + 1, init, False, dec_s // 4, 1)

    new_state_n = jnp.transpose(st_n, (0, 2, 1, 3))[:, None]
    new_state_m = jnp.transpose(st_m[..., 0], (0, 2, 1))[:, None]
    return (y_prompt, y_sample, st_c, new_state_n, new_state_m)
```

```python
import functools
import math

import numpy as np
import jax
import jax.numpy as jnp
from jax import lax
from jax.experimental import pallas as pl
from jax.experimental.pallas import tpu as pltpu

F32 = jnp.float32
BF16 = jnp.bfloat16
HIGHEST = lax.Precision.HIGHEST

D_MODEL = 1024
N_HEADS = 4
HEAD_DIM = 256
D_FF = 4 * D_MODEL
GRID_W = 64
FILT_BANDS = 16
FILT_WIDTH = 64
HYENA_MIN_DECAY = math.log(1e-2) / 1.5
HYENA_MAX_DECAY = math.log(1e-2) / 0.3
CHUNK = 128
RMS_EPS = 1e-6
N_GATES = 4 * N_HEADS
QKVO_COLS = 4 * D_MODEL
REST_COLS = 5 * D_MODEL

V7X_VMEM_BYTES = 64 * 1024 * 1024
VMEM_LIMIT = V7X_VMEM_BYTES - 8 * 1024 * 1024

TOKEN_TILE = 256
HYENA_LANES = 256
FILTER_LANES = 256
SPEC_ROWS = 32
MLSTM_HEADS_PER_STEP = 2


def _params(sem):
    return pltpu.CompilerParams(dimension_semantics=sem, vmem_limit_bytes=VMEM_LIMIT)


def _const_spec(shape):
    nd = len(shape)
    return pl.BlockSpec(shape, lambda *_: (0,) * nd, pipeline_mode=pl.Buffered(1))


def _sigmoid(x):
    return 1.0 / (1.0 + jnp.exp(-x))


def _rms(x):
    return x * lax.rsqrt(jnp.mean(x * x, axis=-1, keepdims=True) + RMS_EPS)


def _dot(a, b):
    return jnp.dot(a, b, preferred_element_type=F32)


def _mod_kernel(c_ref, w_ref, b_ref, o_ref):
    c = c_ref[...]
    s = c * _sigmoid(c)
    o_ref[...] = jnp.dot(s, w_ref[...], precision=HIGHEST, preferred_element_type=F32) + b_ref[...]


def _modulation(cond, w_ada, b_ada):
    rows = cond.shape[0]
    n = w_ada.shape[1]
    tn = 1536
    return pl.pallas_call(
        _mod_kernel,
        grid=(n // tn,),
        in_specs=[pl.BlockSpec((rows, D_MODEL), lambda j: (0, 0)),
                  pl.BlockSpec((D_MODEL, tn), lambda j: (0, j)),
                  pl.BlockSpec((1, tn), lambda j: (0, j))],
        out_specs=pl.BlockSpec((rows, tn), lambda j: (0, j)),
        out_shape=jax.ShapeDtypeStruct((rows, n), F32),
        compiler_params=_params(("arbitrary",)),
        name="modulation",
    )(cond, w_ada, b_ada.reshape(1, n))


def _pos_kernel(o_ref, *, rows):
    quarter = D_MODEL // 4
    half = D_MODEL // 2
    k = lax.broadcasted_iota(jnp.int32, (1, quarter), 1).astype(F32)
    omega = jnp.exp(k * (-math.log(10000.0) / quarter))

    def axis_embed(n):
        p = lax.broadcasted_iota(jnp.int32, (n, 1), 0).astype(F32)
        a = p * omega
        return jnp.concatenate([jnp.sin(a), jnp.cos(a)], axis=-1)

    er = axis_embed(rows)
    ec = axis_embed(GRID_W)
    o_ref[:, :, 0:half] = jnp.broadcast_to(er[:, None, :], (rows, GRID_W, half))
    o_ref[:, :, half:D_MODEL] = jnp.broadcast_to(ec[None, :, :], (rows, GRID_W, half))


def _pos_table(n_tokens):
    rows = n_tokens // GRID_W
    out = pl.pallas_call(
        functools.partial(_pos_kernel, rows=rows),
        out_shape=jax.ShapeDtypeStruct((rows, GRID_W, D_MODEL), F32),
        compiler_params=pltpu.CompilerParams(vmem_limit_bytes=VMEM_LIMIT),
        name="pos_table",
    )()
    return out.reshape(n_tokens, D_MODEL)


def _dft_mats(p):
    n = 2 * p
    idx = np.arange(p, dtype=np.float64)
    ang = 2.0 * np.pi * np.outer(idx, idx) / n
    alt = np.where(np.arange(p) % 2 == 0, 1.0, -1.0)
    fwd = np.zeros((n, p))
    fwd[:p] = np.cos(ang)
    fwd[p] = alt
    fwd[p + 1:] = -np.sin(ang[1:])
    inv = np.zeros((p, n))
    inv[:, :p] = 2.0 * np.cos(ang) / n
    inv[:, 0] = 1.0 / n
    inv[:, p] = alt / n
    inv[:, p + 1:] = -2.0 * np.sin(ang[:, 1:]) / n
    sign = np.concatenate([alt, alt])
    sign[p] = 1.0
    return fwd.astype(np.float32), inv.astype(np.float32), sign.astype(np.float32).reshape(n, 1)


def _filter_kernel(w1t_ref, w1c_ref, w1s_ref, b1_ref, fr1_ref, w2_ref, b2_ref, fr2_ref, w3_ref,
                   fhi_ref, flo_ref, sign_ref, g_ref, feat_ref, *, seq, p, kk0):
    nb = seq // p
    tn = w3_ref.shape[1]
    step = pl.program_id(0)
    idx = lax.broadcasted_iota(jnp.int32, (seq, 1), 0).astype(F32)

    @pl.when(step == 0)
    def _():
        t = idx / float(seq - 1)
        bands = (lax.broadcasted_iota(jnp.int32, (1, FILT_BANDS), 1) + 1).astype(F32)
        ang = ((2.0 * math.pi / seq) * idx) * bands
        pre = (t * w1t_ref[...]
               + jnp.dot(jnp.cos(ang), w1c_ref[...], precision=HIGHEST, preferred_element_type=F32)
               + jnp.dot(jnp.sin(ang), w1s_ref[...], precision=HIGHEST, preferred_element_type=F32)
               + b1_ref[...])
        h1 = jnp.sin(fr1_ref[...] * pre)
        h2 = jnp.sin(fr2_ref[...] * (jnp.dot(h1, w2_ref[...], precision=HIGHEST, preferred_element_type=F32)
                                     + b2_ref[...]))
        feat_ref[...] = h2

    hh = jnp.dot(feat_ref[...], w3_ref[...], precision=HIGHEST, preferred_element_type=F32)
    d0 = lax.rem(step * tn, D_MODEL)
    d = (d0 + lax.broadcasted_iota(jnp.int32, (1, tn), 1)).astype(F32)
    delta = jnp.abs(HYENA_MIN_DECAY + (HYENA_MAX_DECAY - HYENA_MIN_DECAY) * d / float(D_MODEL - 1))
    centre = seq // 2
    dist = jnp.abs(idx - float(centre)) / float(centre)
    hh = hh * jnp.exp(-dist * delta)
    hh = hh / (jnp.sum(jnp.abs(hh), axis=0, keepdims=True) + 1e-6)

    fhi = fhi_ref[...]
    flo = flo_ref[...]
    off = (seq // 2) % p
    if off:
        zeros_hi = jnp.zeros((p - off, tn), F32)
        zeros_lo = jnp.zeros((off, tn), F32)
        blocks = [jnp.concatenate([zeros_hi, hh[0:off]], axis=0)]
        blocks += [hh[off + (k - 1) * p:off + k * p] for k in range(1, nb)]
        blocks += [jnp.concatenate([hh[off + (nb - 1) * p:seq], zeros_lo], axis=0)]
    else:
        blocks = [hh[k * p:(k + 1) * p] for k in range(nb)]
    spec = []
    for hb in blocks:
        hi = hb.astype(BF16)
        lo = (hb - hi.astype(F32)).astype(BF16)
        spec.append(_dot(fhi, hi) + _dot(fhi, lo) + _dot(flo, hi))
    sign = sign_ref[...]
    for k in range(g_ref.shape[0]):
        kk = kk0 + k
        if kk == 0:
            g = spec[0]
        elif kk == len(spec):
            g = sign * spec[kk - 1]
        else:
            g = spec[kk] + sign * spec[kk - 1]
        g_ref[k] = g


def _spectra_layout(seq, p):
    nb = seq // p
    centre = seq // 2
    nblk = nb + (1 if centre % p else 0)
    shift = centre // p + (1 if centre % p else 0)
    kk_lo = max(shift - (nb - 1), 0)
    kk_hi = min(shift + (nb - 1), nblk)
    return shift, kk_lo, kk_hi - kk_lo + 1


def _filter_spectra(seq, p, w1, b1, fr1, w2, b2, fr2, w3, fwd32, sign):
    tn = FILTER_LANES
    ncols = w3.shape[1]
    _, kk0, ng = _spectra_layout(seq, p)
    fhi = fwd32.astype(BF16)
    flo = (fwd32 - fhi.astype(F32)).astype(BF16)
    args = (w1[0:1], w1[1:1 + FILT_BANDS], w1[1 + FILT_BANDS:], b1.reshape(1, -1), fr1.reshape(1, -1),
            w2, b2.reshape(1, -1), fr2.reshape(1, -1))
    in_specs = [_const_spec(a.shape) for a in args]
    in_specs += [pl.BlockSpec((FILT_WIDTH, tn), lambda j: (0, j)),
                 _const_spec(fhi.shape), _const_spec(flo.shape), _const_spec(sign.shape)]
    return pl.pallas_call(
        functools.partial(_filter_kernel, seq=seq, p=p, kk0=kk0),
        grid=(ncols // tn,),
        in_specs=in_specs,
        out_specs=pl.BlockSpec((ng, 2 * p, tn), lambda j: (0, 0, j)),
        out_shape=jax.ShapeDtypeStruct((ng, 2 * p, ncols), F32),
        scratch_shapes=[pltpu.VMEM((seq, FILT_WIDTH), F32)],
        compiler_params=_params(("arbitrary",)),
        name=f"filter_spectra_{seq}",
    )(*args, w3, fhi, flo, sign)


def _inproj_kernel(*refs, has_pos):
    if has_pos:
        x_ref, pos_ref = refs[:2]
        refs = refs[2:]
    else:
        x_ref = refs[0]
        refs = refs[1:]
    (mod_ref, n1w_ref, wa_ref, ba_ref, wg_ref, bg_ref, wr_ref, br_ref,
     qkv_ref, o_ref, gates_ref, uh_ref, gmh_ref) = refs
    x = x_ref[0]
    if has_pos:
        x = x + pos_ref[...]
    sh1 = mod_ref[0, :, 0:D_MODEL]
    sc1 = mod_ref[0, :, D_MODEL:2 * D_MODEL]
    hn = (_rms(x) * n1w_ref[...]) * (1.0 + sc1) + sh1
    hb = hn.astype(BF16)
    hlo = (hn - hb.astype(F32)).astype(BF16)
    for c in range(4):
        sl = slice(c * D_MODEL, (c + 1) * D_MODEL)
        r = _dot(hb, wa_ref[:, sl]) + ba_ref[:, sl]
        if c < 3:
            qkv_ref[0, :, sl] = r.astype(BF16)
        else:
            o_ref[0] = r
    for c in range(5):
        sl = slice(c * D_MODEL, (c + 1) * D_MODEL)
        r = _dot(hb, wr_ref[:, sl]) + br_ref[:, sl]
        if c < 3:
            uh_ref[0, :, sl] = r
        else:
            gmh_ref[0, :, (c - 3) * D_MODEL:(c - 2) * D_MODEL] = r
    nt_dims = (((1,), (1,)), ((), ()))
    p1 = lax.dot_general(wg_ref[...], hb, nt_dims, preferred_element_type=F32)
    p2 = lax.dot_general(wg_ref[...], hlo, nt_dims, preferred_element_type=F32)
    gates_ref[0] = p1[0:N_GATES] + p1[N_GATES:2 * N_GATES] + p2[0:N_GATES] + bg_ref[...]


def _in_projection(x, pos, mod3, mod_row, n1w, wa, ba, wg, bg, wr, br):
    b, s, _ = x.shape
    tm = TOKEN_TILE
    has_pos = pos is not None
    in_specs = [pl.BlockSpec((1, tm, D_MODEL), lambda i, j: (i, j, 0))]
    args = [x]
    if has_pos:
        in_specs.append(pl.BlockSpec((tm, D_MODEL), lambda i, j: (j, 0)))
        args.append(pos)
    in_specs += [pl.BlockSpec((1, 1, 2 * D_MODEL), lambda i, j: (mod_row(i), 0, 0)),
                 _const_spec(n1w.shape), _const_spec(wa.shape), _const_spec(ba.shape),
                 _const_spec(wg.shape), _const_spec(bg.shape), _const_spec(wr.shape), _const_spec(br.shape)]
    args += [mod3, n1w, wa, ba, wg, bg, wr, br]
    out_shape = (jax.ShapeDtypeStruct((b, s, 3 * D_MODEL), BF16),
                 jax.ShapeDtypeStruct((b, s, D_MODEL), F32),
                 jax.ShapeDtypeStruct((b, N_GATES, s), F32),
                 jax.ShapeDtypeStruct((b, s, 3 * D_MODEL), F32),
                 jax.ShapeDtypeStruct((b, s, 2 * D_MODEL), F32))
    out_specs = tuple(pl.BlockSpec((1, N_GATES, tm), lambda i, j: (i, 0, j)) if k == 2 else
                      pl.BlockSpec((1, tm, sh.shape[2]), lambda i, j: (i, j, 0)) for k, sh in enumerate(out_shape))
    return pl.pallas_call(
        functools.partial(_inproj_kernel, has_pos=has_pos),
        grid=(b, s // tm),
        in_specs=in_specs,
        out_specs=out_specs,
        out_shape=out_shape,
        compiler_params=_params(("arbitrary", "arbitrary")),
        name="in_projection_pos" if has_pos else "in_projection",
    )(*args)


def _log_sigmoid(x):
    return jnp.minimum(x, 0.0) - jnp.log(1.0 + jnp.exp(-jnp.abs(x)))


def _split3(x):
    hi = x.astype(BF16).astype(F32)
    mid = (x - hi).astype(BF16).astype(F32)
    lo = (x - hi - mid).astype(BF16).astype(F32)
    return hi, mid, lo


ROW_E, ROW_B, ROW_M_INTER, ROW_W_STATE, ROW_DECAY, N_ROW_KINDS = 0, 3, 6, 7, 8, 9


def _mlstm_kernel(*refs, seq, heads, has_init, emit_state):
    q_ref, k_ref, v_ref, grow_ref = refs[:4]
    refs = refs[4:]
    if has_init:
        c0_ref, n0_ref, m0_ref = refs[:3]
        refs = refs[3:]
    h_ref = refs[0]
    refs = refs[1:]
    if emit_state:
        c_out, n_out, m_out = refs[:3]
        refs = refs[3:]
    c_s, n_s, rows_s = refs
    t = CHUNK
    nc = seq // t
    scale = HEAD_DIM ** -0.5

    r_io = lax.broadcasted_iota(jnp.int32, (t, t), 0)
    c_io = lax.broadcasted_iota(jnp.int32, (t, t), 1)
    chunk_id = lax.broadcasted_iota(jnp.int32, (nc, 1), 0)

    chains = range(2 * heads)

    m_final = []
    for ch in chains:
        hh, dr = divmod(ch, 2)
        if has_init:
            c_s[ch] = c0_ref[0, 0, dr, hh]
            n_s[ch] = n0_ref[0, hh, dr:dr + 1, :]
            m = m0_ref[0, hh, dr:dr + 1, :]
        else:
            c_s[ch] = jnp.zeros((HEAD_DIM, HEAD_DIM), F32)
            n_s[ch] = jnp.zeros((1, HEAD_DIM), F32)
            m = jnp.zeros((1, 1), F32)
        li = grow_ref[0, hh, 2 * dr]
        lf = _log_sigmoid(grow_ref[0, hh, 2 * dr + 1])
        tri = ((r_io <= c_io) if dr == 0 else (r_io >= c_io)).astype(BF16)
        b = sum(_dot(part.astype(BF16), tri) for part in _split3(lf))
        b_last = jnp.sum(lf, axis=1, keepdims=True)
        a = b_last - b + li
        a_max = jnp.max(a, axis=1, keepdims=True)
        m_before = jnp.zeros((nc, 1), F32)
        m_after = jnp.zeros((nc, 1), F32)
        for c in (range(nc) if dr == 0 else reversed(range(nc))):
            m_new = jnp.maximum(b_last[c:c + 1, :] + m, a_max[c:c + 1, :])
            m_before = jnp.where(chunk_id == c, m, m_before)
            m_after = jnp.where(chunk_id == c, m_new, m_after)
            m = m_new
        m_final.append(m)
        kinds = (*_split3(li - b), *_split3(b), b + m_before, jnp.exp(a - m_after),
                 jnp.broadcast_to(jnp.exp(b_last + m_before - m_after), (nc, t)))
        for idx, x in enumerate(kinds):
            rows_s[ch, idx] = x

    sub8 = lax.broadcasted_iota(jnp.int32, (8, t), 0)
    sub16 = lax.broadcasted_iota(jnp.int32, (16, 1), 0)

    def pick3(base, ch, c, lo, ones_lo, ones_hi):
        out = jnp.where((sub8 >= ones_lo) & (sub8 < ones_hi), 1.0, 0.0)
        for i in range(3):
            out = jnp.where(sub8 == lo + i, rows_s[ch, base + i, pl.ds(c, 1), :], out)
        return out

    def hi_lo_rows(x):
        hi = x.astype(BF16).astype(F32)
        lo = x - hi
        return jnp.where(sub16 == 0, hi, jnp.where(sub16 == 1, lo, 0.0)).astype(BF16)

    nt_dims = (((1,), (1,)), ((), ()))
    tn_dims = (((0,), (0,)), ((), ()))

    def both(i, accumulate):
        dirs = chains
        cs = [i if ch % 2 == 0 else nc - 1 - i for ch in chains]
        r0 = [c * t if isinstance(c, int) else pl.multiple_of(c * t, t) for c in cs]
        cols = [slice((ch // 2) * HEAD_DIM, (ch // 2 + 1) * HEAD_DIM) for ch in chains]
        q = [q_ref[0, pl.ds(r0[d], t), cols[d]] for d in dirs]
        k = [k_ref[0, pl.ds(r0[d], t), cols[d]] for d in dirs]
        v = [v_ref[0, pl.ds(r0[d], t), cols[d]] for d in dirs]
        row = lambda kind, d: rows_s[d, kind, pl.ds(cs[d], 1), :]
        d_t = [lax.dot_general(pick3(ROW_E, d, cs[d], 0, 3, 6), pick3(ROW_B, d, cs[d], 3, 0, 3), tn_dims,
                               preferred_element_type=F32) for d in dirs]
        d_t = [jnp.where((r_io <= c_io) if d % 2 == 0 else (r_io >= c_io), d_t[d], -jnp.inf) for d in dirs]
        m_inter = [row(ROW_M_INTER, d) for d in dirs]
        m_comb = [jnp.maximum(m_inter[d], jnp.max(d_t[d], axis=0, keepdims=True)) for d in dirs]
        pw = [jnp.exp(d_t[d] - m_comb[d]) for d in dirs]
        w_inter = [jnp.exp(m_inter[d] - m_comb[d]) for d in dirs]
        c_prev = [c_s[d] for d in dirs]
        n_prev = [n_s[d] for d in dirs]
        kq = [lax.dot_general(jnp.concatenate([k[d], hi_lo_rows(n_prev[d])], axis=0), q[d], nt_dims,
                              preferred_element_type=F32) for d in dirs]
        qc = [_dot(q[d], c_prev[d].astype(BF16)) for d in dirs]
        w_state = [row(ROW_W_STATE, d) for d in dirs]
        kw = [(k[d].T.astype(F32) * w_state[d]).astype(BF16) for d in dirs]
        sp = [kq[d][0:t] * pw[d] for d in dirs]
        den = [scale * (w_inter[d] * (kq[d][t:t + 1] + kq[d][t + 1:t + 2]) + jnp.sum(sp[d], axis=0, keepdims=True))
               for d in dirs]
        inv_den = [scale / jnp.maximum(jnp.abs(den[d]), jnp.exp(-m_comb[d])) for d in dirs]
        inter_col = [jnp.transpose(jnp.where(sub8 == 0, w_inter[d] * inv_den[d], 0.0))[:, 0:1] for d in dirs]
        sv = [lax.dot_general((sp[d] * inv_den[d]).astype(BF16), v[d], tn_dims, preferred_element_type=F32)
              for d in dirs]
        kv = [_dot(kw[d], v[d]) for d in dirs]
        nk = [_dot(hi_lo_rows(w_state[d]), k[d]) for d in dirs]
        decay = [row(ROW_DECAY, d)[:, 0:1] for d in dirs]
        for d in dirs:
            c_s[d] = decay[d] * c_prev[d] + kv[d]
            n_s[d] = decay[d] * n_prev[d] + nk[d][0:1] + nk[d][1:2]
        for d in dirs:
            h = sv[d] + inter_col[d] * qc[d]
            if accumulate:
                h_ref[0, pl.ds(r0[d], t), cols[d]] += h
            else:
                h_ref[0, pl.ds(r0[d], t), cols[d]] = h

    half = nc // 2
    if nc <= 16:
        for i in range(nc):
            both(i, i >= half)
    else:
        lax.fori_loop(0, half, lambda i, carry: (both(i, False), carry)[1], 0)
        lax.fori_loop(half, nc, lambda i, carry: (both(i, True), carry)[1], 0)

    if emit_state:
        for ch in chains:
            hh, dr = divmod(ch, 2)
            c_out[0, 0, dr, hh] = c_s[ch]
            n_out[0, hh, dr:dr + 1, :] = n_s[ch]
            m_out[0, hh, dr:dr + 1, :] = jnp.broadcast_to(m_final[ch], (1, 128))


def _mlstm(qkv, grow, init, emit_state):
    b, s, _ = qkv.shape
    nc = s // CHUNK
    has_init = init is not None
    hpb = MLSTM_HEADS_PER_STEP
    nhb = N_HEADS // hpb
    width = hpb * HEAD_DIM
    in_specs = [pl.BlockSpec((1, s, width), lambda i, h: (i, 0, h)),
                pl.BlockSpec((1, s, width), lambda i, h: (i, 0, nhb + h)),
                pl.BlockSpec((1, s, width), lambda i, h: (i, 0, 2 * nhb + h)),
                pl.BlockSpec((1, hpb, 4, nc, CHUNK), lambda i, h: (i, h, 0, 0, 0))]
    args = [qkv, qkv, qkv, grow]
    if has_init:
        c0, n0, m0 = init
        in_specs += [pl.BlockSpec((1, 1, 2, hpb, HEAD_DIM, HEAD_DIM), lambda i, h: (i, 0, 0, h, 0, 0)),
                     pl.BlockSpec((1, hpb, 2, HEAD_DIM), lambda i, h: (i, h, 0, 0)),
                     pl.BlockSpec((1, hpb, 2, 1), lambda i, h: (i, h, 0, 0))]
        args += [c0, n0, m0]
    out_shape = [jax.ShapeDtypeStruct((b, s, D_MODEL), F32)]
    out_specs = [pl.BlockSpec((1, s, width), lambda i, h: (i, 0, h))]
    if emit_state:
        out_shape += [jax.ShapeDtypeStruct((b, 1, 2, N_HEADS, HEAD_DIM, HEAD_DIM), F32),
                      jax.ShapeDtypeStruct((b, N_HEADS, 2, HEAD_DIM), F32),
                      jax.ShapeDtypeStruct((b, N_HEADS, 2, 128), F32)]
        out_specs += [pl.BlockSpec((1, 1, 2, hpb, HEAD_DIM, HEAD_DIM), lambda i, h: (i, 0, 0, h, 0, 0)),
                      pl.BlockSpec((1, hpb, 2, HEAD_DIM), lambda i, h: (i, h, 0, 0)),
                      pl.BlockSpec((1, hpb, 2, 128), lambda i, h: (i, h, 0, 0))]
    return pl.pallas_call(
        functools.partial(_mlstm_kernel, seq=s, heads=hpb, has_init=has_init, emit_state=emit_state),
        grid=(b, nhb),
        in_specs=in_specs,
        out_specs=tuple(out_specs),
        out_shape=tuple(out_shape),
        scratch_shapes=[pltpu.VMEM((2 * hpb, HEAD_DIM, HEAD_DIM), F32), pltpu.VMEM((2 * hpb, 1, HEAD_DIM), F32),
                        pltpu.VMEM((2 * hpb, N_ROW_KINDS, nc, CHUNK), F32)],
        compiler_params=_params(("arbitrary", "arbitrary")),
        name=f"mlstm_{s}",
    )(*args)


def _loop(n, body, static_max=4):
    if n <= static_max:
        for i in range(n):
            body(i)
    else:
        def step(i, carry):
            body(i)
            return carry
        lax.fori_loop(0, n, step, 0)


def _hyena_kernel(x1_ref, x2_ref, v_ref, w1_ref, w2_ref, wv_ref, b1_ref, b2_ref, bv_ref, skip_ref,
                  g0_ref, g1_ref, fwd_ref, inv_ref, o_ref, zf_s, yf_s, *, seq, p, shift, kk0):
    nb = seq // p
    ng = g0_ref.shape[0]
    bb = x1_ref.shape[0]
    lanes = o_ref.shape[-1]
    sub = lax.broadcasted_iota(jnp.int32, (8, 1), 0)

    def conv_rows(src_ref, bi, r, w_ref, b_ref):
        if isinstance(r, int):
            rr = r * p
            up = src_ref[bi, rr - 8:rr, :][7:8, :] if rr > 0 else jnp.zeros((1, lanes), F32)
            dn = src_ref[bi, rr + p:rr + p + 8, :][0:1, :] if rr + p < seq else jnp.zeros((1, lanes), F32)
        else:
            rr = pl.multiple_of(r * p, p)
            above = pl.multiple_of(jnp.maximum(rr - 8, 0), 8)
            below = pl.multiple_of(jnp.minimum(rr + p, seq - 8), 8)
            up = jnp.where(rr > 0, src_ref[bi, pl.ds(above, 8), :][7:8, :], 0.0)
            dn = jnp.where(rr + p < seq, src_ref[bi, pl.ds(below, 8), :][0:1, :], 0.0)
        cur = src_ref[bi, pl.ds(rr, p), :]
        prev = pltpu.roll(cur, 1, 0)
        prev = jnp.concatenate([jnp.where(sub == 0, up, prev[0:8]), prev[8:]], axis=0)
        nxt = pltpu.roll(cur, p - 1, 0)
        nxt = jnp.concatenate([nxt[:p - 8], jnp.where(sub == 7, dn, nxt[p - 8:])], axis=0)
        return b_ref[...] + prev * w_ref[0:1, :] + cur * w_ref[1:2, :] + nxt * w_ref[2:3, :]

    fwd = fwd_ref[...]
    inv = inv_ref[...]
    terms = [[(j, i - j + shift - kk0) for j in range(nb) if 0 <= i - j + shift - kk0 < ng] for i in range(nb)]
    for bi in range(bb):
        def conv_block(j, bi=bi):
            rows = pl.ds(j * p, p) if isinstance(j, int) else pl.ds(pl.multiple_of(j * p, p), p)
            z = conv_rows(v_ref, bi, j, wv_ref, bv_ref)
            o_ref[bi, rows, :] = z
            zf_s[bi, j] = _dot(fwd, z.astype(BF16))

        _loop(nb, conv_block)
        for order, (g_ref, x_ref, w_ref, b_ref) in enumerate(((g0_ref, x1_ref, w1_ref, b1_ref),
                                                               (g1_ref, x2_ref, w2_ref, b2_ref))):
            if order > 0:
                for j in range(nb):
                    zf_s[bi, j] = _dot(fwd, o_ref[bi, j * p:(j + 1) * p, :].astype(BF16))

            def spec_mac(r, bi=bi, g_ref=g_ref):
                if isinstance(r, int):
                    rr = r * SPEC_ROWS
                else:
                    rr = pl.multiple_of(r * SPEC_ROWS, SPEC_ROWS)
                for i in range(nb):
                    re = jnp.zeros((SPEC_ROWS, lanes), F32)
                    im = jnp.zeros((SPEC_ROWS, lanes), F32)
                    for j, kk in terms[i]:
                        zre = zf_s[bi, j, pl.ds(rr, SPEC_ROWS), :]
                        zim = zf_s[bi, j, pl.ds(p + rr, SPEC_ROWS), :]
                        gre = g_ref[kk, pl.ds(rr, SPEC_ROWS), :]
                        gim = g_ref[kk, pl.ds(p + rr, SPEC_ROWS), :]
                        re = re + (zre * gre - zim * gim)
                        im = im + (zre * gim + zim * gre)
                    yf_s[bi, i, pl.ds(rr, SPEC_ROWS), :] = re
                    yf_s[bi, i, pl.ds(p + rr, SPEC_ROWS), :] = im

            _loop(p // SPEC_ROWS, spec_mac, static_max=8)
            skip = skip_ref[order:order + 1, :]
            for i in range(nb):
                dc = jnp.zeros((1, lanes), F32)
                ny = jnp.zeros((1, lanes), F32)
                for j, kk in terms[i]:
                    dc = dc + zf_s[bi, j, 0:1, :] * g_ref[kk, 0:1, :]
                    ny = ny + zf_s[bi, j, p:p + 1, :] * g_ref[kk, p:p + 1, :]
                yf_s[bi, i, 0:1, :] = dc
                yf_s[bi, i, p:p + 1, :] = ny
            for i in range(nb):
                y = _dot(inv, yf_s[bi, i].astype(BF16))
                rows = slice(i * p, (i + 1) * p)
                gate = conv_rows(x_ref, bi, i, w_ref, b_ref)
                o_ref[bi, rows, :] = gate * (y + skip * o_ref[bi, rows, :])


def _hyena(uh, conv_w, conv_b, skip, spectra, fwd, inv, p, bb):
    b, s, _ = uh.shape
    dc = HYENA_LANES
    nct = D_MODEL // dc
    nb = s // p
    shift, kk0, ng = _spectra_layout(s, p)
    conv_b = conv_b.reshape(1, -1)

    def part(k):
        return pl.BlockSpec((bb, s, dc), lambda c, i, k=k: (i, 0, k * nct + c))

    def wpart(k, rows):
        return pl.BlockSpec((rows, dc), lambda c, i, k=k: (0, k * nct + c))

    def gpart(order):
        return pl.BlockSpec((ng, 2 * p, dc), lambda c, i, order=order: (0, 0, order * nct + c),
                            pipeline_mode=pl.Buffered(1))

    in_specs = [part(0), part(1), part(2), wpart(0, 3), wpart(1, 3), wpart(2, 3),
                wpart(0, 1), wpart(1, 1), wpart(2, 1),
                pl.BlockSpec((2, dc), lambda c, i: (0, c)),
                gpart(0), gpart(1), _const_spec(fwd.shape), _const_spec(inv.shape)]
    return pl.pallas_call(
        functools.partial(_hyena_kernel, seq=s, p=p, shift=shift, kk0=kk0),
        grid=(nct, b // bb),
        in_specs=in_specs,
        out_specs=pl.BlockSpec((bb, s, dc), lambda c, i: (i, 0, c)),
        out_shape=jax.ShapeDtypeStruct((b, s, D_MODEL), F32),
        scratch_shapes=[pltpu.VMEM((bb, nb, 2 * p, dc), F32), pltpu.VMEM((bb, nb, 2 * p, dc), F32)],
        compiler_params=_params(("arbitrary", "arbitrary")),
        name=f"hyena_{s}",
    )(uh, uh, uh, conv_w, conv_w, conv_w, conv_b, conv_b, conv_b, skip, spectra, spectra, fwd, inv)


def _tail_kernel(*refs, has_pos):
    if has_pos:
        x_ref, pos_ref = refs[:2]
        refs = refs[2:]
    else:
        x_ref = refs[0]
        refs = refs[1:]
    (h_ref, o_ref, gm_ref, gh_ref, yh_ref, modb_ref, modc_ref, mnw_ref, n2w_ref, fnw_ref,
     wbm_ref, wbh_ref, wout_ref, w1_ref, b1_ref, w2_ref, b2_ref, y_ref) = refs
    x = x_ref[0]
    if has_pos:
        x = x + pos_ref[...]
    g1 = modb_ref[0, :, 0:D_MODEL]
    sh2 = modb_ref[0, :, D_MODEL:2 * D_MODEL]
    sc2 = modc_ref[0, :, 0:D_MODEL]
    g2 = modc_ref[0, :, D_MODEL:2 * D_MODEL]
    h = h_ref[0]
    heads = [_rms(h[:, hd * HEAD_DIM:(hd + 1) * HEAD_DIM]) for hd in range(N_HEADS)]
    hm = jnp.concatenate(heads, axis=-1) * mnw_ref[...] * _sigmoid(o_ref[0])
    merged = (_sigmoid(gm_ref[0]) * _dot(hm.astype(BF16), wbm_ref[...])
              + _sigmoid(gh_ref[0]) * _dot(yh_ref[0].astype(BF16), wbh_ref[...]))
    x1 = x + g1 * _dot(merged.astype(BF16), wout_ref[...])
    hn2 = ((_rms(x1) * n2w_ref[...]) * (1.0 + sc2) + sh2).astype(BF16)
    ff = b2_ref[...]
    for kc in range(D_FF // D_MODEL):
        sl = slice(kc * D_MODEL, (kc + 1) * D_MODEL)
        a = jnp.maximum(_dot(hn2, w1_ref[:, sl]) + b1_ref[:, sl], 0.0)
        ff = ff + _dot((a * a).astype(BF16), w2_ref[sl, :])
    x2 = x1 + g2 * ff
    y_ref[0] = _rms(x2) * fnw_ref[...]


def _tail(x, pos, h, o, gmh, yh, mod3, mod_row, mnw, n2w, fnw, wbm, wbh, wout, w1, b1, w2, b2):
    b, s, _ = x.shape
    tm = TOKEN_TILE
    has_pos = pos is not None
    tok = pl.BlockSpec((1, tm, D_MODEL), lambda i, j: (i, j, 0))
    in_specs = [tok]
    args = [x]
    if has_pos:
        in_specs.append(pl.BlockSpec((tm, D_MODEL), lambda i, j: (j, 0)))
        args.append(pos)
    in_specs += [tok, tok, tok, pl.BlockSpec((1, tm, D_MODEL), lambda i, j: (i, j, 1)), tok,
                 pl.BlockSpec((1, 1, 2 * D_MODEL), lambda i, j: (mod_row(i), 0, 1)),
                 pl.BlockSpec((1, 1, 2 * D_MODEL), lambda i, j: (mod_row(i), 0, 2))]
    args += [h, o, gmh, gmh, yh, mod3, mod3]
    consts = [mnw, n2w, fnw, wbm, wbh, wout, w1, b1, w2, b2]
    in_specs += [_const_spec(a.shape) for a in consts]
    args += consts
    return pl.pallas_call(
        functools.partial(_tail_kernel, has_pos=has_pos),
        grid=(b, s // tm),
        in_specs=in_specs,
        out_specs=tok,
        out_shape=jax.ShapeDtypeStruct((b, s, D_MODEL), F32),
        compiler_params=_params(("arbitrary", "arbitrary")),
        name="tail_pos" if has_pos else "tail",
    )(*args)


def _gate_rows(gates_t):
    b, _, s = gates_t.shape
    return gates_t.reshape(b, N_HEADS, 4, s // CHUNK, CHUNK)


def kernel(x_prompt, x_sample, state_mlstm_C, state_mlstm_n, state_mlstm_m, c, c_ctx, w_ada, b_ada, norm1_w,
           w_in, b_in, hy_conv_w, hy_conv_b, filt_w1, filt_b1, filt_freq1, filt_w2, filt_b2, filt_freq2,
           filt_w3, hy_skip, mlstm_norm_w, w_br_m, w_br_h, w_out, norm2_w, w_mlp1, b_mlp1, w_mlp2, b_mlp2,
           final_norm_w):
    depth = w_ada.shape[0]
    assert depth == 1, "single-layer configuration"
    l = 0
    dec_b, dec_s, _ = x_sample.shape
    ctx_s = x_prompt.shape[1]

    n_rows = -(-(1 + dec_b) // 8) * 8
    cond = jnp.concatenate([c_ctx[None, :], c, jnp.zeros((n_rows - 1 - dec_b, D_MODEL), F32)], axis=0)
    mod = _modulation(cond, w_ada[l], b_ada[l])
    mod3 = mod.reshape(n_rows, 1, 6 * D_MODEL)

    w = w_in[l]
    bias = b_in[l]
    wa = w[:, :QKVO_COLS].astype(BF16)
    ba = bias[:QKVO_COLS].reshape(1, -1)
    head_major = np.arange(N_GATES).reshape(4, N_HEADS).T.reshape(-1)
    wg32 = w[:, QKVO_COLS:QKVO_COLS + N_GATES][:, head_major].T
    wg_hi = wg32.astype(BF16)
    wg = jnp.concatenate([wg_hi, (wg32 - wg_hi.astype(F32)).astype(BF16)], axis=0)
    bg = bias[QKVO_COLS:QKVO_COLS + N_GATES][head_major].reshape(-1, 1)
    wr = w[:, QKVO_COLS + N_GATES:].astype(BF16)
    br = bias[QKVO_COLS + N_GATES:].reshape(1, -1)
    n1w = norm1_w[l].reshape(1, -1)
    tail_w = (mlstm_norm_w[l].reshape(1, -1), norm2_w[l].reshape(1, -1), final_norm_w.reshape(1, -1),
              w_br_m[l].astype(BF16), w_br_h[l].astype(BF16), w_out[l].astype(BF16),
              w_mlp1[l].astype(BF16), b_mlp1[l].reshape(1, -1), w_mlp2[l].astype(BF16), b_mlp2[l].reshape(1, -1))
    filt = (filt_w1[l], filt_b1[l], filt_freq1[l], filt_w2[l], filt_b2[l], filt_freq2[l], filt_w3[l])

    pos = _pos_table(dec_s)

    def run(x, pos_tab, mod_row, init, emit_state, p, bb):
        s = x.shape[1]
        fwd32, inv32, sign = (jnp.asarray(a) for a in _dft_mats(p))
        spectra = _filter_spectra(s, p, *filt, fwd32, sign)
        qkv, o, gates, uh, gmh = _in_projection(x, pos_tab, mod3, mod_row, n1w, wa, ba, wg, bg, wr, br)
        ml = _mlstm(qkv, _gate_rows(gates), init, emit_state)
        yh = _hyena(uh, hy_conv_w[l], hy_conv_b[l], hy_skip[l], spectra,
                    fwd32.astype(BF16), inv32.astype(BF16), p, bb)
        y = _tail(x, pos_tab, ml[0], o, gmh, yh, mod3, mod_row, *tail_w)
        return y, ml[1:]

    y_prompt, (st_c, st_n, st_m) = run(x_prompt, None, lambda i: 0, None, True, ctx_s, 4)
    init = (state_mlstm_C,
            jnp.transpose(state_mlstm_n[:, l], (0, 2, 1, 3)),
            jnp.transpose(state_mlstm_m[:, l], (0, 2, 1))[..., None])
    y_sample, _ = run(x_sample, pos, lambda i: i + 1, init, False, dec_s // 4, 1)

    new_state_n = jnp.transpose(st_n, (0, 2, 1, 3))[:, None]
    new_state_m = jnp.transpose(st_m[..., 0], (0, 2, 1))[:, None]
    return (y_prompt, y_sample, st_c, new_state_n, new_state_m)
```

```python
import functools
import math

import numpy as np
import jax
import jax.numpy as jnp
from jax import lax
from jax.experimental import pallas as pl
from jax.experimental.pallas import tpu as pltpu

F32 = jnp.float32
BF16 = jnp.bfloat16
HIGHEST = lax.Precision.HIGHEST

D_MODEL = 1024
N_HEADS = 4
HEAD_DIM = 256
D_FF = 4 * D_MODEL
GRID_W = 64
FILT_BANDS = 16
FILT_WIDTH = 64
HYENA_MIN_DECAY = math.log(1e-2) / 1.5
HYENA_MAX_DECAY = math.log(1e-2) / 0.3
CHUNK = 128
RMS_EPS = 1e-6
N_GATES = 4 * N_HEADS
QKVO_COLS = 4 * D_MODEL
REST_COLS = 5 * D_MODEL

V7X_VMEM_BYTES = 64 * 1024 * 1024
VMEM_LIMIT = V7X_VMEM_BYTES - 8 * 1024 * 1024

TOKEN_TILE = 256
HYENA_LANES = 256
FILTER_LANES = 256
SPEC_ROWS = 32
MLSTM_HEADS_PER_STEP = 2


def _params(sem):
    return pltpu.CompilerParams(dimension_semantics=sem, vmem_limit_bytes=VMEM_LIMIT)


def _const_spec(shape):
    nd = len(shape)
    return pl.BlockSpec(shape, lambda *_: (0,) * nd, pipeline_mode=pl.Buffered(1))


def _sigmoid(x):
    return 1.0 / (1.0 + jnp.exp(-x))


def _rms(x):
    return x * lax.rsqrt(jnp.mean(x * x, axis=-1, keepdims=True) + RMS_EPS)


def _dot(a, b):
    return jnp.dot(a, b, preferred_element_type=F32)


def _mod_kernel(c_ref, w_ref, b_ref, o_ref):
    c = c_ref[...]
    s = c * _sigmoid(c)
    o_ref[...] = jnp.dot(s, w_ref[...], precision=HIGHEST, preferred_element_type=F32) + b_ref[...]


def _modulation(cond, w_ada, b_ada):
    rows = cond.shape[0]
    n = w_ada.shape[1]
    tn = 1536
    return pl.pallas_call(
        _mod_kernel,
        grid=(n // tn,),
        in_specs=[pl.BlockSpec((rows, D_MODEL), lambda j: (0, 0)),
                  pl.BlockSpec((D_MODEL, tn), lambda j: (0, j)),
                  pl.BlockSpec((1, tn), lambda j: (0, j))],
        out_specs=pl.BlockSpec((rows, tn), lambda j: (0, j)),
        out_shape=jax.ShapeDtypeStruct((rows, n), F32),
        compiler_params=_params(("arbitrary",)),
        name="modulation",
    )(cond, w_ada, b_ada.reshape(1, n))


def _pos_kernel(o_ref, *, rows):
    quarter = D_MODEL // 4
    half = D_MODEL // 2
    k = lax.broadcasted_iota(jnp.int32, (1, quarter), 1).astype(F32)
    omega = jnp.exp(k * (-math.log(10000.0) / quarter))

    def axis_embed(n):
        p = lax.broadcasted_iota(jnp.int32, (n, 1), 0).astype(F32)
        a = p * omega
        return jnp.concatenate([jnp.sin(a), jnp.cos(a)], axis=-1)

    er = axis_embed(rows)
    ec = axis_embed(GRID_W)
    o_ref[:, :, 0:half] = jnp.broadcast_to(er[:, None, :], (rows, GRID_W, half))
    o_ref[:, :, half:D_MODEL] = jnp.broadcast_to(ec[None, :, :], (rows, GRID_W, half))


def _pos_table(n_tokens):
    rows = n_tokens // GRID_W
    out = pl.pallas_call(
        functools.partial(_pos_kernel, rows=rows),
        out_shape=jax.ShapeDtypeStruct((rows, GRID_W, D_MODEL), F32),
        compiler_params=pltpu.CompilerParams(vmem_limit_bytes=VMEM_LIMIT),
        name="pos_table",
    )()
    return out.reshape(n_tokens, D_MODEL)


def _dft_mats(p):
    n = 2 * p
    idx = np.arange(p, dtype=np.float64)
    ang = 2.0 * np.pi * np.outer(idx, idx) / n
    alt = np.where(np.arange(p) % 2 == 0, 1.0, -1.0)
    fwd = np.zeros((n, p))
    fwd[:p] = np.cos(ang)
    fwd[p] = alt
    fwd[p + 1:] = -np.sin(ang[1:])
    inv = np.zeros((p, n))
    inv[:, :p] = 2.0 * np.cos(ang) / n
    inv[:, 0] = 1.0 / n
    inv[:, p] = alt / n
    inv[:, p + 1:] = -2.0 * np.sin(ang[:, 1:]) / n
    sign = np.concatenate([alt, alt])
    sign[p] = 1.0
    return fwd.astype(np.float32), inv.astype(np.float32), sign.astype(np.float32).reshape(n, 1)


def _filter_kernel(w1t_ref, w1c_ref, w1s_ref, b1_ref, fr1_ref, w2_ref, b2_ref, fr2_ref, w3_ref,
                   fhi_ref, flo_ref, sign_ref, g_ref, feat_ref, *, seq, p, kk0):
    nb = seq // p
    tn = w3_ref.shape[1]
    step = pl.program_id(0)
    idx = lax.broadcasted_iota(jnp.int32, (seq, 1), 0).astype(F32)

    @pl.when(step == 0)
    def _():
        t = idx / float(seq - 1)
        bands = (lax.broadcasted_iota(jnp.int32, (1, FILT_BANDS), 1) + 1).astype(F32)
        ang = ((2.0 * math.pi / seq) * idx) * bands
        pre = (t * w1t_ref[...]
               + jnp.dot(jnp.cos(ang), w1c_ref[...], precision=HIGHEST, preferred_element_type=F32)
               + jnp.dot(jnp.sin(ang), w1s_ref[...], precision=HIGHEST, preferred_element_type=F32)
               + b1_ref[...])
        h1 = jnp.sin(fr1_ref[...] * pre)
        h2 = jnp.sin(fr2_ref[...] * (jnp.dot(h1, w2_ref[...], precision=HIGHEST, preferred_element_type=F32)
                                     + b2_ref[...]))
        feat_ref[...] = h2

    feat = feat_ref[...]
    w3 = w3_ref[...]
    feat_hi = feat.astype(BF16)
    feat_lo = (feat - feat_hi.astype(F32)).astype(BF16)
    w3_hi = w3.astype(BF16)
    w3_lo = (w3 - w3_hi.astype(F32)).astype(BF16)
    hh = _dot(feat_hi, w3_hi) + (_dot(feat_lo, w3_hi) + _dot(feat_hi, w3_lo))
    d0 = lax.rem(step * tn, D_MODEL)
    d = (d0 + lax.broadcasted_iota(jnp.int32, (1, tn), 1)).astype(F32)
    delta = jnp.abs(HYENA_MIN_DECAY + (HYENA_MAX_DECAY - HYENA_MIN_DECAY) * d / float(D_MODEL - 1))
    centre = seq // 2
    dist = jnp.abs(idx - float(centre)) / float(centre)
    hh = hh * jnp.exp(-dist * delta)
    hh = hh / (jnp.sum(jnp.abs(hh), axis=0, keepdims=True) + 1e-6)

    fhi = fhi_ref[...]
    flo = flo_ref[...]
    off = (seq // 2) % p
    if off:
        zeros_hi = jnp.zeros((p - off, tn), F32)
        zeros_lo = jnp.zeros((off, tn), F32)
        blocks = [jnp.concatenate([zeros_hi, hh[0:off]], axis=0)]
        blocks += [hh[off + (k - 1) * p:off + k * p] for k in range(1, nb)]
        blocks += [jnp.concatenate([hh[off + (nb - 1) * p:seq], zeros_lo], axis=0)]
    else:
        blocks = [hh[k * p:(k + 1) * p] for k in range(nb)]
    spec = []
    for hb in blocks:
        hi = hb.astype(BF16)
        lo = (hb - hi.astype(F32)).astype(BF16)
        spec.append(_dot(fhi, hi) + _dot(fhi, lo) + _dot(flo, hi))
    sign = sign_ref[...]
    for k in range(g_ref.shape[0]):
        kk = kk0 + k
        if kk == 0:
            g = spec[0]
        elif kk == len(spec):
            g = sign * spec[kk - 1]
        else:
            g = spec[kk] + sign * spec[kk - 1]
        g_ref[k] = g


def _spectra_layout(seq, p):
    nb = seq // p
    centre = seq // 2
    nblk = nb + (1 if centre % p else 0)
    shift = centre // p + (1 if centre % p else 0)
    kk_lo = max(shift - (nb - 1), 0)
    kk_hi = min(shift + (nb - 1), nblk)
    return shift, kk_lo, kk_hi - kk_lo + 1


def _filter_spectra(seq, p, w1, b1, fr1, w2, b2, fr2, w3, fwd32, sign):
    tn = FILTER_LANES
    ncols = w3.shape[1]
    _, kk0, ng = _spectra_layout(seq, p)
    fhi = fwd32.astype(BF16)
    flo = (fwd32 - fhi.astype(F32)).astype(BF16)
    args = (w1[0:1], w1[1:1 + FILT_BANDS], w1[1 + FILT_BANDS:], b1.reshape(1, -1), fr1.reshape(1, -1),
            w2, b2.reshape(1, -1), fr2.reshape(1, -1))
    in_specs = [_const_spec(a.shape) for a in args]
    in_specs += [pl.BlockSpec((FILT_WIDTH, tn), lambda j: (0, j)),
                 _const_spec(fhi.shape), _const_spec(flo.shape), _const_spec(sign.shape)]
    return pl.pallas_call(
        functools.partial(_filter_kernel, seq=seq, p=p, kk0=kk0),
        grid=(ncols // tn,),
        in_specs=in_specs,
        out_specs=pl.BlockSpec((ng, 2 * p, tn), lambda j: (0, 0, j)),
        out_shape=jax.ShapeDtypeStruct((ng, 2 * p, ncols), F32),
        scratch_shapes=[pltpu.VMEM((seq, FILT_WIDTH), F32)],
        compiler_params=_params(("arbitrary",)),
        name=f"filter_spectra_{seq}",
    )(*args, w3, fhi, flo, sign)


def _inproj_kernel(*refs, has_pos):
    if has_pos:
        x_ref, pos_ref = refs[:2]
        refs = refs[2:]
    else:
        x_ref = refs[0]
        refs = refs[1:]
    (mod_ref, n1w_ref, wa_ref, ba_ref, wg_ref, bg_ref, wr_ref, br_ref,
     qkv_ref, o_ref, gates_ref, uh_ref, gmh_ref) = refs
    x = x_ref[0]
    if has_pos:
        x = x + pos_ref[...]
    sh1 = mod_ref[0, :, 0:D_MODEL]
    sc1 = mod_ref[0, :, D_MODEL:2 * D_MODEL]
    hn = (_rms(x) * n1w_ref[...]) * (1.0 + sc1) + sh1
    hb = hn.astype(BF16)
    hlo = (hn - hb.astype(F32)).astype(BF16)
    nt_dims = (((1,), (1,)), ((), ()))
    for c in range(4):
        sl = slice(c * D_MODEL, (c + 1) * D_MODEL)
        r = lax.dot_general(hb, wa_ref[sl, :], nt_dims, preferred_element_type=F32) + ba_ref[:, sl]
        if c < 3:
            qkv_ref[0, :, sl] = r.astype(BF16)
        else:
            o_ref[0] = r
    for c in range(5):
        sl = slice(c * D_MODEL, (c + 1) * D_MODEL)
        r = lax.dot_general(hb, wr_ref[sl, :], nt_dims, preferred_element_type=F32) + br_ref[:, sl]
        if c < 3:
            uh_ref[0, :, sl] = r
        else:
            gmh_ref[0, :, (c - 3) * D_MODEL:(c - 2) * D_MODEL] = r
    p1 = lax.dot_general(wg_ref[...], hb, nt_dims, preferred_element_type=F32)
    p2 = lax.dot_general(wg_ref[...], hlo, nt_dims, preferred_element_type=F32)
    gates_ref[0] = p1[0:N_GATES] + p1[N_GATES:2 * N_GATES] + p2[0:N_GATES] + bg_ref[...]


def _in_projection(x, pos, mod3, mod_row, n1w, wa, ba, wg, bg, wr, br):
    b, s, _ = x.shape
    tm = TOKEN_TILE
    has_pos = pos is not None
    in_specs = [pl.BlockSpec((1, tm, D_MODEL), lambda i, j: (i, j, 0))]
    args = [x]
    if has_pos:
        in_specs.append(pl.BlockSpec((tm, D_MODEL), lambda i, j: (j, 0)))
        args.append(pos)
    in_specs += [pl.BlockSpec((1, 1, 2 * D_MODEL), lambda i, j: (mod_row(i), 0, 0)),
                 _const_spec(n1w.shape), _const_spec(wa.shape), _const_spec(ba.shape),
                 _const_spec(wg.shape), _const_spec(bg.shape), _const_spec(wr.shape), _const_spec(br.shape)]
    args += [mod3, n1w, wa, ba, wg, bg, wr, br]
    out_shape = (jax.ShapeDtypeStruct((b, s, 3 * D_MODEL), BF16),
                 jax.ShapeDtypeStruct((b, s, D_MODEL), F32),
                 jax.ShapeDtypeStruct((b, N_GATES, s), F32),
                 jax.ShapeDtypeStruct((b, s, 3 * D_MODEL), F32),
                 jax.ShapeDtypeStruct((b, s, 2 * D_MODEL), F32))
    out_specs = tuple(pl.BlockSpec((1, N_GATES, tm), lambda i, j: (i, 0, j)) if k == 2 else
                      pl.BlockSpec((1, tm, sh.shape[2]), lambda i, j: (i, j, 0)) for k, sh in enumerate(out_shape))
    return pl.pallas_call(
        functools.partial(_inproj_kernel, has_pos=has_pos),
        grid=(b, s // tm),
        in_specs=in_specs,
        out_specs=out_specs,
        out_shape=out_shape,
        compiler_params=_params(("arbitrary", "arbitrary")),
        name="in_projection_pos" if has_pos else "in_projection",
    )(*args)


def _log_sigmoid(x):
    return jnp.minimum(x, 0.0) - jnp.log(1.0 + jnp.exp(-jnp.abs(x)))


def _split3(x):
    hi = x.astype(BF16).astype(F32)
    mid = (x - hi).astype(BF16).astype(F32)
    lo = (x - hi - mid).astype(BF16).astype(F32)
    return hi, mid, lo


ROW_E, ROW_B, ROW_M_INTER, ROW_W_STATE, ROW_DECAY, N_ROW_KINDS = 0, 3, 6, 7, 8, 9


def _mlstm_kernel(*refs, seq, heads, has_init, emit_state):
    q_ref, k_ref, v_ref, grow_ref = refs[:4]
    refs = refs[4:]
    if has_init:
        c0_ref, n0_ref, m0_ref = refs[:3]
        refs = refs[3:]
    h_ref = refs[0]
    refs = refs[1:]
    if emit_state:
        c_out, n_out, m_out = refs[:3]
        refs = refs[3:]
    c_s, n_s, rows_s = refs
    t = CHUNK
    nc = seq // t
    scale = HEAD_DIM ** -0.5

    r_io = lax.broadcasted_iota(jnp.int32, (t, t), 0)
    c_io = lax.broadcasted_iota(jnp.int32, (t, t), 1)
    chunk_id = lax.broadcasted_iota(jnp.int32, (nc, 1), 0)

    chains = range(2 * heads)

    m_final = []
    for ch in chains:
        hh, dr = divmod(ch, 2)
        if has_init:
            c_s[ch] = c0_ref[0, 0, dr, hh]
            n_s[ch] = n0_ref[0, hh, dr:dr + 1, :]
            m = m0_ref[0, hh, dr:dr + 1, :]
        else:
            c_s[ch] = jnp.zeros((HEAD_DIM, HEAD_DIM), F32)
            n_s[ch] = jnp.zeros((1, HEAD_DIM), F32)
            m = jnp.zeros((1, 1), F32)
        li = grow_ref[0, hh, 2 * dr]
        lf = _log_sigmoid(grow_ref[0, hh, 2 * dr + 1])
        tri = ((r_io <= c_io) if dr == 0 else (r_io >= c_io)).astype(BF16)
        b = sum(_dot(part.astype(BF16), tri) for part in _split3(lf))
        b_last = jnp.sum(lf, axis=1, keepdims=True)
        a = b_last - b + li
        a_max = jnp.max(a, axis=1, keepdims=True)
        m_before = jnp.zeros((nc, 1), F32)
        m_after = jnp.zeros((nc, 1), F32)
        for c in (range(nc) if dr == 0 else reversed(range(nc))):
            m_new = jnp.maximum(b_last[c:c + 1, :] + m, a_max[c:c + 1, :])
            m_before = jnp.where(chunk_id == c, m, m_before)
            m_after = jnp.where(chunk_id == c, m_new, m_after)
            m = m_new
        m_final.append(m)
        kinds = (*_split3(li - b), *_split3(b), b + m_before, jnp.exp(a - m_after),
                 jnp.broadcast_to(jnp.exp(b_last + m_before - m_after), (nc, t)))
        for idx, x in enumerate(kinds):
            rows_s[ch, idx] = x

    sub8 = lax.broadcasted_iota(jnp.int32, (8, t), 0)
    sub16 = lax.broadcasted_iota(jnp.int32, (16, 1), 0)

    def pick3(base, ch, c, lo, ones_lo, ones_hi):
        out = jnp.where((sub8 >= ones_lo) & (sub8 < ones_hi), 1.0, 0.0)
        for i in range(3):
            out = jnp.where(sub8 == lo + i, rows_s[ch, base + i, pl.ds(c, 1), :], out)
        return out

    def hi_lo_rows(x):
        hi = x.astype(BF16).astype(F32)
        lo = x - hi
        return jnp.where(sub16 == 0, hi, jnp.where(sub16 == 1, lo, 0.0)).astype(BF16)

    nt_dims = (((1,), (1,)), ((), ()))
    tn_dims = (((0,), (0,)), ((), ()))

    def both(i, accumulate):
        dirs = chains
        cs = [i if ch % 2 == 0 else nc - 1 - i for ch in chains]
        r0 = [c * t if isinstance(c, int) else pl.multiple_of(c * t, t) for c in cs]
        cols = [slice((ch // 2) * HEAD_DIM, (ch // 2 + 1) * HEAD_DIM) for ch in chains]
        q = [q_ref[0, pl.ds(r0[d], t), cols[d]] for d in dirs]
        k = [k_ref[0, pl.ds(r0[d], t), cols[d]] for d in dirs]
        v = [v_ref[0, pl.ds(r0[d], t), cols[d]] for d in dirs]
        row = lambda kind, d: rows_s[d, kind, pl.ds(cs[d], 1), :]
        d_t = [lax.dot_general(pick3(ROW_E, d, cs[d], 0, 3, 6), pick3(ROW_B, d, cs[d], 3, 0, 3), tn_dims,
                               preferred_element_type=F32) for d in dirs]
        d_t = [jnp.where((r_io <= c_io) if d % 2 == 0 else (r_io >= c_io), d_t[d], -jnp.inf) for d in dirs]
        m_inter = [row(ROW_M_INTER, d) for d in dirs]
        m_comb = [jnp.maximum(m_inter[d], jnp.max(d_t[d], axis=0, keepdims=True)) for d in dirs]
        pw = [jnp.exp(d_t[d] - m_comb[d]) for d in dirs]
        w_inter = [jnp.exp(m_inter[d] - m_comb[d]) for d in dirs]
        c_prev = [c_s[d] for d in dirs]
        n_prev = [n_s[d] for d in dirs]
        kq = [lax.dot_general(jnp.concatenate([k[d], hi_lo_rows(n_prev[d])], axis=0), q[d], nt_dims,
                              preferred_element_type=F32) for d in dirs]
        qc = [_dot(q[d], c_prev[d].astype(BF16)) for d in dirs]
        w_state = [row(ROW_W_STATE, d) for d in dirs]
        kw = [(k[d].T.astype(F32) * w_state[d]).astype(BF16) for d in dirs]
        sp = [kq[d][0:t] * pw[d] for d in dirs]
        den = [scale * (w_inter[d] * (kq[d][t:t + 1] + kq[d][t + 1:t + 2]) + jnp.sum(sp[d], axis=0, keepdims=True))
               for d in dirs]
        inv_den = [scale / jnp.maximum(jnp.abs(den[d]), jnp.exp(-m_comb[d])) for d in dirs]
        inter_col = [jnp.transpose(jnp.where(sub8 == 0, w_inter[d] * inv_den[d], 0.0))[:, 0:1] for d in dirs]
        sv = [lax.dot_general((sp[d] * inv_den[d]).astype(BF16), v[d], tn_dims, preferred_element_type=F32)
              for d in dirs]
        kv = [_dot(kw[d], v[d]) for d in dirs]
        nk = [_dot(hi_lo_rows(w_state[d]), k[d]) for d in dirs]
        decay = [row(ROW_DECAY, d)[:, 0:1] for d in dirs]
        for d in dirs:
            c_s[d] = decay[d] * c_prev[d] + kv[d]
            n_s[d] = decay[d] * n_prev[d] + nk[d][0:1] + nk[d][1:2]
        for d in dirs:
            h = sv[d] + inter_col[d] * qc[d]
            if accumulate:
                h_ref[0, pl.ds(r0[d], t), cols[d]] += h
            else:
                h_ref[0, pl.ds(r0[d], t), cols[d]] = h

    half = nc // 2
    if nc <= 16:
        for i in range(nc):
            both(i, i >= half)
    else:
        lax.fori_loop(0, half, lambda i, carry: (both(i, False), carry)[1], 0)
        lax.fori_loop(half, nc, lambda i, carry: (both(i, True), carry)[1], 0)

    if emit_state:
        for ch in chains:
            hh, dr = divmod(ch, 2)
            c_out[0, 0, dr, hh] = c_s[ch]
            n_out[0, hh, dr:dr + 1, :] = n_s[ch]
            m_out[0, hh, dr:dr + 1, :] = jnp.broadcast_to(m_final[ch], (1, 128))


def _mlstm(qkv, grow, init, emit_state):
    b, s, _ = qkv.shape
    nc = s // CHUNK
    has_init = init is not None
    hpb = MLSTM_HEADS_PER_STEP
    nhb = N_HEADS // hpb
    width = hpb * HEAD_DIM
    in_specs = [pl.BlockSpec((1, s, width), lambda i, h: (i, 0, h)),
                pl.BlockSpec((1, s, width), lambda i, h: (i, 0, nhb + h)),
                pl.BlockSpec((1, s, width), lambda i, h: (i, 0, 2 * nhb + h)),
                pl.BlockSpec((1, hpb, 4, nc, CHUNK), lambda i, h: (i, h, 0, 0, 0))]
    args = [qkv, qkv, qkv, grow]
    if has_init:
        c0, n0, m0 = init
        in_specs += [pl.BlockSpec((1, 1, 2, hpb, HEAD_DIM, HEAD_DIM), lambda i, h: (i, 0, 0, h, 0, 0)),
                     pl.BlockSpec((1, hpb, 2, HEAD_DIM), lambda i, h: (i, h, 0, 0)),
                     pl.BlockSpec((1, hpb, 2, 1), lambda i, h: (i, h, 0, 0))]
        args += [c0, n0, m0]
    out_shape = [jax.ShapeDtypeStruct((b, s, D_MODEL), F32)]
    out_specs = [pl.BlockSpec((1, s, width), lambda i, h: (i, 0, h))]
    if emit_state:
        out_shape += [jax.ShapeDtypeStruct((b, 1, 2, N_HEADS, HEAD_DIM, HEAD_DIM), F32),
                      jax.ShapeDtypeStruct((b, N_HEADS, 2, HEAD_DIM), F32),
                      jax.ShapeDtypeStruct((b, N_HEADS, 2, 128), F32)]
        out_specs += [pl.BlockSpec((1, 1, 2, hpb, HEAD_DIM, HEAD_DIM), lambda i, h: (i, 0, 0, h, 0, 0)),
                      pl.BlockSpec((1, hpb, 2, HEAD_DIM), lambda i, h: (i, h, 0, 0)),
                      pl.BlockSpec((1, hpb, 2, 128), lambda i, h: (i, h, 0, 0))]
    return pl.pallas_call(
        functools.partial(_mlstm_kernel, seq=s, heads=hpb, has_init=has_init, emit_state=emit_state),
        grid=(b, nhb),
        in_specs=in_specs,
        out_specs=tuple(out_specs),
        out_shape=tuple(out_shape),
        scratch_shapes=[pltpu.VMEM((2 * hpb, HEAD_DIM, HEAD_DIM), F32), pltpu.VMEM((2 * hpb, 1, HEAD_DIM), F32),
                        pltpu.VMEM((2 * hpb, N_ROW_KINDS, nc, CHUNK), F32)],
        compiler_params=_params(("arbitrary", "arbitrary")),
        name=f"mlstm_{s}",
    )(*args)


def _hyena_kernel(x1_ref, x2_ref, v_ref, w1_ref, w2_ref, wv_ref, b1_ref, b2_ref, bv_ref, skip_ref,
                  g0_ref, g1_ref, fwd_ref, inv_ref, o_ref, zf_s, yf_s, *, seq, p, shift, kk0):
    nb = seq // p
    ng = g0_ref.shape[0]
    bb = x1_ref.shape[0]
    lanes = o_ref.shape[-1]
    sub = lax.broadcasted_iota(jnp.int32, (8, 1), 0)

    def conv_rows(src_ref, bi, r, w_ref, b_ref):
        rr = r * p
        up = src_ref[bi, rr - 8:rr, :][7:8, :] if rr > 0 else jnp.zeros((1, lanes), F32)
        dn = src_ref[bi, rr + p:rr + p + 8, :][0:1, :] if rr + p < seq else jnp.zeros((1, lanes), F32)
        cur = src_ref[bi, rr:rr + p, :]
        prev = pltpu.roll(cur, 1, 0)
        prev = jnp.concatenate([jnp.where(sub == 0, up, prev[0:8]), prev[8:]], axis=0)
        nxt = pltpu.roll(cur, p - 1, 0)
        nxt = jnp.concatenate([nxt[:p - 8], jnp.where(sub == 7, dn, nxt[p - 8:])], axis=0)
        return b_ref[...] + prev * w_ref[0:1, :] + cur * w_ref[1:2, :] + nxt * w_ref[2:3, :]

    fwd = fwd_ref[...]
    inv = inv_ref[...]
    terms = [[(j, i - j + shift - kk0) for j in range(nb) if 0 <= i - j + shift - kk0 < ng] for i in range(nb)]
    batch = range(bb)
    orders = ((g0_ref, x1_ref, w1_ref, b1_ref), (g1_ref, x2_ref, w2_ref, b2_ref))

    def forward_dft(bi, order, j, z):
        zf_s[bi, order, j] = _dot(fwd, z.astype(BF16))

    def spectral_mac(bi, order, i):
        g_ref = orders[order][0]
        for r in range(p // SPEC_ROWS):
            rr = r * SPEC_ROWS
            re = jnp.zeros((SPEC_ROWS, lanes), F32)
            im = jnp.zeros((SPEC_ROWS, lanes), F32)
            for j, kk in terms[i]:
                zre = zf_s[bi, order, j, rr:rr + SPEC_ROWS, :]
                zim = zf_s[bi, order, j, p + rr:p + rr + SPEC_ROWS, :]
                gre = g_ref[kk, rr:rr + SPEC_ROWS, :]
                gim = g_ref[kk, p + rr:p + rr + SPEC_ROWS, :]
                re = re + (zre * gre - zim * gim)
                im = im + (zre * gim + zim * gre)
            if r == 0:
                dc = jnp.zeros((1, lanes), F32)
                ny = jnp.zeros((1, lanes), F32)
                for j, kk in terms[i]:
                    dc = dc + zf_s[bi, order, j, 0:1, :] * g_ref[kk, 0:1, :]
                    ny = ny + zf_s[bi, order, j, p:p + 1, :] * g_ref[kk, p:p + 1, :]
                first = lax.broadcasted_iota(jnp.int32, (SPEC_ROWS, 1), 0) == 0
                re = jnp.where(first, dc, re)
                im = jnp.where(first, ny, im)
            yf_s[bi, i, rr:rr + SPEC_ROWS, :] = re
            yf_s[bi, i, p + rr:p + rr + SPEC_ROWS, :] = im

    def finish_block(bi, order, i, y):
        _, x_ref, w_ref, b_ref = orders[order]
        rows = slice(i * p, (i + 1) * p)
        z = conv_rows(x_ref, bi, i, w_ref, b_ref) * (y + skip_ref[order:order + 1, :] * o_ref[bi, rows, :])
        o_ref[bi, rows, :] = z
        if order + 1 < len(orders):
            forward_dft(bi, order + 1, i, z)

    for j in range(nb):
        for bi in batch:
            z = conv_rows(v_ref, bi, j, wv_ref, bv_ref)
            o_ref[bi, j * p:(j + 1) * p, :] = z
            forward_dft(bi, 0, j, z)
    for order in range(len(orders)):
        pending = {}
        for i in range(nb):
            for bi in batch:
                spectral_mac(bi, order, i)
            for bi in batch:
                pending[bi, i] = _dot(inv, yf_s[bi, i].astype(BF16))
            if i >= 1:
                for bi in batch:
                    finish_block(bi, order, i - 1, pending.pop((bi, i - 1)))
        for bi in batch:
            finish_block(bi, order, nb - 1, pending.pop((bi, nb - 1)))


def _hyena(uh, conv_w, conv_b, skip, spectra, fwd, inv, p, bb):
    b, s, _ = uh.shape
    dc = HYENA_LANES
    nct = D_MODEL // dc
    nb = s // p
    shift, kk0, ng = _spectra_layout(s, p)
    conv_b = conv_b.reshape(1, -1)

    def part(k):
        return pl.BlockSpec((bb, s, dc), lambda c, i, k=k: (i, 0, k * nct + c))

    def wpart(k, rows):
        return pl.BlockSpec((rows, dc), lambda c, i, k=k: (0, k * nct + c))

    def gpart(order):
        return pl.BlockSpec((ng, 2 * p, dc), lambda c, i, order=order: (0, 0, order * nct + c),
                            pipeline_mode=pl.Buffered(1))

    in_specs = [part(0), part(1), part(2), wpart(0, 3), wpart(1, 3), wpart(2, 3),
                wpart(0, 1), wpart(1, 1), wpart(2, 1),
                pl.BlockSpec((2, dc), lambda c, i: (0, c)),
                gpart(0), gpart(1), _const_spec(fwd.shape), _const_spec(inv.shape)]
    return pl.pallas_call(
        functools.partial(_hyena_kernel, seq=s, p=p, shift=shift, kk0=kk0),
        grid=(nct, b // bb),
        in_specs=in_specs,
        out_specs=pl.BlockSpec((bb, s, dc), lambda c, i: (i, 0, c)),
        out_shape=jax.ShapeDtypeStruct((b, s, D_MODEL), F32),
        scratch_shapes=[pltpu.VMEM((bb, 2, nb, 2 * p, dc), F32), pltpu.VMEM((bb, nb, 2 * p, dc), F32)],
        compiler_params=_params(("arbitrary", "arbitrary")),
        name=f"hyena_{s}",
    )(uh, uh, uh, conv_w, conv_w, conv_w, conv_b, conv_b, conv_b, skip, spectra, spectra, fwd, inv)


def _tail_kernel(*refs, has_pos):
    if has_pos:
        x_ref, pos_ref = refs[:2]
        refs = refs[2:]
    else:
        x_ref = refs[0]
        refs = refs[1:]
    (h_ref, o_ref, gm_ref, gh_ref, yh_ref, modb_ref, modc_ref, mnw_ref, n2w_ref, fnw_ref,
     wbm_ref, wbh_ref, wout_ref, w1_ref, b1_ref, w2_ref, b2_ref, y_ref) = refs
    x = x_ref[0]
    if has_pos:
        x = x + pos_ref[...]
    g1 = modb_ref[0, :, 0:D_MODEL]
    sh2 = modb_ref[0, :, D_MODEL:2 * D_MODEL]
    sc2 = modc_ref[0, :, 0:D_MODEL]
    g2 = modc_ref[0, :, D_MODEL:2 * D_MODEL]
    h = h_ref[0]
    heads = [_rms(h[:, hd * HEAD_DIM:(hd + 1) * HEAD_DIM]) for hd in range(N_HEADS)]
    hm = jnp.concatenate(heads, axis=-1) * mnw_ref[...] * _sigmoid(o_ref[0])
    merged = (_sigmoid(gm_ref[0]) * _dot(hm.astype(BF16), wbm_ref[...])
              + _sigmoid(gh_ref[0]) * _dot(yh_ref[0].astype(BF16), wbh_ref[...]))
    x1 = x + g1 * _dot(merged.astype(BF16), wout_ref[...])
    hn2 = ((_rms(x1) * n2w_ref[...]) * (1.0 + sc2) + sh2).astype(BF16)
    ff = b2_ref[...]
    for kc in range(D_FF // D_MODEL):
        sl = slice(kc * D_MODEL, (kc + 1) * D_MODEL)
        a = jnp.maximum(_dot(hn2, w1_ref[:, sl]) + b1_ref[:, sl], 0.0)
        ff = ff + _dot((a * a).astype(BF16), w2_ref[sl, :])
    x2 = x1 + g2 * ff
    y_ref[0] = _rms(x2) * fnw_ref[...]


def _tail(x, pos, h, o, gmh, yh, mod3, mod_row, mnw, n2w, fnw, wbm, wbh, wout, w1, b1, w2, b2):
    b, s, _ = x.shape
    tm = TOKEN_TILE
    has_pos = pos is not None
    tok = pl.BlockSpec((1, tm, D_MODEL), lambda i, j: (i, j, 0))
    in_specs = [tok]
    args = [x]
    if has_pos:
        in_specs.append(pl.BlockSpec((tm, D_MODEL), lambda i, j: (j, 0)))
        args.append(pos)
    in_specs += [tok, tok, tok, pl.BlockSpec((1, tm, D_MODEL), lambda i, j: (i, j, 1)), tok,
                 pl.BlockSpec((1, 1, 2 * D_MODEL), lambda i, j: (mod_row(i), 0, 1)),
                 pl.BlockSpec((1, 1, 2 * D_MODEL), lambda i, j: (mod_row(i), 0, 2))]
    args += [h, o, gmh, gmh, yh, mod3, mod3]
    consts = [mnw, n2w, fnw, wbm, wbh, wout, w1, b1, w2, b2]
    in_specs += [_const_spec(a.shape) for a in consts]
    args += consts
    return pl.pallas_call(
        functools.partial(_tail_kernel, has_pos=has_pos),
        grid=(b, s // tm),
        in_specs=in_specs,
        out_specs=tok,
        out_shape=jax.ShapeDtypeStruct((b, s, D_MODEL), F32),
        compiler_params=_params(("arbitrary", "arbitrary")),
        name="tail_pos" if has_pos else "tail",
    )(*args)


def _gate_rows(gates_t):
    b, _, s = gates_t.shape
    return gates_t.reshape(b, N_HEADS, 4, s // CHUNK, CHUNK)


def kernel(x_prompt, x_sample, state_mlstm_C, state_mlstm_n, state_mlstm_m, c, c_ctx, w_ada, b_ada, norm1_w,
           w_in, b_in, hy_conv_w, hy_conv_b, filt_w1, filt_b1, filt_freq1, filt_w2, filt_b2, filt_freq2,
           filt_w3, hy_skip, mlstm_norm_w, w_br_m, w_br_h, w_out, norm2_w, w_mlp1, b_mlp1, w_mlp2, b_mlp2,
           final_norm_w):
    depth = w_ada.shape[0]
    assert depth == 1, "single-layer configuration"
    l = 0
    dec_b, dec_s, _ = x_sample.shape
    ctx_s = x_prompt.shape[1]

    n_rows = -(-(1 + dec_b) // 8) * 8
    cond = jnp.concatenate([c_ctx[None, :], c, jnp.zeros((n_rows - 1 - dec_b, D_MODEL), F32)], axis=0)
    mod = _modulation(cond, w_ada[l], b_ada[l])
    mod3 = mod.reshape(n_rows, 1, 6 * D_MODEL)

    wt = jnp.transpose(w_in[l])
    bias = b_in[l]
    wa = wt[:QKVO_COLS].astype(BF16)
    ba = bias[:QKVO_COLS].reshape(1, -1)
    head_major = np.arange(N_GATES).reshape(4, N_HEADS).T.reshape(-1)
    wg32 = wt[QKVO_COLS:QKVO_COLS + N_GATES][head_major]
    wg_hi = wg32.astype(BF16)
    wg = jnp.concatenate([wg_hi, (wg32 - wg_hi.astype(F32)).astype(BF16)], axis=0)
    bg = bias[QKVO_COLS:QKVO_COLS + N_GATES][head_major].reshape(-1, 1)
    wr = wt[QKVO_COLS + N_GATES:].astype(BF16)
    br = bias[QKVO_COLS + N_GATES:].reshape(1, -1)
    n1w = norm1_w[l].reshape(1, -1)
    tail_w = (mlstm_norm_w[l].reshape(1, -1), norm2_w[l].reshape(1, -1), final_norm_w.reshape(1, -1),
              w_br_m[l].astype(BF16), w_br_h[l].astype(BF16), w_out[l].astype(BF16),
              w_mlp1[l].astype(BF16), b_mlp1[l].reshape(1, -1), w_mlp2[l].astype(BF16), b_mlp2[l].reshape(1, -1))
    filt = (filt_w1[l], filt_b1[l], filt_freq1[l], filt_w2[l], filt_b2[l], filt_freq2[l], filt_w3[l])

    pos = _pos_table(dec_s)

    def run(x, pos_tab, mod_row, init, emit_state, p, bb):
        s = x.shape[1]
        fwd32, inv32, sign = (jnp.asarray(a) for a in _dft_mats(p))
        spectra = _filter_spectra(s, p, *filt, fwd32, sign)
        qkv, o, gates, uh, gmh = _in_projection(x, pos_tab, mod3, mod_row, n1w, wa, ba, wg, bg, wr, br)
        ml = _mlstm(qkv, _gate_rows(gates), init, emit_state)
        yh = _hyena(uh, hy_conv_w[l], hy_conv_b[l], hy_skip[l], spectra,
                    fwd32.astype(BF16), inv32.astype(BF16), p, bb)
        y = _tail(x, pos_tab, ml[0], o, gmh, yh, mod3, mod_row, *tail_w)
        return y, ml[1:]

    y_prompt, (st_c, st_n, st_m) = run(x_prompt, None, lambda i: 0, None, True, ctx_s, 4)
    init = (state_mlstm_C,
            jnp.transpose(state_mlstm_n[:, l], (0, 2, 1, 3)),
            jnp.transpose(state_mlstm_m[:, l], (0, 2, 1))[..., None])
    y_sample, _ = run(x_sample, pos, lambda i: i + 1, init, False, dec_s // 4, 1)

    new_state_n = jnp.transpose(st_n, (0, 2, 1, 3))[:, None]
    new_state_m = jnp.transpose(st_m[..., 0], (0, 2, 1))[:, None]
    return (y_prompt, y_sample, st_c, new_state_n, new_state_m)
```

```python
import functools
import math

import numpy as np
import jax
import jax.numpy as jnp
from jax import lax
from jax.experimental import pallas as pl
from jax.experimental.pallas import tpu as pltpu

F32 = jnp.float32
BF16 = jnp.bfloat16
HIGHEST = lax.Precision.HIGHEST

D_MODEL = 1024
N_HEADS = 4
HEAD_DIM = 256
D_FF = 4 * D_MODEL
GRID_W = 64
FILT_BANDS = 16
FILT_WIDTH = 64
HYENA_MIN_DECAY = math.log(1e-2) / 1.5
HYENA_MAX_DECAY = math.log(1e-2) / 0.3
CHUNK = 128
RMS_EPS = 1e-6
N_GATES = 4 * N_HEADS
QKVO_COLS = 4 * D_MODEL
REST_COLS = 5 * D_MODEL

V7X_VMEM_BYTES = 64 * 1024 * 1024
VMEM_LIMIT = V7X_VMEM_BYTES - 8 * 1024 * 1024

TOKEN_TILE = 256
HYENA_LANES = 256
FILTER_LANES = 256
SPEC_ROWS = 32
MLSTM_HEADS_PER_STEP = 2


def _params(sem):
    return pltpu.CompilerParams(dimension_semantics=sem, vmem_limit_bytes=VMEM_LIMIT)


def _const_spec(shape):
    nd = len(shape)
    return pl.BlockSpec(shape, lambda *_: (0,) * nd, pipeline_mode=pl.Buffered(1))


def _sigmoid(x):
    return 1.0 / (1.0 + jnp.exp(-x))


def _rms(x):
    return x * lax.rsqrt(jnp.mean(x * x, axis=-1, keepdims=True) + RMS_EPS)


def _dot(a, b):
    return jnp.dot(a, b, preferred_element_type=F32)


def _mod_kernel(c_ref, w_ref, b_ref, o_ref):
    c = c_ref[...]
    s = c * _sigmoid(c)
    o_ref[...] = jnp.dot(s, w_ref[...], precision=HIGHEST, preferred_element_type=F32) + b_ref[...]


def _modulation(cond, w_ada, b_ada):
    rows = cond.shape[0]
    n = w_ada.shape[1]
    tn = 1536
    return pl.pallas_call(
        _mod_kernel,
        grid=(n // tn,),
        in_specs=[pl.BlockSpec((rows, D_MODEL), lambda j: (0, 0)),
                  pl.BlockSpec((D_MODEL, tn), lambda j: (0, j)),
                  pl.BlockSpec((1, tn), lambda j: (0, j))],
        out_specs=pl.BlockSpec((rows, tn), lambda j: (0, j)),
        out_shape=jax.ShapeDtypeStruct((rows, n), F32),
        compiler_params=_params(("arbitrary",)),
        name="modulation",
    )(cond, w_ada, b_ada.reshape(1, n))


def _pos_kernel(o_ref, *, rows):
    quarter = D_MODEL // 4
    half = D_MODEL // 2
    k = lax.broadcasted_iota(jnp.int32, (1, quarter), 1).astype(F32)
    omega = jnp.exp(k * (-math.log(10000.0) / quarter))

    def axis_embed(n):
        p = lax.broadcasted_iota(jnp.int32, (n, 1), 0).astype(F32)
        a = p * omega
        return jnp.concatenate([jnp.sin(a), jnp.cos(a)], axis=-1)

    er = axis_embed(rows)
    ec = axis_embed(GRID_W)
    o_ref[:, :, 0:half] = jnp.broadcast_to(er[:, None, :], (rows, GRID_W, half))
    o_ref[:, :, half:D_MODEL] = jnp.broadcast_to(ec[None, :, :], (rows, GRID_W, half))


def _pos_table(n_tokens):
    rows = n_tokens // GRID_W
    out = pl.pallas_call(
        functools.partial(_pos_kernel, rows=rows),
        out_shape=jax.ShapeDtypeStruct((rows, GRID_W, D_MODEL), F32),
        compiler_params=pltpu.CompilerParams(vmem_limit_bytes=VMEM_LIMIT),
        name="pos_table",
    )()
    return out.reshape(n_tokens, D_MODEL)


def _dft_mats(p):
    n = 2 * p
    idx = np.arange(p, dtype=np.float64)
    ang = 2.0 * np.pi * np.outer(idx, idx) / n
    alt = np.where(np.arange(p) % 2 == 0, 1.0, -1.0)
    fwd = np.zeros((n, p))
    fwd[:p] = np.cos(ang)
    fwd[p] = alt
    fwd[p + 1:] = -np.sin(ang[1:])
    inv = np.zeros((p, n))
    inv[:, :p] = 2.0 * np.cos(ang) / n
    inv[:, 0] = 1.0 / n
    inv[:, p] = alt / n
    inv[:, p + 1:] = -2.0 * np.sin(ang[:, 1:]) / n
    sign = np.concatenate([alt, alt])
    sign[p] = 1.0
    return fwd.astype(np.float32), inv.astype(np.float32), sign.astype(np.float32).reshape(n, 1)


def _filter_kernel(w1t_ref, w1c_ref, w1s_ref, b1_ref, fr1_ref, w2_ref, b2_ref, fr2_ref, w3_ref,
                   fhi_ref, flo_ref, sign_ref, g_ref, feat_ref, *, seq, p, kk0):
    nb = seq // p
    tn = w3_ref.shape[1]
    step = pl.program_id(0)
    idx = lax.broadcasted_iota(jnp.int32, (seq, 1), 0).astype(F32)

    @pl.when(step == 0)
    def _():
        t = idx / float(seq - 1)
        bands = (lax.broadcasted_iota(jnp.int32, (1, FILT_BANDS), 1) + 1).astype(F32)
        ang = ((2.0 * math.pi / seq) * idx) * bands
        pre = (t * w1t_ref[...]
               + jnp.dot(jnp.cos(ang), w1c_ref[...], precision=HIGHEST, preferred_element_type=F32)
               + jnp.dot(jnp.sin(ang), w1s_ref[...], precision=HIGHEST, preferred_element_type=F32)
               + b1_ref[...])
        h1 = jnp.sin(fr1_ref[...] * pre)
        h2 = jnp.sin(fr2_ref[...] * (jnp.dot(h1, w2_ref[...], precision=HIGHEST, preferred_element_type=F32)
                                     + b2_ref[...]))
        feat_ref[...] = h2

    feat = feat_ref[...]
    w3 = w3_ref[...]
    feat_hi = feat.astype(BF16)
    feat_lo = (feat - feat_hi.astype(F32)).astype(BF16)
    w3_hi = w3.astype(BF16)
    w3_lo = (w3 - w3_hi.astype(F32)).astype(BF16)
    hh = _dot(feat_hi, w3_hi) + (_dot(feat_lo, w3_hi) + _dot(feat_hi, w3_lo))
    d0 = lax.rem(step * tn, D_MODEL)
    d = (d0 + lax.broadcasted_iota(jnp.int32, (1, tn), 1)).astype(F32)
    delta = jnp.abs(HYENA_MIN_DECAY + (HYENA_MAX_DECAY - HYENA_MIN_DECAY) * d / float(D_MODEL - 1))
    centre = seq // 2
    dist = jnp.abs(idx - float(centre)) / float(centre)
    hh = hh * jnp.exp(-dist * delta)
    hh = hh / (jnp.sum(jnp.abs(hh), axis=0, keepdims=True) + 1e-6)

    fhi = fhi_ref[...]
    flo = flo_ref[...]
    off = (seq // 2) % p
    if off:
        zeros_hi = jnp.zeros((p - off, tn), F32)
        zeros_lo = jnp.zeros((off, tn), F32)
        blocks = [jnp.concatenate([zeros_hi, hh[0:off]], axis=0)]
        blocks += [hh[off + (k - 1) * p:off + k * p] for k in range(1, nb)]
        blocks += [jnp.concatenate([hh[off + (nb - 1) * p:seq], zeros_lo], axis=0)]
    else:
        blocks = [hh[k * p:(k + 1) * p] for k in range(nb)]
    spec = []
    for hb in blocks:
        hi = hb.astype(BF16)
        lo = (hb - hi.astype(F32)).astype(BF16)
        spec.append(_dot(fhi, hi) + _dot(fhi, lo) + _dot(flo, hi))
    sign = sign_ref[...]
    for k in range(g_ref.shape[0]):
        kk = kk0 + k
        if kk == 0:
            g = spec[0]
        elif kk == len(spec):
            g = sign * spec[kk - 1]
        else:
            g = spec[kk] + sign * spec[kk - 1]
        g_ref[k] = g


def _spectra_layout(seq, p):
    nb = seq // p
    centre = seq // 2
    nblk = nb + (1 if centre % p else 0)
    shift = centre // p + (1 if centre % p else 0)
    kk_lo = max(shift - (nb - 1), 0)
    kk_hi = min(shift + (nb - 1), nblk)
    return shift, kk_lo, kk_hi - kk_lo + 1


def _filter_spectra(seq, p, w1, b1, fr1, w2, b2, fr2, w3, fwd32, sign):
    tn = FILTER_LANES
    ncols = w3.shape[1]
    _, kk0, ng = _spectra_layout(seq, p)
    fhi = fwd32.astype(BF16)
    flo = (fwd32 - fhi.astype(F32)).astype(BF16)
    args = (w1[0:1], w1[1:1 + FILT_BANDS], w1[1 + FILT_BANDS:], b1.reshape(1, -1), fr1.reshape(1, -1),
            w2, b2.reshape(1, -1), fr2.reshape(1, -1))
    in_specs = [_const_spec(a.shape) for a in args]
    in_specs += [pl.BlockSpec((FILT_WIDTH, tn), lambda j: (0, j)),
                 _const_spec(fhi.shape), _const_spec(flo.shape), _const_spec(sign.shape)]
    return pl.pallas_call(
        functools.partial(_filter_kernel, seq=seq, p=p, kk0=kk0),
        grid=(ncols // tn,),
        in_specs=in_specs,
        out_specs=pl.BlockSpec((ng, 2 * p, tn), lambda j: (0, 0, j)),
        out_shape=jax.ShapeDtypeStruct((ng, 2 * p, ncols), F32),
        scratch_shapes=[pltpu.VMEM((seq, FILT_WIDTH), F32)],
        compiler_params=_params(("arbitrary",)),
        name=f"filter_spectra_{seq}",
    )(*args, w3, fhi, flo, sign)


def _inproj_kernel(*refs, has_pos):
    if has_pos:
        x_ref, pos_ref = refs[:2]
        refs = refs[2:]
    else:
        x_ref = refs[0]
        refs = refs[1:]
    (mod_ref, n1w_ref, wa_ref, ba_ref, wg_ref, bg_ref, wr_ref, br_ref,
     qkv_ref, o_ref, gates_ref, uh_ref, gmh_ref) = refs
    x = x_ref[0]
    if has_pos:
        x = x + pos_ref[...]
    sh1 = mod_ref[0, :, 0:D_MODEL]
    sc1 = mod_ref[0, :, D_MODEL:2 * D_MODEL]
    hn = (_rms(x) * n1w_ref[...]) * (1.0 + sc1) + sh1
    hb = hn.astype(BF16)
    nt_dims = (((1,), (1,)), ((), ()))
    for c in range(4):
        sl = slice(c * D_MODEL, (c + 1) * D_MODEL)
        r = lax.dot_general(hb, wa_ref[sl, :], nt_dims, preferred_element_type=F32) + ba_ref[:, sl]
        if c < 3:
            qkv_ref[0, :, sl] = r.astype(BF16)
        else:
            o_ref[0] = r.astype(BF16)
    for c in range(5):
        sl = slice(c * D_MODEL, (c + 1) * D_MODEL)
        r = lax.dot_general(hb, wr_ref[sl, :], nt_dims, preferred_element_type=F32) + br_ref[:, sl]
        if c < 3:
            uh_ref[0, :, sl] = r
        else:
            gmh_ref[0, :, (c - 3) * D_MODEL:(c - 2) * D_MODEL] = r.astype(BF16)
    p1 = lax.dot_general(wg_ref[...], hb, nt_dims, preferred_element_type=F32)
    gates_ref[0] = p1[0:N_GATES] + p1[N_GATES:2 * N_GATES] + bg_ref[...]


def _in_projection(x, pos, mod3, mod_row, n1w, wt, ba, wg, bg, br):
    b, s, _ = x.shape
    tm = TOKEN_TILE
    has_pos = pos is not None
    in_specs = [pl.BlockSpec((1, tm, D_MODEL), lambda i, j: (i, j, 0))]
    args = [x]
    if has_pos:
        in_specs.append(pl.BlockSpec((tm, D_MODEL), lambda i, j: (j, 0)))
        args.append(pos)
    wa_spec = pl.BlockSpec((QKVO_COLS, D_MODEL), lambda i, j: (0, 0), pipeline_mode=pl.Buffered(1))
    wr_spec = pl.BlockSpec((pl.Element(REST_COLS), pl.Element(D_MODEL)), lambda i, j: (QKVO_COLS + N_GATES, 0),
                           pipeline_mode=pl.Buffered(1))
    in_specs += [pl.BlockSpec((1, 1, 2 * D_MODEL), lambda i, j: (mod_row(i), 0, 0)),
                 _const_spec(n1w.shape), wa_spec, _const_spec(ba.shape),
                 _const_spec(wg.shape), _const_spec(bg.shape), wr_spec, _const_spec(br.shape)]
    args += [mod3, n1w, wt, ba, wg, bg, wt, br]
    out_shape = (jax.ShapeDtypeStruct((b, s, 3 * D_MODEL), BF16),
                 jax.ShapeDtypeStruct((b, s, D_MODEL), BF16),
                 jax.ShapeDtypeStruct((b, N_GATES, s), F32),
                 jax.ShapeDtypeStruct((b, s, 3 * D_MODEL), F32),
                 jax.ShapeDtypeStruct((b, s, 2 * D_MODEL), BF16))
    out_specs = tuple(pl.BlockSpec((1, N_GATES, tm), lambda i, j: (i, 0, j)) if k == 2 else
                      pl.BlockSpec((1, tm, sh.shape[2]), lambda i, j: (i, j, 0)) for k, sh in enumerate(out_shape))
    return pl.pallas_call(
        functools.partial(_inproj_kernel, has_pos=has_pos),
        grid=(b, s // tm),
        in_specs=in_specs,
        out_specs=out_specs,
        out_shape=out_shape,
        compiler_params=_params(("arbitrary", "arbitrary")),
        name="in_projection_pos" if has_pos else "in_projection",
    )(*args)


def _log_sigmoid(x):
    return jnp.minimum(x, 0.0) - jnp.log(1.0 + jnp.exp(-jnp.abs(x)))


def _split3(x):
    hi = x.astype(BF16).astype(F32)
    mid = (x - hi).astype(BF16).astype(F32)
    lo = (x - hi - mid).astype(BF16).astype(F32)
    return hi, mid, lo


ROW_E, ROW_B, ROW_M_INTER, ROW_W_STATE, ROW_DECAY, N_ROW_KINDS = 0, 3, 6, 7, 8, 9


def _mlstm_kernel(*refs, seq, heads, has_init, emit_state):
    q_ref, k_ref, v_ref, grow_ref = refs[:4]
    refs = refs[4:]
    if has_init:
        c0_ref, n0_ref, m0_ref = refs[:3]
        refs = refs[3:]
    h_ref = refs[0]
    refs = refs[1:]
    if emit_state:
        c_out, n_out, m_out = refs[:3]
        refs = refs[3:]
    c_s, n_s, rows_s = refs
    t = CHUNK
    nc = seq // t
    scale = HEAD_DIM ** -0.5

    r_io = lax.broadcasted_iota(jnp.int32, (t, t), 0)
    c_io = lax.broadcasted_iota(jnp.int32, (t, t), 1)
    chunk_id = lax.broadcasted_iota(jnp.int32, (nc, 1), 0)

    chains = range(2 * heads)

    m_final = []
    for ch in chains:
        hh, dr = divmod(ch, 2)
        if has_init:
            c_s[ch] = c0_ref[0, 0, dr, hh]
            n_s[ch] = n0_ref[0, hh, dr:dr + 1, :]
            m = m0_ref[0, hh, dr:dr + 1, :]
        else:
            c_s[ch] = jnp.zeros((HEAD_DIM, HEAD_DIM), F32)
            n_s[ch] = jnp.zeros((1, HEAD_DIM), F32)
            m = jnp.zeros((1, 1), F32)
        li = grow_ref[0, hh, 2 * dr]
        lf = _log_sigmoid(grow_ref[0, hh, 2 * dr + 1])
        tri = ((r_io <= c_io) if dr == 0 else (r_io >= c_io)).astype(BF16)
        b = sum(_dot(part.astype(BF16), tri) for part in _split3(lf))
        b_last = jnp.sum(lf, axis=1, keepdims=True)
        a = b_last - b + li
        a_max = jnp.max(a, axis=1, keepdims=True)
        m_before = jnp.zeros((nc, 1), F32)
        m_after = jnp.zeros((nc, 1), F32)
        for c in (range(nc) if dr == 0 else reversed(range(nc))):
            m_new = jnp.maximum(b_last[c:c + 1, :] + m, a_max[c:c + 1, :])
            m_before = jnp.where(chunk_id == c, m, m_before)
            m_after = jnp.where(chunk_id == c, m_new, m_after)
            m = m_new
        m_final.append(m)
        kinds = (*_split3(li - b), *_split3(b), b + m_before, jnp.exp(a - m_after),
                 jnp.broadcast_to(jnp.exp(b_last + m_before - m_after), (nc, t)))
        for idx, x in enumerate(kinds):
            rows_s[ch, idx] = x

    sub8 = lax.broadcasted_iota(jnp.int32, (8, t), 0)
    sub16 = lax.broadcasted_iota(jnp.int32, (16, 1), 0)

    def pick3(base, ch, c, lo, ones_lo, ones_hi):
        out = jnp.where((sub8 >= ones_lo) & (sub8 < ones_hi), 1.0, 0.0)
        for i in range(3):
            out = jnp.where(sub8 == lo + i, rows_s[ch, base + i, pl.ds(c, 1), :], out)
        return out

    def hi_lo_rows(x):
        hi = x.astype(BF16).astype(F32)
        lo = x - hi
        return jnp.where(sub16 == 0, hi, jnp.where(sub16 == 1, lo, 0.0)).astype(BF16)

    nt_dims = (((1,), (1,)), ((), ()))
    tn_dims = (((0,), (0,)), ((), ()))

    def both(i, accumulate):
        dirs = chains
        cs = [i if ch % 2 == 0 else nc - 1 - i for ch in chains]
        r0 = [c * t if isinstance(c, int) else pl.multiple_of(c * t, t) for c in cs]
        cols = [slice((ch // 2) * HEAD_DIM, (ch // 2 + 1) * HEAD_DIM) for ch in chains]
        q = [q_ref[0, pl.ds(r0[d], t), cols[d]] for d in dirs]
        k = [k_ref[0, pl.ds(r0[d], t), cols[d]] for d in dirs]
        v = [v_ref[0, pl.ds(r0[d], t), cols[d]] for d in dirs]
        row = lambda kind, d: rows_s[d, kind, pl.ds(cs[d], 1), :]
        d_t = [lax.dot_general(pick3(ROW_E, d, cs[d], 0, 3, 6), pick3(ROW_B, d, cs[d], 3, 0, 3), tn_dims,
                               preferred_element_type=F32) for d in dirs]
        d_t = [jnp.where((r_io <= c_io) if d % 2 == 0 else (r_io >= c_io), d_t[d], -jnp.inf) for d in dirs]
        m_inter = [row(ROW_M_INTER, d) for d in dirs]
        m_comb = [jnp.maximum(m_inter[d], jnp.max(d_t[d], axis=0, keepdims=True)) for d in dirs]
        pw = [jnp.exp(d_t[d] - m_comb[d]) for d in dirs]
        w_inter = [jnp.exp(m_inter[d] - m_comb[d]) for d in dirs]
        c_prev = [c_s[d] for d in dirs]
        n_prev = [n_s[d] for d in dirs]
        kq = [lax.dot_general(jnp.concatenate([k[d], hi_lo_rows(n_prev[d])], axis=0), q[d], nt_dims,
                              preferred_element_type=F32) for d in dirs]
        qc = [_dot(q[d], c_prev[d].astype(BF16)) for d in dirs]
        w_state = [row(ROW_W_STATE, d) for d in dirs]
        kw = [(k[d].T.astype(F32) * w_state[d]).astype(BF16) for d in dirs]
        sp = [kq[d][0:t] * pw[d] for d in dirs]
        den = [scale * (w_inter[d] * (kq[d][t:t + 1] + kq[d][t + 1:t + 2]) + jnp.sum(sp[d], axis=0, keepdims=True))
               for d in dirs]
        inv_den = [scale / jnp.maximum(jnp.abs(den[d]), jnp.exp(-m_comb[d])) for d in dirs]
        inter_col = [jnp.transpose(jnp.where(sub8 == 0, w_inter[d] * inv_den[d], 0.0))[:, 0:1] for d in dirs]
        sv = [lax.dot_general((sp[d] * inv_den[d]).astype(BF16), v[d], tn_dims, preferred_element_type=F32)
              for d in dirs]
        kv = [_dot(kw[d], v[d]) for d in dirs]
        nk = [_dot(hi_lo_rows(w_state[d]), k[d]) for d in dirs]
        decay = [row(ROW_DECAY, d)[:, 0:1] for d in dirs]
        for d in dirs:
            c_s[d] = decay[d] * c_prev[d] + kv[d]
            n_s[d] = decay[d] * n_prev[d] + nk[d][0:1] + nk[d][1:2]
        for d in dirs:
            h = sv[d] + inter_col[d] * qc[d]
            if accumulate:
                h_ref[0, pl.ds(r0[d], t), cols[d]] += h
            else:
                h_ref[0, pl.ds(r0[d], t), cols[d]] = h

    half = nc // 2
    if nc <= 16:
        for i in range(nc):
            both(i, i >= half)
    else:
        lax.fori_loop(0, half, lambda i, carry: (both(i, False), carry)[1], 0)
        lax.fori_loop(half, nc, lambda i, carry: (both(i, True), carry)[1], 0)

    if emit_state:
        for ch in chains:
            hh, dr = divmod(ch, 2)
            c_out[0, 0, dr, hh] = c_s[ch]
            n_out[0, hh, dr:dr + 1, :] = n_s[ch]
            m_out[0, hh, dr:dr + 1, :] = jnp.broadcast_to(m_final[ch], (1, 128))


def _mlstm(qkv, grow, init, emit_state):
    b, s, _ = qkv.shape
    nc = s // CHUNK
    has_init = init is not None
    hpb = MLSTM_HEADS_PER_STEP
    nhb = N_HEADS // hpb
    width = hpb * HEAD_DIM
    in_specs = [pl.BlockSpec((1, s, width), lambda i, h: (i, 0, h)),
                pl.BlockSpec((1, s, width), lambda i, h: (i, 0, nhb + h)),
                pl.BlockSpec((1, s, width), lambda i, h: (i, 0, 2 * nhb + h)),
                pl.BlockSpec((1, hpb, 4, nc, CHUNK), lambda i, h: (i, h, 0, 0, 0))]
    args = [qkv, qkv, qkv, grow]
    if has_init:
        c0, n0, m0 = init
        in_specs += [pl.BlockSpec((1, 1, 2, hpb, HEAD_DIM, HEAD_DIM), lambda i, h: (i, 0, 0, h, 0, 0)),
                     pl.BlockSpec((1, hpb, 2, HEAD_DIM), lambda i, h: (i, h, 0, 0)),
                     pl.BlockSpec((1, hpb, 2, 1), lambda i, h: (i, h, 0, 0))]
        args += [c0, n0, m0]
    out_shape = [jax.ShapeDtypeStruct((b, s, D_MODEL), F32)]
    out_specs = [pl.BlockSpec((1, s, width), lambda i, h: (i, 0, h))]
    if emit_state:
        out_shape += [jax.ShapeDtypeStruct((b, 1, 2, N_HEADS, HEAD_DIM, HEAD_DIM), F32),
                      jax.ShapeDtypeStruct((b, N_HEADS, 2, HEAD_DIM), F32),
                      jax.ShapeDtypeStruct((b, N_HEADS, 2, 128), F32)]
        out_specs += [pl.BlockSpec((1, 1, 2, hpb, HEAD_DIM, HEAD_DIM), lambda i, h: (i, 0, 0, h, 0, 0)),
                      pl.BlockSpec((1, hpb, 2, HEAD_DIM), lambda i, h: (i, h, 0, 0)),
                      pl.BlockSpec((1, hpb, 2, 128), lambda i, h: (i, h, 0, 0))]
    return pl.pallas_call(
        functools.partial(_mlstm_kernel, seq=s, heads=hpb, has_init=has_init, emit_state=emit_state),
        grid=(b, nhb),
        in_specs=in_specs,
        out_specs=tuple(out_specs),
        out_shape=tuple(out_shape),
        scratch_shapes=[pltpu.VMEM((2 * hpb, HEAD_DIM, HEAD_DIM), F32), pltpu.VMEM((2 * hpb, 1, HEAD_DIM), F32),
                        pltpu.VMEM((2 * hpb, N_ROW_KINDS, nc, CHUNK), F32)],
        compiler_params=_params(("arbitrary", "arbitrary")),
        name=f"mlstm_{s}",
    )(*args)


def _hyena_kernel(x1_ref, x2_ref, v_ref, w1_ref, w2_ref, wv_ref, b1_ref, b2_ref, bv_ref, skip_ref,
                  g0_ref, g1_ref, fwd_ref, inv_ref, o_ref, zf_s, yf_s, *, seq, p, shift, kk0):
    nb = seq // p
    ng = g0_ref.shape[0]
    bb = x1_ref.shape[0]
    lanes = o_ref.shape[-1]
    sub = lax.broadcasted_iota(jnp.int32, (8, 1), 0)

    def conv_rows(src_ref, bi, r, w_ref, b_ref):
        rr = r * p
        up = src_ref[bi, rr - 8:rr, :][7:8, :] if rr > 0 else jnp.zeros((1, lanes), F32)
        dn = src_ref[bi, rr + p:rr + p + 8, :][0:1, :] if rr + p < seq else jnp.zeros((1, lanes), F32)
        cur = src_ref[bi, rr:rr + p, :]
        prev = pltpu.roll(cur, 1, 0)
        prev = jnp.concatenate([jnp.where(sub == 0, up, prev[0:8]), prev[8:]], axis=0)
        nxt = pltpu.roll(cur, p - 1, 0)
        nxt = jnp.concatenate([nxt[:p - 8], jnp.where(sub == 7, dn, nxt[p - 8:])], axis=0)
        return b_ref[...] + prev * w_ref[0:1, :] + cur * w_ref[1:2, :] + nxt * w_ref[2:3, :]

    fwd = fwd_ref[...]
    inv = inv_ref[...]
    terms = [[(j, i - j + shift - kk0) for j in range(nb) if 0 <= i - j + shift - kk0 < ng] for i in range(nb)]
    batch = range(bb)
    orders = ((g0_ref, x1_ref, w1_ref, b1_ref), (g1_ref, x2_ref, w2_ref, b2_ref))

    def forward_dft(bi, order, j, z):
        zf_s[bi, order, j] = _dot(fwd, z.astype(BF16))

    def spectral_mac(bi, order, i):
        g_ref = orders[order][0]
        for r in range(p // SPEC_ROWS):
            rr = r * SPEC_ROWS
            re = jnp.zeros((SPEC_ROWS, lanes), F32)
            im = jnp.zeros((SPEC_ROWS, lanes), F32)
            for j, kk in terms[i]:
                zre = zf_s[bi, order, j, rr:rr + SPEC_ROWS, :]
                zim = zf_s[bi, order, j, p + rr:p + rr + SPEC_ROWS, :]
                gre = g_ref[kk, rr:rr + SPEC_ROWS, :]
                gim = g_ref[kk, p + rr:p + rr + SPEC_ROWS, :]
                re = re + (zre * gre - zim * gim)
                im = im + (zre * gim + zim * gre)
            if r == 0:
                dc = jnp.zeros((1, lanes), F32)
                ny = jnp.zeros((1, lanes), F32)
                for j, kk in terms[i]:
                    dc = dc + zf_s[bi, order, j, 0:1, :] * g_ref[kk, 0:1, :]
                    ny = ny + zf_s[bi, order, j, p:p + 1, :] * g_ref[kk, p:p + 1, :]
                first = lax.broadcasted_iota(jnp.int32, (SPEC_ROWS, 1), 0) == 0
                re = jnp.where(first, dc, re)
                im = jnp.where(first, ny, im)
            yf_s[bi, i, rr:rr + SPEC_ROWS, :] = re
            yf_s[bi, i, p + rr:p + rr + SPEC_ROWS, :] = im

    def finish_block(bi, order, i, y):
        _, x_ref, w_ref, b_ref = orders[order]
        rows = slice(i * p, (i + 1) * p)
        z = conv_rows(x_ref, bi, i, w_ref, b_ref) * (y + skip_ref[order:order + 1, :] * o_ref[bi, rows, :])
        o_ref[bi, rows, :] = z
        if order + 1 < len(orders):
            forward_dft(bi, order + 1, i, z)

    for j in range(nb):
        for bi in batch:
            z = conv_rows(v_ref, bi, j, wv_ref, bv_ref)
            o_ref[bi, j * p:(j + 1) * p, :] = z
            forward_dft(bi, 0, j, z)
    for order in range(len(orders)):
        pending = {}
        for i in range(nb):
            for bi in batch:
                spectral_mac(bi, order, i)
            for bi in batch:
                pending[bi, i] = _dot(inv, yf_s[bi, i].astype(BF16))
            if i >= 1:
                for bi in batch:
                    finish_block(bi, order, i - 1, pending.pop((bi, i - 1)))
        for bi in batch:
            finish_block(bi, order, nb - 1, pending.pop((bi, nb - 1)))


def _hyena(uh, conv_w, conv_b, skip, spectra, fwd, inv, p, bb):
    b, s, _ = uh.shape
    dc = HYENA_LANES
    nct = D_MODEL // dc
    nb = s // p
    shift, kk0, ng = _spectra_layout(s, p)
    conv_b = conv_b.reshape(1, -1)

    def part(k):
        return pl.BlockSpec((bb, s, dc), lambda c, i, k=k: (i, 0, k * nct + c))

    def wpart(k, rows):
        return pl.BlockSpec((rows, dc), lambda c, i, k=k: (0, k * nct + c))

    def gpart(order):
        return pl.BlockSpec((ng, 2 * p, dc), lambda c, i, order=order: (0, 0, order * nct + c),
                            pipeline_mode=pl.Buffered(1))

    in_specs = [part(0), part(1), part(2), wpart(0, 3), wpart(1, 3), wpart(2, 3),
                wpart(0, 1), wpart(1, 1), wpart(2, 1),
                pl.BlockSpec((2, dc), lambda c, i: (0, c)),
                gpart(0), gpart(1), _const_spec(fwd.shape), _const_spec(inv.shape)]
    return pl.pallas_call(
        functools.partial(_hyena_kernel, seq=s, p=p, shift=shift, kk0=kk0),
        grid=(nct, b // bb),
        in_specs=in_specs,
        out_specs=pl.BlockSpec((bb, s, dc), lambda c, i: (i, 0, c)),
        out_shape=jax.ShapeDtypeStruct((b, s, D_MODEL), F32),
        scratch_shapes=[pltpu.VMEM((bb, 2, nb, 2 * p, dc), F32), pltpu.VMEM((bb, nb, 2 * p, dc), F32)],
        compiler_params=_params(("arbitrary", "arbitrary")),
        name=f"hyena_{s}",
    )(uh, uh, uh, conv_w, conv_w, conv_w, conv_b, conv_b, conv_b, skip, spectra, spectra, fwd, inv)


def _tail_kernel(*refs, has_pos):
    if has_pos:
        x_ref, pos_ref = refs[:2]
        refs = refs[2:]
    else:
        x_ref = refs[0]
        refs = refs[1:]
    (h_ref, o_ref, gm_ref, gh_ref, yh_ref, modb_ref, modc_ref, mnw_ref, n2w_ref, fnw_ref,
     wbm_ref, wbh_ref, wout_ref, w1_ref, b1_ref, w2_ref, b2_ref, y_ref) = refs
    x = x_ref[0]
    if has_pos:
        x = x + pos_ref[...]
    g1 = modb_ref[0, :, 0:D_MODEL]
    sh2 = modb_ref[0, :, D_MODEL:2 * D_MODEL]
    sc2 = modc_ref[0, :, 0:D_MODEL]
    g2 = modc_ref[0, :, D_MODEL:2 * D_MODEL]
    h = h_ref[0]
    heads = [_rms(h[:, hd * HEAD_DIM:(hd + 1) * HEAD_DIM]) for hd in range(N_HEADS)]
    hm = jnp.concatenate(heads, axis=-1) * mnw_ref[...] * _sigmoid(o_ref[0].astype(F32))
    merged = (_sigmoid(gm_ref[0].astype(F32)) * _dot(hm.astype(BF16), wbm_ref[...])
              + _sigmoid(gh_ref[0].astype(F32)) * _dot(yh_ref[0].astype(BF16), wbh_ref[...]))
    x1 = x + g1 * _dot(merged.astype(BF16), wout_ref[...])
    hn2 = ((_rms(x1) * n2w_ref[...]) * (1.0 + sc2) + sh2).astype(BF16)
    ff = b2_ref[...]
    for kc in range(D_FF // D_MODEL):
        sl = slice(kc * D_MODEL, (kc + 1) * D_MODEL)
        a = jnp.maximum(_dot(hn2, w1_ref[:, sl]) + b1_ref[:, sl], 0.0)
        ff = ff + _dot((a * a).astype(BF16), w2_ref[sl, :])
    x2 = x1 + g2 * ff
    y_ref[0] = _rms(x2) * fnw_ref[...]


def _tail(x, pos, h, o, gmh, yh, mod3, mod_row, mnw, n2w, fnw, wbm, wbh, wout, w1, b1, w2, b2):
    b, s, _ = x.shape
    tm = TOKEN_TILE
    has_pos = pos is not None
    tok = pl.BlockSpec((1, tm, D_MODEL), lambda i, j: (i, j, 0))
    in_specs = [tok]
    args = [x]
    if has_pos:
        in_specs.append(pl.BlockSpec((tm, D_MODEL), lambda i, j: (j, 0)))
        args.append(pos)
    in_specs += [tok, tok, tok, pl.BlockSpec((1, tm, D_MODEL), lambda i, j: (i, j, 1)), tok,
                 pl.BlockSpec((1, 1, 2 * D_MODEL), lambda i, j: (mod_row(i), 0, 1)),
                 pl.BlockSpec((1, 1, 2 * D_MODEL), lambda i, j: (mod_row(i), 0, 2))]
    args += [h, o, gmh, gmh, yh, mod3, mod3]
    consts = [mnw, n2w, fnw, wbm, wbh, wout, w1, b1, w2, b2]
    in_specs += [_const_spec(a.shape) for a in consts]
    args += consts
    return pl.pallas_call(
        functools.partial(_tail_kernel, has_pos=has_pos),
        grid=(b, s // tm),
        in_specs=in_specs,
        out_specs=tok,
        out_shape=jax.ShapeDtypeStruct((b, s, D_MODEL), F32),
        compiler_params=_params(("arbitrary", "arbitrary")),
        name="tail_pos" if has_pos else "tail",
    )(*args)


def _gate_rows(gates_t):
    b, _, s = gates_t.shape
    return gates_t.reshape(b, N_HEADS, 4, s // CHUNK, CHUNK)


def kernel(x_prompt, x_sample, state_mlstm_C, state_mlstm_n, state_mlstm_m, c, c_ctx, w_ada, b_ada, norm1_w,
           w_in, b_in, hy_conv_w, hy_conv_b, filt_w1, filt_b1, filt_freq1, filt_w2, filt_b2, filt_freq2,
           filt_w3, hy_skip, mlstm_norm_w, w_br_m, w_br_h, w_out, norm2_w, w_mlp1, b_mlp1, w_mlp2, b_mlp2,
           final_norm_w):
    depth = w_ada.shape[0]
    assert depth == 1, "single-layer configuration"
    l = 0
    dec_b, dec_s, _ = x_sample.shape
    ctx_s = x_prompt.shape[1]

    n_rows = -(-(1 + dec_b) // 8) * 8
    cond = jnp.concatenate([c_ctx[None, :], c, jnp.zeros((n_rows - 1 - dec_b, D_MODEL), F32)], axis=0)
    mod = _modulation(cond, w_ada[l], b_ada[l])
    mod3 = mod.reshape(n_rows, 1, 6 * D_MODEL)

    wt32 = jnp.transpose(w_in[l])
    wt = wt32.astype(BF16)
    bias = b_in[l]
    ba = bias[:QKVO_COLS].reshape(1, -1)
    head_major = np.arange(N_GATES).reshape(4, N_HEADS).T.reshape(-1)
    wg32 = wt32[QKVO_COLS:QKVO_COLS + N_GATES][head_major]
    wg_hi = wg32.astype(BF16)
    wg = jnp.concatenate([wg_hi, (wg32 - wg_hi.astype(F32)).astype(BF16)], axis=0)
    bg = bias[QKVO_COLS:QKVO_COLS + N_GATES][head_major].reshape(-1, 1)
    br = bias[QKVO_COLS + N_GATES:].reshape(1, -1)
    n1w = norm1_w[l].reshape(1, -1)
    tail_w = (mlstm_norm_w[l].reshape(1, -1), norm2_w[l].reshape(1, -1), final_norm_w.reshape(1, -1),
              w_br_m[l].astype(BF16), w_br_h[l].astype(BF16), w_out[l].astype(BF16),
              w_mlp1[l].astype(BF16), b_mlp1[l].reshape(1, -1), w_mlp2[l].astype(BF16), b_mlp2[l].reshape(1, -1))
    filt = (filt_w1[l], filt_b1[l], filt_freq1[l], filt_w2[l], filt_b2[l], filt_freq2[l], filt_w3[l])

    pos = _pos_table(dec_s)

    def run(x, pos_tab, mod_row, init, emit_state, p, bb):
        s = x.shape[1]
        fwd32, inv32, sign = (jnp.asarray(a) for a in _dft_mats(p))
        spectra = _filter_spectra(s, p, *filt, fwd32, sign)
        qkv, o, gates, uh, gmh = _in_projection(x, pos_tab, mod3, mod_row, n1w, wt, ba, wg, bg, br)
        ml = _mlstm(qkv, _gate_rows(gates), init, emit_state)
        yh = _hyena(uh, hy_conv_w[l], hy_conv_b[l], hy_skip[l], spectra,
                    fwd32.astype(BF16), inv32.astype(BF16), p, bb)
        y = _tail(x, pos_tab, ml[0], o, gmh, yh, mod3, mod_row, *tail_w)
        return y, ml[1:]

    y_prompt, (st_c, st_n, st_m) = run(x_prompt, None, lambda i: 0, None, True, ctx_s, 4)
    init = (state_mlstm_C,
            jnp.transpose(state_mlstm_n[:, l], (0, 2, 1, 3)),
            jnp.transpose(state_mlstm_m[:, l], (0, 2, 1))[..., None])
    y_sample, _ = run(x_sample, pos, lambda i: i + 1, init, False, dec_s // 4, 1)

    new_state_n = jnp.transpose(st_n, (0, 2, 1, 3))[:, None]
    new_state_m = jnp.transpose(st_m[..., 0], (0, 2, 1))[:, None]
    return (y_prompt, y_sample, st_c, new_state_n, new_state_m)
```

```python
import functools
import math

import numpy as np
import jax
import jax.numpy as jnp
from jax import lax
from jax.experimental import pallas as pl
from jax.experimental.pallas import tpu as pltpu

F32 = jnp.float32
BF16 = jnp.bfloat16
HIGHEST = lax.Precision.HIGHEST

D_MODEL = 1024
N_HEADS = 4
HEAD_DIM = 256
D_FF = 4 * D_MODEL
GRID_W = 64
FILT_BANDS = 16
FILT_WIDTH = 64
HYENA_MIN_DECAY = math.log(1e-2) / 1.5
HYENA_MAX_DECAY = math.log(1e-2) / 0.3
CHUNK = 128
RMS_EPS = 1e-6
N_GATES = 4 * N_HEADS
QKVO_COLS = 4 * D_MODEL
REST_COLS = 5 * D_MODEL

V7X_VMEM_BYTES = 64 * 1024 * 1024
VMEM_LIMIT = V7X_VMEM_BYTES - 8 * 1024 * 1024

TOKEN_TILE = 512
TAIL_TOKEN_TILE = 512
HYENA_LANES = 256
FILTER_LANES = 256
SPEC_ROWS = 32
MLSTM_HEADS_PER_STEP = 2


def _params(sem):
    return pltpu.CompilerParams(dimension_semantics=sem, vmem_limit_bytes=VMEM_LIMIT)


def _const_spec(shape):
    nd = len(shape)
    return pl.BlockSpec(shape, lambda *_: (0,) * nd, pipeline_mode=pl.Buffered(1))


def _sigmoid(x):
    return 1.0 / (1.0 + jnp.exp(-x))


def _rms(x):
    return x * lax.rsqrt(jnp.mean(x * x, axis=-1, keepdims=True) + RMS_EPS)


def _dot(a, b):
    return jnp.dot(a, b, preferred_element_type=F32)


def _mod_kernel(c_ref, w_ref, b_ref, o_ref):
    c = c_ref[...]
    s = c * _sigmoid(c)
    o_ref[...] = jnp.dot(s, w_ref[...], precision=HIGHEST, preferred_element_type=F32) + b_ref[...]


def _modulation(cond, w_ada, b_ada):
    rows = cond.shape[0]
    n = w_ada.shape[1]
    tn = 1536
    return pl.pallas_call(
        _mod_kernel,
        grid=(n // tn,),
        in_specs=[pl.BlockSpec((rows, D_MODEL), lambda j: (0, 0)),
                  pl.BlockSpec((D_MODEL, tn), lambda j: (0, j)),
                  pl.BlockSpec((1, tn), lambda j: (0, j))],
        out_specs=pl.BlockSpec((rows, tn), lambda j: (0, j)),
        out_shape=jax.ShapeDtypeStruct((rows, n), F32),
        compiler_params=_params(("arbitrary",)),
        name="modulation",
    )(cond, w_ada, b_ada.reshape(1, n))


def _pos_kernel(o_ref, *, rows):
    quarter = D_MODEL // 4
    half = D_MODEL // 2
    k = lax.broadcasted_iota(jnp.int32, (1, quarter), 1).astype(F32)
    omega = jnp.exp(k * (-math.log(10000.0) / quarter))

    def axis_embed(n):
        p = lax.broadcasted_iota(jnp.int32, (n, 1), 0).astype(F32)
        a = p * omega
        return jnp.concatenate([jnp.sin(a), jnp.cos(a)], axis=-1)

    er = axis_embed(rows)
    ec = axis_embed(GRID_W)
    o_ref[:, :, 0:half] = jnp.broadcast_to(er[:, None, :], (rows, GRID_W, half))
    o_ref[:, :, half:D_MODEL] = jnp.broadcast_to(ec[None, :, :], (rows, GRID_W, half))


def _pos_table(n_tokens):
    rows = n_tokens // GRID_W
    out = pl.pallas_call(
        functools.partial(_pos_kernel, rows=rows),
        out_shape=jax.ShapeDtypeStruct((rows, GRID_W, D_MODEL), F32),
        compiler_params=pltpu.CompilerParams(vmem_limit_bytes=VMEM_LIMIT),
        name="pos_table",
    )()
    return out.reshape(n_tokens, D_MODEL)


def _dft_mats(p):
    n = 2 * p
    idx = np.arange(p, dtype=np.float64)
    ang = 2.0 * np.pi * np.outer(idx, idx) / n
    alt = np.where(np.arange(p) % 2 == 0, 1.0, -1.0)
    fwd = np.zeros((n, p))
    fwd[:p] = np.cos(ang)
    fwd[p] = alt
    fwd[p + 1:] = -np.sin(ang[1:])
    inv = np.zeros((p, n))
    inv[:, :p] = 2.0 * np.cos(ang) / n
    inv[:, 0] = 1.0 / n
    inv[:, p] = alt / n
    inv[:, p + 1:] = -2.0 * np.sin(ang[:, 1:]) / n
    sign = np.concatenate([alt, alt])
    sign[p] = 1.0
    return fwd.astype(np.float32), inv.astype(np.float32), sign.astype(np.float32).reshape(n, 1)


def _filter_kernel(w1t_ref, w1c_ref, w1s_ref, b1_ref, fr1_ref, w2_ref, b2_ref, fr2_ref, w3_ref,
                   fhi_ref, flo_ref, sign_ref, g_ref, feat_ref, *, seq, p, kk0):
    nb = seq // p
    tn = w3_ref.shape[1]
    step = pl.program_id(0)
    idx = lax.broadcasted_iota(jnp.int32, (seq, 1), 0).astype(F32)

    @pl.when(step == 0)
    def _():
        t = idx / float(seq - 1)
        bands = (lax.broadcasted_iota(jnp.int32, (1, FILT_BANDS), 1) + 1).astype(F32)
        ang = ((2.0 * math.pi / seq) * idx) * bands
        pre = (t * w1t_ref[...]
               + jnp.dot(jnp.cos(ang), w1c_ref[...], precision=HIGHEST, preferred_element_type=F32)
               + jnp.dot(jnp.sin(ang), w1s_ref[...], precision=HIGHEST, preferred_element_type=F32)
               + b1_ref[...])
        h1 = jnp.sin(fr1_ref[...] * pre)
        h2 = jnp.sin(fr2_ref[...] * (jnp.dot(h1, w2_ref[...], precision=HIGHEST, preferred_element_type=F32)
                                     + b2_ref[...]))
        feat_ref[...] = h2

    feat = feat_ref[...]
    w3 = w3_ref[...]
    feat_hi = feat.astype(BF16)
    feat_lo = (feat - feat_hi.astype(F32)).astype(BF16)
    w3_hi = w3.astype(BF16)
    w3_lo = (w3 - w3_hi.astype(F32)).astype(BF16)
    hh = _dot(feat_hi, w3_hi) + (_dot(feat_lo, w3_hi) + _dot(feat_hi, w3_lo))
    d0 = lax.rem(step * tn, D_MODEL)
    d = (d0 + lax.broadcasted_iota(jnp.int32, (1, tn), 1)).astype(F32)
    delta = jnp.abs(HYENA_MIN_DECAY + (HYENA_MAX_DECAY - HYENA_MIN_DECAY) * d / float(D_MODEL - 1))
    centre = seq // 2
    dist = jnp.abs(idx - float(centre)) / float(centre)
    hh = hh * jnp.exp(-dist * delta)
    hh = hh / (jnp.sum(jnp.abs(hh), axis=0, keepdims=True) + 1e-6)

    fhi = fhi_ref[...]
    flo = flo_ref[...]
    off = (seq // 2) % p
    if off:
        zeros_hi = jnp.zeros((p - off, tn), F32)
        zeros_lo = jnp.zeros((off, tn), F32)
        blocks = [jnp.concatenate([zeros_hi, hh[0:off]], axis=0)]
        blocks += [hh[off + (k - 1) * p:off + k * p] for k in range(1, nb)]
        blocks += [jnp.concatenate([hh[off + (nb - 1) * p:seq], zeros_lo], axis=0)]
    else:
        blocks = [hh[k * p:(k + 1) * p] for k in range(nb)]
    spec = []
    for hb in blocks:
        hi = hb.astype(BF16)
        lo = (hb - hi.astype(F32)).astype(BF16)
        spec.append(_dot(fhi, hi) + _dot(fhi, lo) + _dot(flo, hi))
    sign = sign_ref[...]
    for k in range(g_ref.shape[0]):
        kk = kk0 + k
        if kk == 0:
            g = spec[0]
        elif kk == len(spec):
            g = sign * spec[kk - 1]
        else:
            g = spec[kk] + sign * spec[kk - 1]
        g_ref[k] = g


def _spectra_layout(seq, p):
    nb = seq // p
    centre = seq // 2
    nblk = nb + (1 if centre % p else 0)
    shift = centre // p + (1 if centre % p else 0)
    kk_lo = max(shift - (nb - 1), 0)
    kk_hi = min(shift + (nb - 1), nblk)
    return shift, kk_lo, kk_hi - kk_lo + 1


def _filter_spectra(seq, p, w1, b1, fr1, w2, b2, fr2, w3, fwd32, sign):
    tn = FILTER_LANES
    ncols = w3.shape[1]
    _, kk0, ng = _spectra_layout(seq, p)
    fhi = fwd32.astype(BF16)
    flo = (fwd32 - fhi.astype(F32)).astype(BF16)
    args = (w1[0:1], w1[1:1 + FILT_BANDS], w1[1 + FILT_BANDS:], b1.reshape(1, -1), fr1.reshape(1, -1),
            w2, b2.reshape(1, -1), fr2.reshape(1, -1))
    in_specs = [_const_spec(a.shape) for a in args]
    in_specs += [pl.BlockSpec((FILT_WIDTH, tn), lambda j: (0, j)),
                 _const_spec(fhi.shape), _const_spec(flo.shape), _const_spec(sign.shape)]
    return pl.pallas_call(
        functools.partial(_filter_kernel, seq=seq, p=p, kk0=kk0),
        grid=(ncols // tn,),
        in_specs=in_specs,
        out_specs=pl.BlockSpec((ng, 2 * p, tn), lambda j: (0, 0, j)),
        out_shape=jax.ShapeDtypeStruct((ng, 2 * p, ncols), F32),
        scratch_shapes=[pltpu.VMEM((seq, FILT_WIDTH), F32)],
        compiler_params=_params(("arbitrary",)),
        name=f"filter_spectra_{seq}",
    )(*args, w3, fhi, flo, sign)


def _inproj_kernel(*refs, has_pos):
    if has_pos:
        x_ref, pos_ref = refs[:2]
        refs = refs[2:]
    else:
        x_ref = refs[0]
        refs = refs[1:]
    (mod_ref, n1w_ref, wa_ref, ba_ref, wg_ref, bg_ref, wr_ref, br_ref,
     qkv_ref, o_ref, gates_ref, uh_ref, gmh_ref) = refs
    x = x_ref[0]
    if has_pos:
        x = x + pos_ref[...]
    sh1 = mod_ref[0, :, 0:D_MODEL]
    sc1 = mod_ref[0, :, D_MODEL:2 * D_MODEL]
    hn = (_rms(x) * n1w_ref[...]) * (1.0 + sc1) + sh1
    hb = hn.astype(BF16)
    nt_dims = (((1,), (1,)), ((), ()))
    for c in range(4):
        sl = slice(c * D_MODEL, (c + 1) * D_MODEL)
        r = lax.dot_general(hb, wa_ref[sl, :], nt_dims, preferred_element_type=F32) + ba_ref[:, sl]
        if c < 3:
            qkv_ref[0, :, sl] = r.astype(BF16)
        else:
            o_ref[0] = r.astype(BF16)
    for c in range(5):
        sl = slice(c * D_MODEL, (c + 1) * D_MODEL)
        r = lax.dot_general(hb, wr_ref[sl, :], nt_dims, preferred_element_type=F32) + br_ref[:, sl]
        if c < 3:
            uh_ref[0, :, sl] = r
        else:
            gmh_ref[0, :, (c - 3) * D_MODEL:(c - 2) * D_MODEL] = r.astype(BF16)
    p1 = lax.dot_general(wg_ref[...], hb, nt_dims, preferred_element_type=F32)
    gates_ref[0] = p1[0:N_GATES] + p1[N_GATES:2 * N_GATES] + bg_ref[...]


def _in_projection(x, pos, mod3, mod_row, n1w, wt, ba, wg, bg, br):
    b, s, _ = x.shape
    tm = min(TOKEN_TILE, s)
    has_pos = pos is not None
    in_specs = [pl.BlockSpec((1, tm, D_MODEL), lambda i, j: (i, j, 0))]
    args = [x]
    if has_pos:
        in_specs.append(pl.BlockSpec((tm, D_MODEL), lambda i, j: (j, 0)))
        args.append(pos)
    wa_spec = pl.BlockSpec((QKVO_COLS, D_MODEL), lambda i, j: (0, 0), pipeline_mode=pl.Buffered(1))
    wr_spec = pl.BlockSpec((pl.Element(REST_COLS), pl.Element(D_MODEL)), lambda i, j: (QKVO_COLS + N_GATES, 0),
                           pipeline_mode=pl.Buffered(1))
    in_specs += [pl.BlockSpec((1, 1, 2 * D_MODEL), lambda i, j: (mod_row(i), 0, 0)),
                 _const_spec(n1w.shape), wa_spec, _const_spec(ba.shape),
                 _const_spec(wg.shape), _const_spec(bg.shape), wr_spec, _const_spec(br.shape)]
    args += [mod3, n1w, wt, ba, wg, bg, wt, br]
    out_shape = (jax.ShapeDtypeStruct((b, s, 3 * D_MODEL), BF16),
                 jax.ShapeDtypeStruct((b, s, D_MODEL), BF16),
                 jax.ShapeDtypeStruct((b, N_GATES, s), F32),
                 jax.ShapeDtypeStruct((b, s, 3 * D_MODEL), F32),
                 jax.ShapeDtypeStruct((b, s, 2 * D_MODEL), BF16))
    out_specs = tuple(pl.BlockSpec((1, N_GATES, tm), lambda i, j: (i, 0, j)) if k == 2 else
                      pl.BlockSpec((1, tm, sh.shape[2]), lambda i, j: (i, j, 0)) for k, sh in enumerate(out_shape))
    return pl.pallas_call(
        functools.partial(_inproj_kernel, has_pos=has_pos),
        grid=(b, s // tm),
        in_specs=in_specs,
        out_specs=out_specs,
        out_shape=out_shape,
        compiler_params=_params(("arbitrary", "arbitrary")),
        name="in_projection_pos" if has_pos else "in_projection",
    )(*args)


def _log_sigmoid(x):
    return jnp.minimum(x, 0.0) - jnp.log(1.0 + jnp.exp(-jnp.abs(x)))


def _split3(x):
    hi = x.astype(BF16).astype(F32)
    mid = (x - hi).astype(BF16).astype(F32)
    lo = (x - hi - mid).astype(BF16).astype(F32)
    return hi, mid, lo


ROW_E, ROW_B, ROW_M_INTER, ROW_W_STATE, ROW_DECAY, N_ROW_KINDS = 0, 3, 6, 7, 8, 9


def _mlstm_kernel(*refs, seq, heads, has_init, emit_state):
    q_ref, k_ref, v_ref, grow_ref = refs[:4]
    refs = refs[4:]
    if has_init:
        c0_ref, n0_ref, m0_ref = refs[:3]
        refs = refs[3:]
    h_ref = refs[0]
    refs = refs[1:]
    if emit_state:
        c_out, n_out, m_out = refs[:3]
        refs = refs[3:]
    c_s, n_s, rows_s = refs
    t = CHUNK
    nc = seq // t
    scale = HEAD_DIM ** -0.5

    r_io = lax.broadcasted_iota(jnp.int32, (t, t), 0)
    c_io = lax.broadcasted_iota(jnp.int32, (t, t), 1)
    chunk_id = lax.broadcasted_iota(jnp.int32, (nc, 1), 0)

    chains = range(2 * heads)

    m_final = []
    for ch in chains:
        hh, dr = divmod(ch, 2)
        if has_init:
            c_s[ch] = c0_ref[0, 0, dr, hh]
            n_s[ch] = n0_ref[0, hh, dr:dr + 1, :]
            m = m0_ref[0, hh, dr:dr + 1, :]
        else:
            c_s[ch] = jnp.zeros((HEAD_DIM, HEAD_DIM), F32)
            n_s[ch] = jnp.zeros((1, HEAD_DIM), F32)
            m = jnp.zeros((1, 1), F32)
        li = grow_ref[0, hh, 2 * dr]
        lf = _log_sigmoid(grow_ref[0, hh, 2 * dr + 1])
        tri = ((r_io <= c_io) if dr == 0 else (r_io >= c_io)).astype(BF16)
        b = sum(_dot(part.astype(BF16), tri) for part in _split3(lf))
        b_last = jnp.sum(lf, axis=1, keepdims=True)
        a = b_last - b + li
        a_max = jnp.max(a, axis=1, keepdims=True)
        m_before = jnp.zeros((nc, 1), F32)
        m_after = jnp.zeros((nc, 1), F32)
        for c in (range(nc) if dr == 0 else reversed(range(nc))):
            m_new = jnp.maximum(b_last[c:c + 1, :] + m, a_max[c:c + 1, :])
            m_before = jnp.where(chunk_id == c, m, m_before)
            m_after = jnp.where(chunk_id == c, m_new, m_after)
            m = m_new
        m_final.append(m)
        kinds = (*_split3(li - b), *_split3(b), b + m_before, jnp.exp(a - m_after),
                 jnp.broadcast_to(jnp.exp(b_last + m_before - m_after), (nc, t)))
        for idx, x in enumerate(kinds):
            rows_s[ch, idx] = x

    sub8 = lax.broadcasted_iota(jnp.int32, (8, t), 0)
    sub16 = lax.broadcasted_iota(jnp.int32, (16, 1), 0)

    def pick3(base, ch, c, lo, ones_lo, ones_hi):
        out = jnp.where((sub8 >= ones_lo) & (sub8 < ones_hi), 1.0, 0.0)
        for i in range(3):
            out = jnp.where(sub8 == lo + i, rows_s[ch, base + i, pl.ds(c, 1), :], out)
        return out

    def hi_lo_rows(x):
        hi = x.astype(BF16).astype(F32)
        lo = x - hi
        return jnp.where(sub16 == 0, hi, jnp.where(sub16 == 1, lo, 0.0)).astype(BF16)

    nt_dims = (((1,), (1,)), ((), ()))
    tn_dims = (((0,), (0,)), ((), ()))

    def both(i, accumulate):
        dirs = chains
        cs = [i if ch % 2 == 0 else nc - 1 - i for ch in chains]
        r0 = [c * t if isinstance(c, int) else pl.multiple_of(c * t, t) for c in cs]
        cols = [slice((ch // 2) * HEAD_DIM, (ch // 2 + 1) * HEAD_DIM) for ch in chains]
        q = [q_ref[0, pl.ds(r0[d], t), cols[d]] for d in dirs]
        k = [k_ref[0, pl.ds(r0[d], t), cols[d]] for d in dirs]
        v = [v_ref[0, pl.ds(r0[d], t), cols[d]] for d in dirs]
        row = lambda kind, d: rows_s[d, kind, pl.ds(cs[d], 1), :]
        d_t = [lax.dot_general(pick3(ROW_E, d, cs[d], 0, 3, 6), pick3(ROW_B, d, cs[d], 3, 0, 3), tn_dims,
                               preferred_element_type=F32) for d in dirs]
        d_t = [jnp.where((r_io <= c_io) if d % 2 == 0 else (r_io >= c_io), d_t[d], -jnp.inf) for d in dirs]
        m_inter = [row(ROW_M_INTER, d) for d in dirs]
        m_comb = [jnp.maximum(m_inter[d], jnp.max(d_t[d], axis=0, keepdims=True)) for d in dirs]
        pw = [jnp.exp(d_t[d] - m_comb[d]) for d in dirs]
        w_inter = [jnp.exp(m_inter[d] - m_comb[d]) for d in dirs]
        c_prev = [c_s[d] for d in dirs]
        n_prev = [n_s[d] for d in dirs]
        kq = [lax.dot_general(jnp.concatenate([k[d], hi_lo_rows(n_prev[d])], axis=0), q[d], nt_dims,
                              preferred_element_type=F32) for d in dirs]
        qc = [_dot(q[d], c_prev[d].astype(BF16)) for d in dirs]
        w_state = [row(ROW_W_STATE, d) for d in dirs]
        kw = [(k[d].T.astype(F32) * w_state[d]).astype(BF16) for d in dirs]
        sp = [kq[d][0:t] * pw[d] for d in dirs]
        den = [scale * (w_inter[d] * (kq[d][t:t + 1] + kq[d][t + 1:t + 2]) + jnp.sum(sp[d], axis=0, keepdims=True))
               for d in dirs]
        inv_den = [scale / jnp.maximum(jnp.abs(den[d]), jnp.exp(-m_comb[d])) for d in dirs]
        inter_col = [jnp.transpose(jnp.where(sub8 == 0, w_inter[d] * inv_den[d], 0.0))[:, 0:1] for d in dirs]
        sv = [lax.dot_general((sp[d] * inv_den[d]).astype(BF16), v[d], tn_dims, preferred_element_type=F32)
              for d in dirs]
        kv = [_dot(kw[d], v[d]) for d in dirs]
        nk = [_dot(hi_lo_rows(w_state[d]), k[d]) for d in dirs]
        decay = [row(ROW_DECAY, d)[:, 0:1] for d in dirs]
        for d in dirs:
            c_s[d] = decay[d] * c_prev[d] + kv[d]
            n_s[d] = decay[d] * n_prev[d] + nk[d][0:1] + nk[d][1:2]
        for d in dirs:
            h = sv[d] + inter_col[d] * qc[d]
            if accumulate:
                h_ref[0, pl.ds(r0[d], t), cols[d]] += h
            else:
                h_ref[0, pl.ds(r0[d], t), cols[d]] = h

    half = nc // 2
    if nc <= 16:
        for i in range(nc):
            both(i, i >= half)
    else:
        lax.fori_loop(0, half, lambda i, carry: (both(i, False), carry)[1], 0)
        lax.fori_loop(half, nc, lambda i, carry: (both(i, True), carry)[1], 0)

    if emit_state:
        for ch in chains:
            hh, dr = divmod(ch, 2)
            c_out[0, 0, dr, hh] = c_s[ch]
            n_out[0, hh, dr:dr + 1, :] = n_s[ch]
            m_out[0, hh, dr:dr + 1, :] = jnp.broadcast_to(m_final[ch], (1, 128))


def _mlstm(qkv, grow, init, emit_state):
    b, s, _ = qkv.shape
    nc = s // CHUNK
    has_init = init is not None
    hpb = N_HEADS if nc <= 2 else MLSTM_HEADS_PER_STEP
    nhb = N_HEADS // hpb
    width = hpb * HEAD_DIM
    in_specs = [pl.BlockSpec((1, s, width), lambda i, h: (i, 0, h)),
                pl.BlockSpec((1, s, width), lambda i, h: (i, 0, nhb + h)),
                pl.BlockSpec((1, s, width), lambda i, h: (i, 0, 2 * nhb + h)),
                pl.BlockSpec((1, hpb, 4, nc, CHUNK), lambda i, h: (i, h, 0, 0, 0))]
    args = [qkv, qkv, qkv, grow]
    if has_init:
        c0, n0, m0 = init
        in_specs += [pl.BlockSpec((1, 1, 2, hpb, HEAD_DIM, HEAD_DIM), lambda i, h: (i, 0, 0, h, 0, 0)),
                     pl.BlockSpec((1, hpb, 2, HEAD_DIM), lambda i, h: (i, h, 0, 0)),
                     pl.BlockSpec((1, hpb, 2, 1), lambda i, h: (i, h, 0, 0))]
        args += [c0, n0, m0]
    out_shape = [jax.ShapeDtypeStruct((b, s, D_MODEL), F32)]
    out_specs = [pl.BlockSpec((1, s, width), lambda i, h: (i, 0, h))]
    if emit_state:
        out_shape += [jax.ShapeDtypeStruct((b, 1, 2, N_HEADS, HEAD_DIM, HEAD_DIM), F32),
                      jax.ShapeDtypeStruct((b, N_HEADS, 2, HEAD_DIM), F32),
                      jax.ShapeDtypeStruct((b, N_HEADS, 2, 128), F32)]
        out_specs += [pl.BlockSpec((1, 1, 2, hpb, HEAD_DIM, HEAD_DIM), lambda i, h: (i, 0, 0, h, 0, 0)),
                      pl.BlockSpec((1, hpb, 2, HEAD_DIM), lambda i, h: (i, h, 0, 0)),
                      pl.BlockSpec((1, hpb, 2, 128), lambda i, h: (i, h, 0, 0))]
    return pl.pallas_call(
        functools.partial(_mlstm_kernel, seq=s, heads=hpb, has_init=has_init, emit_state=emit_state),
        grid=(b, nhb),
        in_specs=in_specs,
        out_specs=tuple(out_specs),
        out_shape=tuple(out_shape),
        scratch_shapes=[pltpu.VMEM((2 * hpb, HEAD_DIM, HEAD_DIM), F32), pltpu.VMEM((2 * hpb, 1, HEAD_DIM), F32),
                        pltpu.VMEM((2 * hpb, N_ROW_KINDS, nc, CHUNK), F32)],
        compiler_params=_params(("arbitrary", "arbitrary")),
        name=f"mlstm_{s}",
    )(*args)


def _hyena_kernel(x1_ref, x2_ref, v_ref, w1_ref, w2_ref, wv_ref, b1_ref, b2_ref, bv_ref, skip_ref,
                  g0_ref, g1_ref, fwd_ref, inv_ref, o_ref, zf_s, yf_s, *, seq, p, shift, kk0):
    nb = seq // p
    ng = g0_ref.shape[0]
    bb = x1_ref.shape[0]
    lanes = o_ref.shape[-1]
    sub = lax.broadcasted_iota(jnp.int32, (8, 1), 0)

    def conv_rows(src_ref, bi, r, w_ref, b_ref):
        rr = r * p
        up = src_ref[bi, rr - 8:rr, :][7:8, :] if rr > 0 else jnp.zeros((1, lanes), F32)
        dn = src_ref[bi, rr + p:rr + p + 8, :][0:1, :] if rr + p < seq else jnp.zeros((1, lanes), F32)
        cur = src_ref[bi, rr:rr + p, :]
        prev = pltpu.roll(cur, 1, 0)
        prev = jnp.concatenate([jnp.where(sub == 0, up, prev[0:8]), prev[8:]], axis=0)
        nxt = pltpu.roll(cur, p - 1, 0)
        nxt = jnp.concatenate([nxt[:p - 8], jnp.where(sub == 7, dn, nxt[p - 8:])], axis=0)
        return b_ref[...] + prev * w_ref[0:1, :] + cur * w_ref[1:2, :] + nxt * w_ref[2:3, :]

    fwd = fwd_ref[...]
    inv = inv_ref[...]
    terms = [[(j, i - j + shift - kk0) for j in range(nb) if 0 <= i - j + shift - kk0 < ng] for i in range(nb)]
    batch = range(bb)
    orders = ((g0_ref, x1_ref, w1_ref, b1_ref), (g1_ref, x2_ref, w2_ref, b2_ref))

    def forward_dft(bi, order, j, z):
        zf_s[bi, order, j] = _dot(fwd, z.astype(BF16))

    def spectral_mac(bi, order, i):
        g_ref = orders[order][0]
        for r in range(p // SPEC_ROWS):
            rr = r * SPEC_ROWS
            re = jnp.zeros((SPEC_ROWS, lanes), F32)
            im = jnp.zeros((SPEC_ROWS, lanes), F32)
            for j, kk in terms[i]:
                zre = zf_s[bi, order, j, rr:rr + SPEC_ROWS, :]
                zim = zf_s[bi, order, j, p + rr:p + rr + SPEC_ROWS, :]
                gre = g_ref[kk, rr:rr + SPEC_ROWS, :]
                gim = g_ref[kk, p + rr:p + rr + SPEC_ROWS, :]
                re = re + (zre * gre - zim * gim)
                im = im + (zre * gim + zim * gre)
            if r == 0:
                dc = jnp.zeros((1, lanes), F32)
                ny = jnp.zeros((1, lanes), F32)
                for j, kk in terms[i]:
                    dc = dc + zf_s[bi, order, j, 0:1, :] * g_ref[kk, 0:1, :]
                    ny = ny + zf_s[bi, order, j, p:p + 1, :] * g_ref[kk, p:p + 1, :]
                first = lax.broadcasted_iota(jnp.int32, (SPEC_ROWS, 1), 0) == 0
                re = jnp.where(first, dc, re)
                im = jnp.where(first, ny, im)
            yf_s[bi, i, rr:rr + SPEC_ROWS, :] = re
            yf_s[bi, i, p + rr:p + rr + SPEC_ROWS, :] = im

    def finish_block(bi, order, i, y):
        _, x_ref, w_ref, b_ref = orders[order]
        rows = slice(i * p, (i + 1) * p)
        z = conv_rows(x_ref, bi, i, w_ref, b_ref) * (y + skip_ref[order:order + 1, :] * o_ref[bi, rows, :])
        o_ref[bi, rows, :] = z
        if order + 1 < len(orders):
            forward_dft(bi, order + 1, i, z)

    for j in range(nb):
        for bi in batch:
            z = conv_rows(v_ref, bi, j, wv_ref, bv_ref)
            o_ref[bi, j * p:(j + 1) * p, :] = z
            forward_dft(bi, 0, j, z)
    for order in range(len(orders)):
        pending = {}
        for i in range(nb):
            for bi in batch:
                spectral_mac(bi, order, i)
            for bi in batch:
                pending[bi, i] = _dot(inv, yf_s[bi, i].astype(BF16))
            if i >= 1:
                for bi in batch:
                    finish_block(bi, order, i - 1, pending.pop((bi, i - 1)))
        for bi in batch:
            finish_block(bi, order, nb - 1, pending.pop((bi, nb - 1)))


def _hyena(uh, conv_w, conv_b, skip, spectra, fwd, inv, p, bb):
    b, s, _ = uh.shape
    dc = HYENA_LANES
    nct = D_MODEL // dc
    nb = s // p
    shift, kk0, ng = _spectra_layout(s, p)
    conv_b = conv_b.reshape(1, -1)

    def part(k):
        return pl.BlockSpec((bb, s, dc), lambda c, i, k=k: (i, 0, k * nct + c))

    def wpart(k, rows):
        return pl.BlockSpec((rows, dc), lambda c, i, k=k: (0, k * nct + c))

    def gpart(order):
        return pl.BlockSpec((ng, 2 * p, dc), lambda c, i, order=order: (0, 0, order * nct + c),
                            pipeline_mode=pl.Buffered(1))

    in_specs = [part(0), part(1), part(2), wpart(0, 3), wpart(1, 3), wpart(2, 3),
                wpart(0, 1), wpart(1, 1), wpart(2, 1),
                pl.BlockSpec((2, dc), lambda c, i: (0, c)),
                gpart(0), gpart(1), _const_spec(fwd.shape), _const_spec(inv.shape)]
    return pl.pallas_call(
        functools.partial(_hyena_kernel, seq=s, p=p, shift=shift, kk0=kk0),
        grid=(nct, b // bb),
        in_specs=in_specs,
        out_specs=pl.BlockSpec((bb, s, dc), lambda c, i: (i, 0, c)),
        out_shape=jax.ShapeDtypeStruct((b, s, D_MODEL), F32),
        scratch_shapes=[pltpu.VMEM((bb, 2, nb, 2 * p, dc), F32), pltpu.VMEM((bb, nb, 2 * p, dc), F32)],
        compiler_params=_params(("arbitrary", "arbitrary")),
        name=f"hyena_{s}",
    )(uh, uh, uh, conv_w, conv_w, conv_w, conv_b, conv_b, conv_b, skip, spectra, spectra, fwd, inv)


def _tail_kernel(*refs, has_pos):
    if has_pos:
        x_ref, pos_ref = refs[:2]
        refs = refs[2:]
    else:
        x_ref = refs[0]
        refs = refs[1:]
    (h_ref, o_ref, gm_ref, gh_ref, yh_ref, modb_ref, modc_ref, mnw_ref, n2w_ref, fnw_ref,
     wbm_ref, wbh_ref, wout_ref, w1_ref, b1_ref, w2_ref, b2_ref, y_ref) = refs
    x = x_ref[0]
    if has_pos:
        x = x + pos_ref[...]
    g1 = modb_ref[0, :, 0:D_MODEL]
    sh2 = modb_ref[0, :, D_MODEL:2 * D_MODEL]
    sc2 = modc_ref[0, :, 0:D_MODEL]
    g2 = modc_ref[0, :, D_MODEL:2 * D_MODEL]
    h = h_ref[0]
    heads = [_rms(h[:, hd * HEAD_DIM:(hd + 1) * HEAD_DIM]) for hd in range(N_HEADS)]
    hm = jnp.concatenate(heads, axis=-1) * mnw_ref[...] * _sigmoid(o_ref[0].astype(F32))
    merged = (_sigmoid(gm_ref[0].astype(F32)) * _dot(hm.astype(BF16), wbm_ref[...])
              + _sigmoid(gh_ref[0].astype(F32)) * _dot(yh_ref[0].astype(BF16), wbh_ref[...]))
    x1 = x + g1 * _dot(merged.astype(BF16), wout_ref[...])
    hn2 = ((_rms(x1) * n2w_ref[...]) * (1.0 + sc2) + sh2).astype(BF16)
    ff = b2_ref[...]
    for kc in range(D_FF // D_MODEL):
        sl = slice(kc * D_MODEL, (kc + 1) * D_MODEL)
        a = jnp.maximum(_dot(hn2, w1_ref[:, sl]) + b1_ref[:, sl], 0.0)
        ff = ff + _dot((a * a).astype(BF16), w2_ref[sl, :])
    x2 = x1 + g2 * ff
    y_ref[0] = _rms(x2) * fnw_ref[...]


def _tail(x, pos, h, o, gmh, yh, mod3, mod_row, mnw, n2w, fnw, wbm, wbh, wout, w1, b1, w2, b2):
    b, s, _ = x.shape
    tm = min(TAIL_TOKEN_TILE, s)
    has_pos = pos is not None
    tok = pl.BlockSpec((1, tm, D_MODEL), lambda i, j: (i, j, 0))
    in_specs = [tok]
    args = [x]
    if has_pos:
        in_specs.append(pl.BlockSpec((tm, D_MODEL), lambda i, j: (j, 0)))
        args.append(pos)
    in_specs += [tok, tok, tok, pl.BlockSpec((1, tm, D_MODEL), lambda i, j: (i, j, 1)), tok,
                 pl.BlockSpec((1, 1, 2 * D_MODEL), lambda i, j: (mod_row(i), 0, 1)),
                 pl.BlockSpec((1, 1, 2 * D_MODEL), lambda i, j: (mod_row(i), 0, 2))]
    args += [h, o, gmh, gmh, yh, mod3, mod3]
    consts = [mnw, n2w, fnw, wbm, wbh, wout, w1, b1, w2, b2]
    in_specs += [_const_spec(a.shape) for a in consts]
    args += consts
    return pl.pallas_call(
        functools.partial(_tail_kernel, has_pos=has_pos),
        grid=(b, s // tm),
        in_specs=in_specs,
        out_specs=tok,
        out_shape=jax.ShapeDtypeStruct((b, s, D_MODEL), F32),
        compiler_params=_params(("arbitrary", "arbitrary")),
        name="tail_pos" if has_pos else "tail",
    )(*args)


def _gate_rows(gates_t):
    b, _, s = gates_t.shape
    return gates_t.reshape(b, N_HEADS, 4, s // CHUNK, CHUNK)


def kernel(x_prompt, x_sample, state_mlstm_C, state_mlstm_n, state_mlstm_m, c, c_ctx, w_ada, b_ada, norm1_w,
           w_in, b_in, hy_conv_w, hy_conv_b, filt_w1, filt_b1, filt_freq1, filt_w2, filt_b2, filt_freq2,
           filt_w3, hy_skip, mlstm_norm_w, w_br_m, w_br_h, w_out, norm2_w, w_mlp1, b_mlp1, w_mlp2, b_mlp2,
           final_norm_w):
    depth = w_ada.shape[0]
    assert depth == 1, "single-layer configuration"
    l = 0
    dec_b, dec_s, _ = x_sample.shape
    ctx_s = x_prompt.shape[1]

    n_rows = -(-(1 + dec_b) // 8) * 8
    cond = jnp.concatenate([c_ctx[None, :], c, jnp.zeros((n_rows - 1 - dec_b, D_MODEL), F32)], axis=0)
    mod = _modulation(cond, w_ada[l], b_ada[l])
    mod3 = mod.reshape(n_rows, 1, 6 * D_MODEL)

    wt32 = jnp.transpose(w_in[l])
    wt = wt32.astype(BF16)
    bias = b_in[l]
    ba = bias[:QKVO_COLS].reshape(1, -1)
    head_major = np.arange(N_GATES).reshape(4, N_HEADS).T.reshape(-1)
    wg32 = wt32[QKVO_COLS:QKVO_COLS + N_GATES][head_major]
    wg_hi = wg32.astype(BF16)
    wg = jnp.concatenate([wg_hi, (wg32 - wg_hi.astype(F32)).astype(BF16)], axis=0)
    bg = bias[QKVO_COLS:QKVO_COLS + N_GATES][head_major].reshape(-1, 1)
    br = bias[QKVO_COLS + N_GATES:].reshape(1, -1)
    n1w = norm1_w[l].reshape(1, -1)
    tail_w = (mlstm_norm_w[l].reshape(1, -1), norm2_w[l].reshape(1, -1), final_norm_w.reshape(1, -1),
              w_br_m[l].astype(BF16), w_br_h[l].astype(BF16), w_out[l].astype(BF16),
              w_mlp1[l].astype(BF16), b_mlp1[l].reshape(1, -1), w_mlp2[l].astype(BF16), b_mlp2[l].reshape(1, -1))
    filt = (filt_w1[l], filt_b1[l], filt_freq1[l], filt_w2[l], filt_b2[l], filt_freq2[l], filt_w3[l])

    pos = _pos_table(dec_s)

    def run(x, pos_tab, mod_row, init, emit_state, p, bb):
        s = x.shape[1]
        fwd32, inv32, sign = (jnp.asarray(a) for a in _dft_mats(p))
        spectra = _filter_spectra(s, p, *filt, fwd32, sign)
        qkv, o, gates, uh, gmh = _in_projection(x, pos_tab, mod3, mod_row, n1w, wt, ba, wg, bg, br)
        ml = _mlstm(qkv, _gate_rows(gates), init, emit_state)
        yh = _hyena(uh, hy_conv_w[l], hy_conv_b[l], hy_skip[l], spectra,
                    fwd32.astype(BF16), inv32.astype(BF16), p, bb)
        y = _tail(x, pos_tab, ml[0], o, gmh, yh, mod3, mod_row, *tail_w)
        return y, ml[1:]

    y_prompt, (st_c, st_n, st_m) = run(x_prompt, None, lambda i: 0, None, True, ctx_s, 4)
    init = (state_mlstm_C,
            jnp.transpose(state_mlstm_n[:, l], (0, 2, 1, 3)),
            jnp.transpose(state_mlstm_m[:, l], (0, 2, 1))[..., None])
    y_sample, _ = run(x_sample, pos, lambda i: i + 1, init, False, dec_s // 4, 1)

    new_state_n = jnp.transpose(st_n, (0, 2, 1, 3))[:, None]
    new_state_m = jnp.transpose(st_m[..., 0], (0, 2, 1))[:, None]
    return (y_prompt, y_sample, st_c, new_state_n, new_state_m)
```

```python
import functools
import math

import numpy as np
import jax
import jax.numpy as jnp
from jax import lax
from jax.experimental import pallas as pl
from jax.experimental.pallas import tpu as pltpu

F32 = jnp.float32
BF16 = jnp.bfloat16
HIGHEST = lax.Precision.HIGHEST

D_MODEL = 1024
N_HEADS = 4
HEAD_DIM = 256
D_FF = 4 * D_MODEL
GRID_W = 64
FILT_BANDS = 16
FILT_WIDTH = 64
HYENA_MIN_DECAY = math.log(1e-2) / 1.5
HYENA_MAX_DECAY = math.log(1e-2) / 0.3
CHUNK = 128
RMS_EPS = 1e-6
N_GATES = 4 * N_HEADS
QKVO_COLS = 4 * D_MODEL
REST_COLS = 5 * D_MODEL

V7X_VMEM_BYTES = 64 * 1024 * 1024
VMEM_LIMIT = V7X_VMEM_BYTES - 8 * 1024 * 1024

TOKEN_TILE = 512
TAIL_TOKEN_TILE = 512
HYENA_LANES = 256
FILTER_LANES = 256
SPEC_ROWS = 32
MLSTM_HEADS_PER_STEP = 2


def _params(sem):
    return pltpu.CompilerParams(dimension_semantics=sem, vmem_limit_bytes=VMEM_LIMIT)


def _const_spec(shape):
    nd = len(shape)
    return pl.BlockSpec(shape, lambda *_: (0,) * nd, pipeline_mode=pl.Buffered(1))


def _sigmoid(x):
    return 1.0 / (1.0 + jnp.exp(-x))


def _rms(x):
    return x * lax.rsqrt(jnp.mean(x * x, axis=-1, keepdims=True) + RMS_EPS)


def _dot(a, b):
    return jnp.dot(a, b, preferred_element_type=F32)


def _mod_kernel(c_ref, w_ref, b_ref, o_ref):
    c = c_ref[...]
    s = c * _sigmoid(c)
    rows = s.shape[0]
    s_hi = s.astype(BF16)
    s_lo = (s - s_hi.astype(F32)).astype(BF16)
    w = w_ref[...]
    w_hi = w.astype(BF16)
    w_lo = (w - w_hi.astype(F32)).astype(BF16)
    p = _dot(jnp.concatenate([s_hi, s_lo], axis=0), w_hi)
    o_ref[...] = p[0:rows] + (p[rows:2 * rows] + _dot(s_hi, w_lo)) + b_ref[...]


def _modulation(cond, w_ada, b_ada):
    rows = cond.shape[0]
    n = w_ada.shape[1]
    tn = 1536
    return pl.pallas_call(
        _mod_kernel,
        grid=(n // tn,),
        in_specs=[pl.BlockSpec((rows, D_MODEL), lambda j: (0, 0)),
                  pl.BlockSpec((D_MODEL, tn), lambda j: (0, j)),
                  pl.BlockSpec((1, tn), lambda j: (0, j))],
        out_specs=pl.BlockSpec((rows, tn), lambda j: (0, j)),
        out_shape=jax.ShapeDtypeStruct((rows, n), F32),
        compiler_params=_params(("arbitrary",)),
        name="modulation",
    )(cond, w_ada, b_ada.reshape(1, n))


def _pos_kernel(o_ref, *, rows):
    quarter = D_MODEL // 4
    half = D_MODEL // 2
    k = lax.broadcasted_iota(jnp.int32, (1, quarter), 1).astype(F32)
    omega = jnp.exp(k * (-math.log(10000.0) / quarter))

    def axis_embed(n):
        p = lax.broadcasted_iota(jnp.int32, (n, 1), 0).astype(F32)
        a = p * omega
        return jnp.concatenate([jnp.sin(a), jnp.cos(a)], axis=-1)

    er = axis_embed(rows)
    ec = axis_embed(GRID_W)
    o_ref[:, :, 0:half] = jnp.broadcast_to(er[:, None, :], (rows, GRID_W, half))
    o_ref[:, :, half:D_MODEL] = jnp.broadcast_to(ec[None, :, :], (rows, GRID_W, half))


def _pos_table(n_tokens):
    rows = n_tokens // GRID_W
    out = pl.pallas_call(
        functools.partial(_pos_kernel, rows=rows),
        out_shape=jax.ShapeDtypeStruct((rows, GRID_W, D_MODEL), F32),
        compiler_params=pltpu.CompilerParams(vmem_limit_bytes=VMEM_LIMIT),
        name="pos_table",
    )()
    return out.reshape(n_tokens, D_MODEL)


def _dft_mats(p):
    n = 2 * p
    idx = np.arange(p, dtype=np.float64)
    ang = 2.0 * np.pi * np.outer(idx, idx) / n
    alt = np.where(np.arange(p) % 2 == 0, 1.0, -1.0)
    fwd = np.zeros((n, p))
    fwd[:p] = np.cos(ang)
    fwd[p] = alt
    fwd[p + 1:] = -np.sin(ang[1:])
    inv = np.zeros((p, n))
    inv[:, :p] = 2.0 * np.cos(ang) / n
    inv[:, 0] = 1.0 / n
    inv[:, p] = alt / n
    inv[:, p + 1:] = -2.0 * np.sin(ang[:, 1:]) / n
    sign = np.concatenate([alt, alt])
    sign[p] = 1.0
    return fwd.astype(np.float32), inv.astype(np.float32), sign.astype(np.float32).reshape(n, 1)


def _filter_kernel(w1t_ref, w1c_ref, w1s_ref, b1_ref, fr1_ref, w2_ref, b2_ref, fr2_ref, w3_ref,
                   fhi_ref, flo_ref, sign_ref, g_ref, feat_ref, *, seq, p, kk0):
    nb = seq // p
    tn = w3_ref.shape[1]
    step = pl.program_id(0)
    idx = lax.broadcasted_iota(jnp.int32, (seq, 1), 0).astype(F32)

    @pl.when(step == 0)
    def _():
        t = idx / float(seq - 1)
        bands = (lax.broadcasted_iota(jnp.int32, (1, FILT_BANDS), 1) + 1).astype(F32)
        ang = ((2.0 * math.pi / seq) * idx) * bands
        pre = (t * w1t_ref[...]
               + jnp.dot(jnp.cos(ang), w1c_ref[...], precision=HIGHEST, preferred_element_type=F32)
               + jnp.dot(jnp.sin(ang), w1s_ref[...], precision=HIGHEST, preferred_element_type=F32)
               + b1_ref[...])
        h1 = jnp.sin(fr1_ref[...] * pre)
        h2 = jnp.sin(fr2_ref[...] * (jnp.dot(h1, w2_ref[...], precision=HIGHEST, preferred_element_type=F32)
                                     + b2_ref[...]))
        feat_ref[...] = h2

    feat = feat_ref[...]
    w3 = w3_ref[...]
    feat_hi = feat.astype(BF16)
    feat_lo = (feat - feat_hi.astype(F32)).astype(BF16)
    w3_hi = w3.astype(BF16)
    w3_lo = (w3 - w3_hi.astype(F32)).astype(BF16)
    hh = _dot(feat_hi, w3_hi) + (_dot(feat_lo, w3_hi) + _dot(feat_hi, w3_lo))
    d0 = lax.rem(step * tn, D_MODEL)
    d = (d0 + lax.broadcasted_iota(jnp.int32, (1, tn), 1)).astype(F32)
    delta = jnp.abs(HYENA_MIN_DECAY + (HYENA_MAX_DECAY - HYENA_MIN_DECAY) * d / float(D_MODEL - 1))
    centre = seq // 2
    dist = jnp.abs(idx - float(centre)) / float(centre)
    hh = hh * jnp.exp(-dist * delta)
    hh = hh / (jnp.sum(jnp.abs(hh), axis=0, keepdims=True) + 1e-6)

    fhi = fhi_ref[...]
    flo = flo_ref[...]
    off = (seq // 2) % p
    if off:
        zeros_hi = jnp.zeros((p - off, tn), F32)
        zeros_lo = jnp.zeros((off, tn), F32)
        blocks = [jnp.concatenate([zeros_hi, hh[0:off]], axis=0)]
        blocks += [hh[off + (k - 1) * p:off + k * p] for k in range(1, nb)]
        blocks += [jnp.concatenate([hh[off + (nb - 1) * p:seq], zeros_lo], axis=0)]
    else:
        blocks = [hh[k * p:(k + 1) * p] for k in range(nb)]
    spec = []
    for hb in blocks:
        hi = hb.astype(BF16)
        lo = (hb - hi.astype(F32)).astype(BF16)
        spec.append(_dot(fhi, hi) + _dot(fhi, lo) + _dot(flo, hi))
    sign = sign_ref[...]
    for k in range(g_ref.shape[0]):
        kk = kk0 + k
        if kk == 0:
            g = spec[0]
        elif kk == len(spec):
            g = sign * spec[kk - 1]
        else:
            g = spec[kk] + sign * spec[kk - 1]
        g_ref[k] = g


def _spectra_layout(seq, p):
    nb = seq // p
    centre = seq // 2
    nblk = nb + (1 if centre % p else 0)
    shift = centre // p + (1 if centre % p else 0)
    kk_lo = max(shift - (nb - 1), 0)
    kk_hi = min(shift + (nb - 1), nblk)
    return shift, kk_lo, kk_hi - kk_lo + 1


def _filter_spectra(seq, p, w1, b1, fr1, w2, b2, fr2, w3, fwd32, sign):
    tn = FILTER_LANES
    ncols = w3.shape[1]
    _, kk0, ng = _spectra_layout(seq, p)
    fhi = fwd32.astype(BF16)
    flo = (fwd32 - fhi.astype(F32)).astype(BF16)
    args = (w1[0:1], w1[1:1 + FILT_BANDS], w1[1 + FILT_BANDS:], b1.reshape(1, -1), fr1.reshape(1, -1),
            w2, b2.reshape(1, -1), fr2.reshape(1, -1))
    in_specs = [_const_spec(a.shape) for a in args]
    in_specs += [pl.BlockSpec((FILT_WIDTH, tn), lambda j: (0, j)),
                 _const_spec(fhi.shape), _const_spec(flo.shape), _const_spec(sign.shape)]
    return pl.pallas_call(
        functools.partial(_filter_kernel, seq=seq, p=p, kk0=kk0),
        grid=(ncols // tn,),
        in_specs=in_specs,
        out_specs=pl.BlockSpec((ng, 2 * p, tn), lambda j: (0, 0, j)),
        out_shape=jax.ShapeDtypeStruct((ng, 2 * p, ncols), F32),
        scratch_shapes=[pltpu.VMEM((seq, FILT_WIDTH), F32)],
        compiler_params=_params(("arbitrary",)),
        name=f"filter_spectra_{seq}",
    )(*args, w3, fhi, flo, sign)


def _inproj_kernel(*refs, has_pos):
    if has_pos:
        x_ref, pos_ref = refs[:2]
        refs = refs[2:]
    else:
        x_ref = refs[0]
        refs = refs[1:]
    (mod_ref, n1w_ref, wa_ref, ba_ref, wg_ref, bg_ref, wr_ref, br_ref,
     qkv_ref, o_ref, gates_ref, uh_ref, gmh_ref) = refs
    bb, tm, _ = x_ref.shape
    x = x_ref[...].reshape(bb * tm, D_MODEL)
    if has_pos:
        x = x + pos_ref[...]
    sh1 = mod_ref[0, :, 0:D_MODEL]
    sc1 = mod_ref[0, :, D_MODEL:2 * D_MODEL]
    hn = (_rms(x) * n1w_ref[...]) * (1.0 + sc1) + sh1
    hb = hn.astype(BF16)
    nt_dims = (((1,), (1,)), ((), ()))
    for c in range(4):
        sl = slice(c * D_MODEL, (c + 1) * D_MODEL)
        r = lax.dot_general(hb, wa_ref[sl, :], nt_dims, preferred_element_type=F32) + ba_ref[:, sl]
        if c < 3:
            qkv_ref[:, :, sl] = r.astype(BF16).reshape(bb, tm, D_MODEL)
        else:
            o_ref[...] = r.astype(BF16).reshape(bb, tm, D_MODEL)
    for c in range(5):
        sl = slice(c * D_MODEL, (c + 1) * D_MODEL)
        r = lax.dot_general(hb, wr_ref[sl, :], nt_dims, preferred_element_type=F32) + br_ref[:, sl]
        if c < 3:
            uh_ref[:, :, sl] = r.reshape(bb, tm, D_MODEL)
        else:
            gmh_ref[:, :, (c - 3) * D_MODEL:(c - 2) * D_MODEL] = r.astype(BF16).reshape(bb, tm, D_MODEL)
    p1 = lax.dot_general(wg_ref[...], hb, nt_dims, preferred_element_type=F32)
    gates = p1[0:N_GATES] + p1[N_GATES:2 * N_GATES] + bg_ref[...]
    for i in range(bb):
        gates_ref[i] = gates[:, i * tm:(i + 1) * tm]


def _in_projection(x, pos, mod3, mod_row, n1w, wt, ba, wg, bg, br):
    b, s, _ = x.shape
    tm = min(TOKEN_TILE, s)
    has_pos = pos is not None
    bb = 1 if has_pos else max(1, TOKEN_TILE // s)
    in_specs = [pl.BlockSpec((bb, tm, D_MODEL), lambda i, j: (i, j, 0))]
    args = [x]
    if has_pos:
        in_specs.append(pl.BlockSpec((tm, D_MODEL), lambda i, j: (j, 0)))
        args.append(pos)
    wa_spec = pl.BlockSpec((QKVO_COLS, D_MODEL), lambda i, j: (0, 0), pipeline_mode=pl.Buffered(1))
    wr_spec = pl.BlockSpec((pl.Element(REST_COLS), pl.Element(D_MODEL)), lambda i, j: (QKVO_COLS + N_GATES, 0),
                           pipeline_mode=pl.Buffered(1))
    in_specs += [pl.BlockSpec((1, 1, 2 * D_MODEL), lambda i, j: (mod_row(i), 0, 0)),
                 _const_spec(n1w.shape), wa_spec, _const_spec(ba.shape),
                 _const_spec(wg.shape), _const_spec(bg.shape), wr_spec, _const_spec(br.shape)]
    args += [mod3, n1w, wt, ba, wg, bg, wt, br]
    out_shape = (jax.ShapeDtypeStruct((b, s, 3 * D_MODEL), BF16),
                 jax.ShapeDtypeStruct((b, s, D_MODEL), BF16),
                 jax.ShapeDtypeStruct((b, N_GATES, s), F32),
                 jax.ShapeDtypeStruct((b, s, 3 * D_MODEL), F32),
                 jax.ShapeDtypeStruct((b, s, 2 * D_MODEL), BF16))
    out_specs = tuple(pl.BlockSpec((bb, N_GATES, tm), lambda i, j: (i, 0, j)) if k == 2 else
                      pl.BlockSpec((bb, tm, sh.shape[2]), lambda i, j: (i, j, 0)) for k, sh in enumerate(out_shape))
    return pl.pallas_call(
        functools.partial(_inproj_kernel, has_pos=has_pos),
        grid=(b // bb, s // tm),
        in_specs=in_specs,
        out_specs=out_specs,
        out_shape=out_shape,
        compiler_params=_params(("arbitrary", "arbitrary")),
        name="in_projection_pos" if has_pos else "in_projection",
    )(*args)


def _log_sigmoid(x):
    return jnp.minimum(x, 0.0) - jnp.log(1.0 + jnp.exp(-jnp.abs(x)))


def _split3(x):
    hi = x.astype(BF16).astype(F32)
    mid = (x - hi).astype(BF16).astype(F32)
    lo = (x - hi - mid).astype(BF16).astype(F32)
    return hi, mid, lo


ROW_E, ROW_B, ROW_M_INTER, ROW_W_STATE, ROW_DECAY, N_ROW_KINDS = 0, 3, 6, 7, 8, 9


def _mlstm_kernel(*refs, seq, heads, has_init, emit_state):
    q_ref, k_ref, v_ref, grow_ref = refs[:4]
    refs = refs[4:]
    if has_init:
        c0_ref, n0_ref, m0_ref = refs[:3]
        refs = refs[3:]
    h_ref = refs[0]
    refs = refs[1:]
    if emit_state:
        c_out, n_out, m_out = refs[:3]
        refs = refs[3:]
    c_s, n_s, rows_s = refs
    t = CHUNK
    nc = seq // t
    scale = HEAD_DIM ** -0.5

    r_io = lax.broadcasted_iota(jnp.int32, (t, t), 0)
    c_io = lax.broadcasted_iota(jnp.int32, (t, t), 1)
    chunk_id = lax.broadcasted_iota(jnp.int32, (nc, 1), 0)

    chains = range(2 * heads)

    m_final = []
    for ch in chains:
        hh, dr = divmod(ch, 2)
        if has_init:
            c_s[ch] = c0_ref[0, 0, dr, hh]
            n_s[ch] = n0_ref[0, hh, dr:dr + 1, :]
            m = m0_ref[0, hh, dr:dr + 1, :]
        else:
            c_s[ch] = jnp.zeros((HEAD_DIM, HEAD_DIM), F32)
            n_s[ch] = jnp.zeros((1, HEAD_DIM), F32)
            m = jnp.zeros((1, 1), F32)
        li = grow_ref[0, hh, 2 * dr]
        lf = _log_sigmoid(grow_ref[0, hh, 2 * dr + 1])
        tri = ((r_io <= c_io) if dr == 0 else (r_io >= c_io)).astype(BF16)
        b = sum(_dot(part.astype(BF16), tri) for part in _split3(lf))
        b_last = jnp.sum(lf, axis=1, keepdims=True)
        a = b_last - b + li
        a_max = jnp.max(a, axis=1, keepdims=True)
        m_before = jnp.zeros((nc, 1), F32)
        m_after = jnp.zeros((nc, 1), F32)
        for c in (range(nc) if dr == 0 else reversed(range(nc))):
            m_new = jnp.maximum(b_last[c:c + 1, :] + m, a_max[c:c + 1, :])
            m_before = jnp.where(chunk_id == c, m, m_before)
            m_after = jnp.where(chunk_id == c, m_new, m_after)
            m = m_new
        m_final.append(m)
        kinds = (*_split3(li - b), *_split3(b), b + m_before, jnp.exp(a - m_after),
                 jnp.broadcast_to(jnp.exp(b_last + m_before - m_after), (nc, t)))
        for idx, x in enumerate(kinds):
            rows_s[ch, idx] = x

    sub8 = lax.broadcasted_iota(jnp.int32, (8, t), 0)
    sub16 = lax.broadcasted_iota(jnp.int32, (16, 1), 0)

    def pick3(base, ch, c, lo, ones_lo, ones_hi):
        out = jnp.where((sub8 >= ones_lo) & (sub8 < ones_hi), 1.0, 0.0)
        for i in range(3):
            out = jnp.where(sub8 == lo + i, rows_s[ch, base + i, pl.ds(c, 1), :], out)
        return out

    def hi_lo_rows(x):
        hi = x.astype(BF16).astype(F32)
        lo = x - hi
        return jnp.where(sub16 == 0, hi, jnp.where(sub16 == 1, lo, 0.0)).astype(BF16)

    nt_dims = (((1,), (1,)), ((), ()))
    tn_dims = (((0,), (0,)), ((), ()))

    def both(i, accumulate):
        dirs = chains
        cs = [i if ch % 2 == 0 else nc - 1 - i for ch in chains]
        r0 = [c * t if isinstance(c, int) else pl.multiple_of(c * t, t) for c in cs]
        cols = [slice((ch // 2) * HEAD_DIM, (ch // 2 + 1) * HEAD_DIM) for ch in chains]
        q = [q_ref[0, pl.ds(r0[d], t), cols[d]] for d in dirs]
        k = [k_ref[0, pl.ds(r0[d], t), cols[d]] for d in dirs]
        v = [v_ref[0, pl.ds(r0[d], t), cols[d]] for d in dirs]
        row = lambda kind, d: rows_s[d, kind, pl.ds(cs[d], 1), :]
        d_t = [lax.dot_general(pick3(ROW_E, d, cs[d], 0, 3, 6), pick3(ROW_B, d, cs[d], 3, 0, 3), tn_dims,
                               preferred_element_type=F32) for d in dirs]
        d_t = [jnp.where((r_io <= c_io) if d % 2 == 0 else (r_io >= c_io), d_t[d], -jnp.inf) for d in dirs]
        m_inter = [row(ROW_M_INTER, d) for d in dirs]
        m_comb = [jnp.maximum(m_inter[d], jnp.max(d_t[d], axis=0, keepdims=True)) for d in dirs]
        pw = [jnp.exp(d_t[d] - m_comb[d]) for d in dirs]
        w_inter = [jnp.exp(m_inter[d] - m_comb[d]) for d in dirs]
        c_prev = [c_s[d] for d in dirs]
        n_prev = [n_s[d] for d in dirs]
        kq = [lax.dot_general(jnp.concatenate([k[d], hi_lo_rows(n_prev[d])], axis=0), q[d], nt_dims,
                              preferred_element_type=F32) for d in dirs]
        qc = [_dot(q[d], c_prev[d].astype(BF16)) for d in dirs]
        w_state = [row(ROW_W_STATE, d) for d in dirs]
        kw = [(k[d].T.astype(F32) * w_state[d]).astype(BF16) for d in dirs]
        sp = [kq[d][0:t] * pw[d] for d in dirs]
        den = [scale * (w_inter[d] * (kq[d][t:t + 1] + kq[d][t + 1:t + 2]) + jnp.sum(sp[d], axis=0, keepdims=True))
               for d in dirs]
        inv_den = [scale / jnp.maximum(jnp.abs(den[d]), jnp.exp(-m_comb[d])) for d in dirs]
        inter_col = [jnp.transpose(jnp.where(sub8 == 0, w_inter[d] * inv_den[d], 0.0))[:, 0:1] for d in dirs]
        sv = [lax.dot_general((sp[d] * inv_den[d]).astype(BF16), v[d], tn_dims, preferred_element_type=F32)
              for d in dirs]
        kv = [_dot(kw[d], v[d]) for d in dirs]
        nk = [_dot(hi_lo_rows(w_state[d]), k[d]) for d in dirs]
        decay = [row(ROW_DECAY, d)[:, 0:1] for d in dirs]
        for d in dirs:
            c_s[d] = decay[d] * c_prev[d] + kv[d]
            n_s[d] = decay[d] * n_prev[d] + nk[d][0:1] + nk[d][1:2]
        for d in dirs:
            h = sv[d] + inter_col[d] * qc[d]
            if accumulate:
                h_ref[0, pl.ds(r0[d], t), cols[d]] += h
            else:
                h_ref[0, pl.ds(r0[d], t), cols[d]] = h

    half = nc // 2
    if nc <= 16:
        for i in range(nc):
            both(i, i >= half)
    else:
        lax.fori_loop(0, half, lambda i, carry: (both(i, False), carry)[1], 0)
        lax.fori_loop(half, nc, lambda i, carry: (both(i, True), carry)[1], 0)

    if emit_state:
        for ch in chains:
            hh, dr = divmod(ch, 2)
            c_out[0, 0, dr, hh] = c_s[ch]
            n_out[0, hh, dr:dr + 1, :] = n_s[ch]
            m_out[0, hh, dr:dr + 1, :] = jnp.broadcast_to(m_final[ch], (1, 128))


def _mlstm(qkv, grow, init, emit_state):
    b, s, _ = qkv.shape
    nc = s // CHUNK
    has_init = init is not None
    hpb = N_HEADS if nc <= 2 else MLSTM_HEADS_PER_STEP
    nhb = N_HEADS // hpb
    width = hpb * HEAD_DIM
    in_specs = [pl.BlockSpec((1, s, width), lambda i, h: (i, 0, h)),
                pl.BlockSpec((1, s, width), lambda i, h: (i, 0, nhb + h)),
                pl.BlockSpec((1, s, width), lambda i, h: (i, 0, 2 * nhb + h)),
                pl.BlockSpec((1, hpb, 4, nc, CHUNK), lambda i, h: (i, h, 0, 0, 0))]
    args = [qkv, qkv, qkv, grow]
    if has_init:
        c0, n0, m0 = init
        in_specs += [pl.BlockSpec((1, 1, 2, hpb, HEAD_DIM, HEAD_DIM), lambda i, h: (i, 0, 0, h, 0, 0)),
                     pl.BlockSpec((1, hpb, 2, HEAD_DIM), lambda i, h: (i, h, 0, 0)),
                     pl.BlockSpec((1, hpb, 2, 1), lambda i, h: (i, h, 0, 0))]
        args += [c0, n0, m0]
    out_shape = [jax.ShapeDtypeStruct((b, s, D_MODEL), F32)]
    out_specs = [pl.BlockSpec((1, s, width), lambda i, h: (i, 0, h))]
    if emit_state:
        out_shape += [jax.ShapeDtypeStruct((b, 1, 2, N_HEADS, HEAD_DIM, HEAD_DIM), F32),
                      jax.ShapeDtypeStruct((b, N_HEADS, 2, HEAD_DIM), F32),
                      jax.ShapeDtypeStruct((b, N_HEADS, 2, 128), F32)]
        out_specs += [pl.BlockSpec((1, 1, 2, hpb, HEAD_DIM, HEAD_DIM), lambda i, h: (i, 0, 0, h, 0, 0)),
                      pl.BlockSpec((1, hpb, 2, HEAD_DIM), lambda i, h: (i, h, 0, 0)),
                      pl.BlockSpec((1, hpb, 2, 128), lambda i, h: (i, h, 0, 0))]
    return pl.pallas_call(
        functools.partial(_mlstm_kernel, seq=s, heads=hpb, has_init=has_init, emit_state=emit_state),
        grid=(b, nhb),
        in_specs=in_specs,
        out_specs=tuple(out_specs),
        out_shape=tuple(out_shape),
        scratch_shapes=[pltpu.VMEM((2 * hpb, HEAD_DIM, HEAD_DIM), F32), pltpu.VMEM((2 * hpb, 1, HEAD_DIM), F32),
                        pltpu.VMEM((2 * hpb, N_ROW_KINDS, nc, CHUNK), F32)],
        compiler_params=_params(("arbitrary", "arbitrary")),
        name=f"mlstm_{s}",
    )(*args)


def _hyena_kernel(x1_ref, x2_ref, v_ref, w1_ref, w2_ref, wv_ref, b1_ref, b2_ref, bv_ref, skip_ref,
                  g0_ref, g1_ref, fwd_ref, inv_ref, o_ref, zf_s, yf_s, *, seq, p, shift, kk0):
    nb = seq // p
    ng = g0_ref.shape[0]
    bb = x1_ref.shape[0]
    lanes = o_ref.shape[-1]
    sub = lax.broadcasted_iota(jnp.int32, (8, 1), 0)

    def conv_rows(src_ref, bi, r, w_ref, b_ref):
        rr = r * p
        up = src_ref[bi, rr - 8:rr, :][7:8, :] if rr > 0 else jnp.zeros((1, lanes), F32)
        dn = src_ref[bi, rr + p:rr + p + 8, :][0:1, :] if rr + p < seq else jnp.zeros((1, lanes), F32)
        cur = src_ref[bi, rr:rr + p, :]
        prev = pltpu.roll(cur, 1, 0)
        prev = jnp.concatenate([jnp.where(sub == 0, up, prev[0:8]), prev[8:]], axis=0)
        nxt = pltpu.roll(cur, p - 1, 0)
        nxt = jnp.concatenate([nxt[:p - 8], jnp.where(sub == 7, dn, nxt[p - 8:])], axis=0)
        return b_ref[...] + prev * w_ref[0:1, :] + cur * w_ref[1:2, :] + nxt * w_ref[2:3, :]

    fwd = fwd_ref[...]
    inv = inv_ref[...]
    terms = [[(j, i - j + shift - kk0) for j in range(nb) if 0 <= i - j + shift - kk0 < ng] for i in range(nb)]
    batch = range(bb)
    orders = ((g0_ref, x1_ref, w1_ref, b1_ref), (g1_ref, x2_ref, w2_ref, b2_ref))

    def forward_dft(bi, order, j, z):
        zf_s[bi, order, j] = _dot(fwd, z.astype(BF16))

    def spectral_mac(bi, order, i):
        g_ref = orders[order][0]
        for r in range(p // SPEC_ROWS):
            rr = r * SPEC_ROWS
            re = jnp.zeros((SPEC_ROWS, lanes), F32)
            im = jnp.zeros((SPEC_ROWS, lanes), F32)
            for j, kk in terms[i]:
                zre = zf_s[bi, order, j, rr:rr + SPEC_ROWS, :]
                zim = zf_s[bi, order, j, p + rr:p + rr + SPEC_ROWS, :]
                gre = g_ref[kk, rr:rr + SPEC_ROWS, :]
                gim = g_ref[kk, p + rr:p + rr + SPEC_ROWS, :]
                re = re + (zre * gre - zim * gim)
                im = im + (zre * gim + zim * gre)
            if r == 0:
                dc = jnp.zeros((1, lanes), F32)
                ny = jnp.zeros((1, lanes), F32)
                for j, kk in terms[i]:
                    dc = dc + zf_s[bi, order, j, 0:1, :] * g_ref[kk, 0:1, :]
                    ny = ny + zf_s[bi, order, j, p:p + 1, :] * g_ref[kk, p:p + 1, :]
                first = lax.broadcasted_iota(jnp.int32, (SPEC_ROWS, 1), 0) == 0
                re = jnp.where(first, dc, re)
                im = jnp.where(first, ny, im)
            yf_s[bi, i, rr:rr + SPEC_ROWS, :] = re
            yf_s[bi, i, p + rr:p + rr + SPEC_ROWS, :] = im

    def finish_block(bi, order, i, y):
        _, x_ref, w_ref, b_ref = orders[order]
        rows = slice(i * p, (i + 1) * p)
        z = conv_rows(x_ref, bi, i, w_ref, b_ref) * (y + skip_ref[order:order + 1, :] * o_ref[bi, rows, :])
        o_ref[bi, rows, :] = z
        if order + 1 < len(orders):
            forward_dft(bi, order + 1, i, z)

    for j in range(nb):
        for bi in batch:
            z = conv_rows(v_ref, bi, j, wv_ref, bv_ref)
            o_ref[bi, j * p:(j + 1) * p, :] = z
            forward_dft(bi, 0, j, z)
    for order in range(len(orders)):
        pending = {}
        for i in range(nb):
            for bi in batch:
                spectral_mac(bi, order, i)
            for bi in batch:
                pending[bi, i] = _dot(inv, yf_s[bi, i].astype(BF16))
            if i >= 1:
                for bi in batch:
                    finish_block(bi, order, i - 1, pending.pop((bi, i - 1)))
        for bi in batch:
            finish_block(bi, order, nb - 1, pending.pop((bi, nb - 1)))


def _hyena(uh, conv_w, conv_b, skip, spectra, fwd, inv, p, bb):
    b, s, _ = uh.shape
    dc = HYENA_LANES
    nct = D_MODEL // dc
    nb = s // p
    shift, kk0, ng = _spectra_layout(s, p)
    conv_b = conv_b.reshape(1, -1)

    def part(k):
        return pl.BlockSpec((bb, s, dc), lambda c, i, k=k: (i, 0, k * nct + c))

    def wpart(k, rows):
        return pl.BlockSpec((rows, dc), lambda c, i, k=k: (0, k * nct + c))

    def gpart(order):
        return pl.BlockSpec((ng, 2 * p, dc), lambda c, i, order=order: (0, 0, order * nct + c),
                            pipeline_mode=pl.Buffered(1))

    in_specs = [part(0), part(1), part(2), wpart(0, 3), wpart(1, 3), wpart(2, 3),
                wpart(0, 1), wpart(1, 1), wpart(2, 1),
                pl.BlockSpec((2, dc), lambda c, i: (0, c)),
                gpart(0), gpart(1), _const_spec(fwd.shape), _const_spec(inv.shape)]
    return pl.pallas_call(
        functools.partial(_hyena_kernel, seq=s, p=p, shift=shift, kk0=kk0),
        grid=(nct, b // bb),
        in_specs=in_specs,
        out_specs=pl.BlockSpec((bb, s, dc), lambda c, i: (i, 0, c)),
        out_shape=jax.ShapeDtypeStruct((b, s, D_MODEL), F32),
        scratch_shapes=[pltpu.VMEM((bb, 2, nb, 2 * p, dc), F32), pltpu.VMEM((bb, nb, 2 * p, dc), F32)],
        compiler_params=_params(("arbitrary", "arbitrary")),
        name=f"hyena_{s}",
    )(uh, uh, uh, conv_w, conv_w, conv_w, conv_b, conv_b, conv_b, skip, spectra, spectra, fwd, inv)


def _tail_kernel(*refs, has_pos):
    if has_pos:
        x_ref, pos_ref = refs[:2]
        refs = refs[2:]
    else:
        x_ref = refs[0]
        refs = refs[1:]
    (h_ref, o_ref, gm_ref, gh_ref, yh_ref, modb_ref, modc_ref, mnw_ref, n2w_ref, fnw_ref,
     wbm_ref, wbh_ref, wout_ref, w1_ref, b1_ref, w2_ref, b2_ref, y_ref) = refs
    bb, tm, _ = x_ref.shape
    rows = bb * tm
    tile = lambda ref: ref[...].reshape(rows, D_MODEL)
    x = tile(x_ref)
    if has_pos:
        x = x + pos_ref[...]
    g1 = modb_ref[0, :, 0:D_MODEL]
    sh2 = modb_ref[0, :, D_MODEL:2 * D_MODEL]
    sc2 = modc_ref[0, :, 0:D_MODEL]
    g2 = modc_ref[0, :, D_MODEL:2 * D_MODEL]
    h = tile(h_ref)
    heads = [_rms(h[:, hd * HEAD_DIM:(hd + 1) * HEAD_DIM]) for hd in range(N_HEADS)]
    hm = jnp.concatenate(heads, axis=-1) * mnw_ref[...] * _sigmoid(tile(o_ref).astype(F32))
    merged = (_sigmoid(tile(gm_ref).astype(F32)) * _dot(hm.astype(BF16), wbm_ref[...])
              + _sigmoid(tile(gh_ref).astype(F32)) * _dot(tile(yh_ref).astype(BF16), wbh_ref[...]))
    x1 = x + g1 * _dot(merged.astype(BF16), wout_ref[...])
    hn2 = ((_rms(x1) * n2w_ref[...]) * (1.0 + sc2) + sh2).astype(BF16)
    ff = b2_ref[...]
    for kc in range(D_FF // D_MODEL):
        sl = slice(kc * D_MODEL, (kc + 1) * D_MODEL)
        a = jnp.maximum(_dot(hn2, w1_ref[:, sl]) + b1_ref[:, sl], 0.0)
        ff = ff + _dot((a * a).astype(BF16), w2_ref[sl, :])
    x2 = x1 + g2 * ff
    y_ref[...] = (_rms(x2) * fnw_ref[...]).reshape(bb, tm, D_MODEL)


def _tail(x, pos, h, o, gmh, yh, mod3, mod_row, mnw, n2w, fnw, wbm, wbh, wout, w1, b1, w2, b2):
    b, s, _ = x.shape
    tm = min(TAIL_TOKEN_TILE, s)
    has_pos = pos is not None
    bb = 1 if has_pos else max(1, TAIL_TOKEN_TILE // s)
    tok = pl.BlockSpec((bb, tm, D_MODEL), lambda i, j: (i, j, 0))
    in_specs = [tok]
    args = [x]
    if has_pos:
        in_specs.append(pl.BlockSpec((tm, D_MODEL), lambda i, j: (j, 0)))
        args.append(pos)
    in_specs += [tok, tok, tok, pl.BlockSpec((bb, tm, D_MODEL), lambda i, j: (i, j, 1)), tok,
                 pl.BlockSpec((1, 1, 2 * D_MODEL), lambda i, j: (mod_row(i), 0, 1)),
                 pl.BlockSpec((1, 1, 2 * D_MODEL), lambda i, j: (mod_row(i), 0, 2))]
    args += [h, o, gmh, gmh, yh, mod3, mod3]
    consts = [mnw, n2w, fnw, wbm, wbh, wout, w1, b1, w2, b2]
    in_specs += [_const_spec(a.shape) for a in consts]
    args += consts
    return pl.pallas_call(
        functools.partial(_tail_kernel, has_pos=has_pos),
        grid=(b // bb, s // tm),
        in_specs=in_specs,
        out_specs=tok,
        out_shape=jax.ShapeDtypeStruct((b, s, D_MODEL), F32),
        compiler_params=_params(("arbitrary", "arbitrary")),
        name="tail_pos" if has_pos else "tail",
    )(*args)


def _gate_rows(gates_t):
    b, _, s = gates_t.shape
    return gates_t.reshape(b, N_HEADS, 4, s // CHUNK, CHUNK)


def kernel(x_prompt, x_sample, state_mlstm_C, state_mlstm_n, state_mlstm_m, c, c_ctx, w_ada, b_ada, norm1_w,
           w_in, b_in, hy_conv_w, hy_conv_b, filt_w1, filt_b1, filt_freq1, filt_w2, filt_b2, filt_freq2,
           filt_w3, hy_skip, mlstm_norm_w, w_br_m, w_br_h, w_out, norm2_w, w_mlp1, b_mlp1, w_mlp2, b_mlp2,
           final_norm_w):
    depth = w_ada.shape[0]
    assert depth == 1, "single-layer configuration"
    l = 0
    dec_b, dec_s, _ = x_sample.shape
    ctx_s = x_prompt.shape[1]

    n_rows = -(-(1 + dec_b) // 8) * 8
    cond = jnp.concatenate([c_ctx[None, :], c, jnp.zeros((n_rows - 1 - dec_b, D_MODEL), F32)], axis=0)
    mod = _modulation(cond, w_ada[l], b_ada[l])
    mod3 = mod.reshape(n_rows, 1, 6 * D_MODEL)

    wt32 = jnp.transpose(w_in[l])
    wt = wt32.astype(BF16)
    bias = b_in[l]
    ba = bias[:QKVO_COLS].reshape(1, -1)
    head_major = np.arange(N_GATES).reshape(4, N_HEADS).T.reshape(-1)
    wg32 = wt32[QKVO_COLS:QKVO_COLS + N_GATES][head_major]
    wg_hi = wg32.astype(BF16)
    wg = jnp.concatenate([wg_hi, (wg32 - wg_hi.astype(F32)).astype(BF16)], axis=0)
    bg = bias[QKVO_COLS:QKVO_COLS + N_GATES][head_major].reshape(-1, 1)
    br = bias[QKVO_COLS + N_GATES:].reshape(1, -1)
    n1w = norm1_w[l].reshape(1, -1)
    tail_w = (mlstm_norm_w[l].reshape(1, -1), norm2_w[l].reshape(1, -1), final_norm_w.reshape(1, -1),
              w_br_m[l].astype(BF16), w_br_h[l].astype(BF16), w_out[l].astype(BF16),
              w_mlp1[l].astype(BF16), b_mlp1[l].reshape(1, -1), w_mlp2[l].astype(BF16), b_mlp2[l].reshape(1, -1))
    filt = (filt_w1[l], filt_b1[l], filt_freq1[l], filt_w2[l], filt_b2[l], filt_freq2[l], filt_w3[l])

    pos = _pos_table(dec_s)

    def run(x, pos_tab, mod_row, init, emit_state, p, bb):
        s = x.shape[1]
        fwd32, inv32, sign = (jnp.asarray(a) for a in _dft_mats(p))
        spectra = _filter_spectra(s, p, *filt, fwd32, sign)
        qkv, o, gates, uh, gmh = _in_projection(x, pos_tab, mod3, mod_row, n1w, wt, ba, wg, bg, br)
        ml = _mlstm(qkv, _gate_rows(gates), init, emit_state)
        yh = _hyena(uh, hy_conv_w[l], hy_conv_b[l], hy_skip[l], spectra,
                    fwd32.astype(BF16), inv32.astype(BF16), p, bb)
        y = _tail(x, pos_tab, ml[0], o, gmh, yh, mod3, mod_row, *tail_w)
        return y, ml[1:]

    y_prompt, (st_c, st_n, st_m) = run(x_prompt, None, lambda i: 0, None, True, ctx_s, 4)
    init = (state_mlstm_C,
            jnp.transpose(state_mlstm_n[:, l], (0, 2, 1, 3)),
            jnp.transpose(state_mlstm_m[:, l], (0, 2, 1))[..., None])
    y_sample, _ = run(x_sample, pos, lambda i: i + 1, init, False, dec_s // 4, 1)

    new_state_n = jnp.transpose(st_n, (0, 2, 1, 3))[:, None]
    new_state_m = jnp.transpose(st_m[..., 0], (0, 2, 1))[:, None]
    return (y_prompt, y_sample, st_c, new_state_n, new_state_m)
```

```python
import functools
import math

import numpy as np
import jax
import jax.numpy as jnp
from jax import lax
from jax.experimental import pallas as pl
from jax.experimental.pallas import tpu as pltpu

F32 = jnp.float32
BF16 = jnp.bfloat16
HIGHEST = lax.Precision.HIGHEST

D_MODEL = 1024
N_HEADS = 4
HEAD_DIM = 256
D_FF = 4 * D_MODEL
GRID_W = 64
FILT_BANDS = 16
FILT_WIDTH = 64
HYENA_MIN_DECAY = math.log(1e-2) / 1.5
HYENA_MAX_DECAY = math.log(1e-2) / 0.3
CHUNK = 128
RMS_EPS = 1e-6
N_GATES = 4 * N_HEADS
QKVO_COLS = 4 * D_MODEL
REST_COLS = 5 * D_MODEL

V7X_VMEM_BYTES = 64 * 1024 * 1024
VMEM_LIMIT = V7X_VMEM_BYTES - 8 * 1024 * 1024

TOKEN_TILE = 512
TAIL_TOKEN_TILE = 512
HYENA_LANES = 256
FILTER_LANES = 256
SPEC_ROWS = 32
MLSTM_HEADS_PER_STEP = 2


def _params(sem):
    return pltpu.CompilerParams(dimension_semantics=sem, vmem_limit_bytes=VMEM_LIMIT)


def _const_spec(shape):
    nd = len(shape)
    return pl.BlockSpec(shape, lambda *_: (0,) * nd, pipeline_mode=pl.Buffered(1))


def _sigmoid(x):
    return 1.0 / (1.0 + jnp.exp(-x))


def _rms(x):
    return x * lax.rsqrt(jnp.mean(x * x, axis=-1, keepdims=True) + RMS_EPS)


def _dot(a, b):
    return jnp.dot(a, b, preferred_element_type=F32)


def _mod_kernel(c_ref, w_ref, b_ref, o_ref):
    c = c_ref[...]
    s = c * _sigmoid(c)
    rows = s.shape[0]
    s_hi = s.astype(BF16)
    s_lo = (s - s_hi.astype(F32)).astype(BF16)
    w = w_ref[...]
    w_hi = w.astype(BF16)
    w_lo = (w - w_hi.astype(F32)).astype(BF16)
    p = _dot(jnp.concatenate([s_hi, s_lo], axis=0), w_hi)
    o_ref[...] = p[0:rows] + (p[rows:2 * rows] + _dot(s_hi, w_lo)) + b_ref[...]


def _modulation(cond, w_ada, b_ada):
    rows = cond.shape[0]
    n = w_ada.shape[1]
    tn = 1536
    return pl.pallas_call(
        _mod_kernel,
        grid=(n // tn,),
        in_specs=[pl.BlockSpec((rows, D_MODEL), lambda j: (0, 0)),
                  pl.BlockSpec((D_MODEL, tn), lambda j: (0, j)),
                  pl.BlockSpec((1, tn), lambda j: (0, j))],
        out_specs=pl.BlockSpec((rows, tn), lambda j: (0, j)),
        out_shape=jax.ShapeDtypeStruct((rows, n), F32),
        compiler_params=_params(("arbitrary",)),
        name="modulation",
    )(cond, w_ada, b_ada.reshape(1, n))


def _pos_kernel(o_ref, *, rows):
    quarter = D_MODEL // 4
    half = D_MODEL // 2
    k = lax.broadcasted_iota(jnp.int32, (1, quarter), 1).astype(F32)
    omega = jnp.exp(k * (-math.log(10000.0) / quarter))

    def axis_embed(n):
        p = lax.broadcasted_iota(jnp.int32, (n, 1), 0).astype(F32)
        a = p * omega
        return jnp.concatenate([jnp.sin(a), jnp.cos(a)], axis=-1)

    er = axis_embed(rows)
    ec = axis_embed(GRID_W)
    o_ref[:, :, 0:half] = jnp.broadcast_to(er[:, None, :], (rows, GRID_W, half))
    o_ref[:, :, half:D_MODEL] = jnp.broadcast_to(ec[None, :, :], (rows, GRID_W, half))


def _pos_table(n_tokens):
    rows = n_tokens // GRID_W
    out = pl.pallas_call(
        functools.partial(_pos_kernel, rows=rows),
        out_shape=jax.ShapeDtypeStruct((rows, GRID_W, D_MODEL), F32),
        compiler_params=pltpu.CompilerParams(vmem_limit_bytes=VMEM_LIMIT),
        name="pos_table",
    )()
    return out.reshape(n_tokens, D_MODEL)


def _dft_mats(p):
    n = 2 * p
    idx = np.arange(p, dtype=np.float64)
    ang = 2.0 * np.pi * np.outer(idx, idx) / n
    alt = np.where(np.arange(p) % 2 == 0, 1.0, -1.0)
    fwd = np.zeros((n, p))
    fwd[:p] = np.cos(ang)
    fwd[p] = alt
    fwd[p + 1:] = -np.sin(ang[1:])
    inv = np.zeros((p, n))
    inv[:, :p] = 2.0 * np.cos(ang) / n
    inv[:, 0] = 1.0 / n
    inv[:, p] = alt / n
    inv[:, p + 1:] = -2.0 * np.sin(ang[:, 1:]) / n
    sign = np.concatenate([alt, alt])
    sign[p] = 1.0
    return fwd.astype(np.float32), inv.astype(np.float32), sign.astype(np.float32).reshape(n, 1)


def _filter_kernel(w1t_ref, w1c_ref, w1s_ref, b1_ref, fr1_ref, w2_ref, b2_ref, fr2_ref, w3_ref,
                   fhi_ref, flo_ref, sign_ref, g_ref, feat_ref, *, seq, p, kk0):
    nb = seq // p
    tn = w3_ref.shape[1]
    step = pl.program_id(0)
    idx = lax.broadcasted_iota(jnp.int32, (seq, 1), 0).astype(F32)

    @pl.when(step == 0)
    def _():
        pos = lax.broadcasted_iota(jnp.int32, (1, seq), 1).astype(F32)
        t = pos / float(seq - 1)
        bands = (lax.broadcasted_iota(jnp.int32, (FILT_BANDS, 1), 0) + 1).astype(F32)
        ang = ((2.0 * math.pi / seq) * pos) * bands
        pre = (w1t_ref[...] * t
               + jnp.dot(w1c_ref[...], jnp.cos(ang), precision=HIGHEST, preferred_element_type=F32)
               + jnp.dot(w1s_ref[...], jnp.sin(ang), precision=HIGHEST, preferred_element_type=F32)
               + b1_ref[...])
        h1 = jnp.sin(fr1_ref[...] * pre)
        h2 = jnp.sin(fr2_ref[...] * (jnp.dot(w2_ref[...], h1, precision=HIGHEST, preferred_element_type=F32)
                                     + b2_ref[...]))
        feat_ref[...] = h2

    tn_dims = (((0,), (0,)), ((), ()))
    feat = feat_ref[...]
    w3 = w3_ref[...]
    feat_hi = feat.astype(BF16)
    feat_lo = (feat - feat_hi.astype(F32)).astype(BF16)
    w3_hi = w3.astype(BF16)
    w3_lo = (w3 - w3_hi.astype(F32)).astype(BF16)
    hh = (lax.dot_general(feat_hi, w3_hi, tn_dims, preferred_element_type=F32)
          + (lax.dot_general(feat_lo, w3_hi, tn_dims, preferred_element_type=F32)
             + lax.dot_general(feat_hi, w3_lo, tn_dims, preferred_element_type=F32)))
    d0 = lax.rem(step * tn, D_MODEL)
    d = (d0 + lax.broadcasted_iota(jnp.int32, (1, tn), 1)).astype(F32)
    delta = jnp.abs(HYENA_MIN_DECAY + (HYENA_MAX_DECAY - HYENA_MIN_DECAY) * d / float(D_MODEL - 1))
    centre = seq // 2
    dist = jnp.abs(idx - float(centre)) / float(centre)
    hh = hh * jnp.exp(-dist * delta)
    hh = hh / (jnp.sum(jnp.abs(hh), axis=0, keepdims=True) + 1e-6)

    fhi = fhi_ref[...]
    flo = flo_ref[...]
    off = (seq // 2) % p
    if off:
        zeros_hi = jnp.zeros((p - off, tn), F32)
        zeros_lo = jnp.zeros((off, tn), F32)
        blocks = [jnp.concatenate([zeros_hi, hh[0:off]], axis=0)]
        blocks += [hh[off + (k - 1) * p:off + k * p] for k in range(1, nb)]
        blocks += [jnp.concatenate([hh[off + (nb - 1) * p:seq], zeros_lo], axis=0)]
    else:
        blocks = [hh[k * p:(k + 1) * p] for k in range(nb)]
    spec = []
    for hb in blocks:
        hi = hb.astype(BF16)
        lo = (hb - hi.astype(F32)).astype(BF16)
        spec.append(_dot(fhi, hi) + _dot(fhi, lo) + _dot(flo, hi))
    sign = sign_ref[...]
    for k in range(g_ref.shape[0]):
        kk = kk0 + k
        if kk == 0:
            g = spec[0]
        elif kk == len(spec):
            g = sign * spec[kk - 1]
        else:
            g = spec[kk] + sign * spec[kk - 1]
        g_ref[k] = g


def _spectra_layout(seq, p):
    nb = seq // p
    centre = seq // 2
    nblk = nb + (1 if centre % p else 0)
    shift = centre // p + (1 if centre % p else 0)
    kk_lo = max(shift - (nb - 1), 0)
    kk_hi = min(shift + (nb - 1), nblk)
    return shift, kk_lo, kk_hi - kk_lo + 1


def _filter_spectra(seq, p, w1, b1, fr1, w2, b2, fr2, w3, fwd32, sign):
    tn = FILTER_LANES
    ncols = w3.shape[1]
    _, kk0, ng = _spectra_layout(seq, p)
    fhi = fwd32.astype(BF16)
    flo = (fwd32 - fhi.astype(F32)).astype(BF16)
    w1t = w1.T
    args = (w1t[:, 0:1], w1t[:, 1:1 + FILT_BANDS], w1t[:, 1 + FILT_BANDS:], b1.reshape(-1, 1), fr1.reshape(-1, 1),
            w2.T, b2.reshape(-1, 1), fr2.reshape(-1, 1))
    in_specs = [_const_spec(a.shape) for a in args]
    in_specs += [pl.BlockSpec((FILT_WIDTH, tn), lambda j: (0, j)),
                 _const_spec(fhi.shape), _const_spec(flo.shape), _const_spec(sign.shape)]
    return pl.pallas_call(
        functools.partial(_filter_kernel, seq=seq, p=p, kk0=kk0),
        grid=(ncols // tn,),
        in_specs=in_specs,
        out_specs=pl.BlockSpec((ng, 2 * p, tn), lambda j: (0, 0, j)),
        out_shape=jax.ShapeDtypeStruct((ng, 2 * p, ncols), F32),
        scratch_shapes=[pltpu.VMEM((FILT_WIDTH, seq), F32)],
        compiler_params=_params(("arbitrary",)),
        name=f"filter_spectra_{seq}",
    )(*args, w3, fhi, flo, sign)


def _inproj_kernel(*refs, has_pos):
    if has_pos:
        x_ref, pos_ref = refs[:2]
        refs = refs[2:]
    else:
        x_ref = refs[0]
        refs = refs[1:]
    (mod_ref, n1w_ref, wa_ref, ba_ref, wg_ref, bg_ref, wr_ref, br_ref,
     qkv_ref, o_ref, gates_ref, uh_ref, gmh_ref) = refs
    bb, tm, _ = x_ref.shape
    x = x_ref[...].reshape(bb * tm, D_MODEL)
    if has_pos:
        x = x + pos_ref[...]
    sh1 = mod_ref[0, :, 0:D_MODEL]
    sc1 = mod_ref[0, :, D_MODEL:2 * D_MODEL]
    hn = (_rms(x) * n1w_ref[...]) * (1.0 + sc1) + sh1
    hb = hn.astype(BF16)
    nt_dims = (((1,), (1,)), ((), ()))
    for c in range(4):
        sl = slice(c * D_MODEL, (c + 1) * D_MODEL)
        r = lax.dot_general(hb, wa_ref[sl, :], nt_dims, preferred_element_type=F32) + ba_ref[:, sl]
        if c < 3:
            qkv_ref[:, :, sl] = r.astype(BF16).reshape(bb, tm, D_MODEL)
        else:
            o_ref[...] = r.astype(BF16).reshape(bb, tm, D_MODEL)
    for c in range(5):
        sl = slice(c * D_MODEL, (c + 1) * D_MODEL)
        r = lax.dot_general(hb, wr_ref[sl, :], nt_dims, preferred_element_type=F32) + br_ref[:, sl]
        if c < 3:
            uh_ref[:, :, sl] = r.reshape(bb, tm, D_MODEL)
        else:
            gmh_ref[:, :, (c - 3) * D_MODEL:(c - 2) * D_MODEL] = r.astype(BF16).reshape(bb, tm, D_MODEL)
    p1 = lax.dot_general(wg_ref[...], hb, nt_dims, preferred_element_type=F32)
    gates = p1[0:N_GATES] + p1[N_GATES:2 * N_GATES] + bg_ref[...]
    for i in range(bb):
        gates_ref[i] = gates[:, i * tm:(i + 1) * tm]


def _in_projection(x, pos, mod3, mod_row, n1w, wt, ba, wg, bg, br):
    b, s, _ = x.shape
    tm = min(TOKEN_TILE, s)
    has_pos = pos is not None
    bb = 1 if has_pos else max(1, TOKEN_TILE // s)
    in_specs = [pl.BlockSpec((bb, tm, D_MODEL), lambda i, j: (i, j, 0))]
    args = [x]
    if has_pos:
        in_specs.append(pl.BlockSpec((tm, D_MODEL), lambda i, j: (j, 0)))
        args.append(pos)
    wa_spec = pl.BlockSpec((QKVO_COLS, D_MODEL), lambda i, j: (0, 0), pipeline_mode=pl.Buffered(1))
    wr_spec = pl.BlockSpec((pl.Element(REST_COLS), pl.Element(D_MODEL)), lambda i, j: (QKVO_COLS + N_GATES, 0),
                           pipeline_mode=pl.Buffered(1))
    in_specs += [pl.BlockSpec((1, 1, 2 * D_MODEL), lambda i, j: (mod_row(i), 0, 0)),
                 _const_spec(n1w.shape), wa_spec, _const_spec(ba.shape),
                 _const_spec(wg.shape), _const_spec(bg.shape), wr_spec, _const_spec(br.shape)]
    args += [mod3, n1w, wt, ba, wg, bg, wt, br]
    out_shape = (jax.ShapeDtypeStruct((b, s, 3 * D_MODEL), BF16),
                 jax.ShapeDtypeStruct((b, s, D_MODEL), BF16),
                 jax.ShapeDtypeStruct((b, N_GATES, s), F32),
                 jax.ShapeDtypeStruct((b, s, 3 * D_MODEL), F32),
                 jax.ShapeDtypeStruct((b, s, 2 * D_MODEL), BF16))
    out_specs = tuple(pl.BlockSpec((bb, N_GATES, tm), lambda i, j: (i, 0, j)) if k == 2 else
                      pl.BlockSpec((bb, tm, sh.shape[2]), lambda i, j: (i, j, 0)) for k, sh in enumerate(out_shape))
    return pl.pallas_call(
        functools.partial(_inproj_kernel, has_pos=has_pos),
        grid=(b // bb, s // tm),
        in_specs=in_specs,
        out_specs=out_specs,
        out_shape=out_shape,
        compiler_params=_params(("arbitrary", "arbitrary")),
        name="in_projection_pos" if has_pos else "in_projection",
    )(*args)


def _log_sigmoid(x):
    return jnp.minimum(x, 0.0) - jnp.log(1.0 + jnp.exp(-jnp.abs(x)))


def _split3(x):
    hi = x.astype(BF16).astype(F32)
    mid = (x - hi).astype(BF16).astype(F32)
    lo = (x - hi - mid).astype(BF16).astype(F32)
    return hi, mid, lo


ROW_E, ROW_B, ROW_M_INTER, ROW_W_STATE, ROW_DECAY, N_ROW_KINDS = 0, 3, 6, 7, 8, 9


def _mlstm_kernel(*refs, seq, heads, has_init, emit_state):
    q_ref, k_ref, v_ref, grow_ref = refs[:4]
    refs = refs[4:]
    if has_init:
        c0_ref, n0_ref, m0_ref = refs[:3]
        refs = refs[3:]
    h_ref = refs[0]
    refs = refs[1:]
    if emit_state:
        c_out, n_out, m_out = refs[:3]
        refs = refs[3:]
    c_s, n_s, rows_s = refs
    t = CHUNK
    nc = seq // t
    scale = HEAD_DIM ** -0.5

    r_io = lax.broadcasted_iota(jnp.int32, (t, t), 0)
    c_io = lax.broadcasted_iota(jnp.int32, (t, t), 1)
    chunk_id = lax.broadcasted_iota(jnp.int32, (nc, 1), 0)

    chains = range(2 * heads)

    m_final = []
    for ch in chains:
        hh, dr = divmod(ch, 2)
        if has_init:
            c_s[ch] = c0_ref[0, 0, dr, hh]
            n_s[ch] = n0_ref[0, hh, dr:dr + 1, :]
            m = m0_ref[0, hh, dr:dr + 1, :]
        else:
            c_s[ch] = jnp.zeros((HEAD_DIM, HEAD_DIM), F32)
            n_s[ch] = jnp.zeros((1, HEAD_DIM), F32)
            m = jnp.zeros((1, 1), F32)
        li = grow_ref[0, hh, 2 * dr]
        lf = _log_sigmoid(grow_ref[0, hh, 2 * dr + 1])
        tri = ((r_io <= c_io) if dr == 0 else (r_io >= c_io)).astype(BF16)
        b = sum(_dot(part.astype(BF16), tri) for part in _split3(lf))
        b_last = jnp.sum(lf, axis=1, keepdims=True)
        a = b_last - b + li
        a_max = jnp.max(a, axis=1, keepdims=True)
        m_before = jnp.zeros((nc, 1), F32)
        m_after = jnp.zeros((nc, 1), F32)
        for c in (range(nc) if dr == 0 else reversed(range(nc))):
            m_new = jnp.maximum(b_last[c:c + 1, :] + m, a_max[c:c + 1, :])
            m_before = jnp.where(chunk_id == c, m, m_before)
            m_after = jnp.where(chunk_id == c, m_new, m_after)
            m = m_new
        m_final.append(m)
        kinds = (*_split3(li - b), *_split3(b), b + m_before, jnp.exp(a - m_after),
                 jnp.broadcast_to(jnp.exp(b_last + m_before - m_after), (nc, t)))
        for idx, x in enumerate(kinds):
            rows_s[ch, idx] = x

    sub8 = lax.broadcasted_iota(jnp.int32, (8, t), 0)
    sub16 = lax.broadcasted_iota(jnp.int32, (16, 1), 0)

    def pick3(base, ch, c, lo, ones_lo, ones_hi):
        out = jnp.where((sub8 >= ones_lo) & (sub8 < ones_hi), 1.0, 0.0)
        for i in range(3):
            out = jnp.where(sub8 == lo + i, rows_s[ch, base + i, pl.ds(c, 1), :], out)
        return out

    def hi_lo_rows(x):
        hi = x.astype(BF16).astype(F32)
        lo = x - hi
        return jnp.where(sub16 == 0, hi, jnp.where(sub16 == 1, lo, 0.0)).astype(BF16)

    nt_dims = (((1,), (1,)), ((), ()))
    tn_dims = (((0,), (0,)), ((), ()))

    def both(i, accumulate):
        dirs = chains
        cs = [i if ch % 2 == 0 else nc - 1 - i for ch in chains]
        r0 = [c * t if isinstance(c, int) else pl.multiple_of(c * t, t) for c in cs]
        cols = [slice((ch // 2) * HEAD_DIM, (ch // 2 + 1) * HEAD_DIM) for ch in chains]
        q = [q_ref[0, pl.ds(r0[d], t), cols[d]] for d in dirs]
        k = [k_ref[0, pl.ds(r0[d], t), cols[d]] for d in dirs]
        v = [v_ref[0, pl.ds(r0[d], t), cols[d]] for d in dirs]
        row = lambda kind, d: rows_s[d, kind, pl.ds(cs[d], 1), :]
        d_t = [lax.dot_general(pick3(ROW_E, d, cs[d], 0, 3, 6), pick3(ROW_B, d, cs[d], 3, 0, 3), tn_dims,
                               preferred_element_type=F32) for d in dirs]
        d_t = [jnp.where((r_io <= c_io) if d % 2 == 0 else (r_io >= c_io), d_t[d], -jnp.inf) for d in dirs]
        m_inter = [row(ROW_M_INTER, d) for d in dirs]
        m_comb = [jnp.maximum(m_inter[d], jnp.max(d_t[d], axis=0, keepdims=True)) for d in dirs]
        pw = [jnp.exp(d_t[d] - m_comb[d]) for d in dirs]
        w_inter = [jnp.exp(m_inter[d] - m_comb[d]) for d in dirs]
        c_prev = [c_s[d] for d in dirs]
        n_prev = [n_s[d] for d in dirs]
        kq = [lax.dot_general(jnp.concatenate([k[d], hi_lo_rows(n_prev[d])], axis=0), q[d], nt_dims,
                              preferred_element_type=F32) for d in dirs]
        qc = [_dot(q[d], c_prev[d].astype(BF16)) for d in dirs]
        w_state = [row(ROW_W_STATE, d) for d in dirs]
        kw = [(k[d].T.astype(F32) * w_state[d]).astype(BF16) for d in dirs]
        sp = [kq[d][0:t] * pw[d] for d in dirs]
        den = [scale * (w_inter[d] * (kq[d][t:t + 1] + kq[d][t + 1:t + 2]) + jnp.sum(sp[d], axis=0, keepdims=True))
               for d in dirs]
        inv_den = [scale / jnp.maximum(jnp.abs(den[d]), jnp.exp(-m_comb[d])) for d in dirs]
        inter_col = [jnp.transpose(jnp.where(sub8 == 0, w_inter[d] * inv_den[d], 0.0))[:, 0:1] for d in dirs]
        sv = [lax.dot_general((sp[d] * inv_den[d]).astype(BF16), v[d], tn_dims, preferred_element_type=F32)
              for d in dirs]
        kv = [_dot(kw[d], v[d]) for d in dirs]
        nk = [_dot(hi_lo_rows(w_state[d]), k[d]) for d in dirs]
        decay = [row(ROW_DECAY, d)[:, 0:1] for d in dirs]
        for d in dirs:
            c_s[d] = decay[d] * c_prev[d] + kv[d]
            n_s[d] = decay[d] * n_prev[d] + nk[d][0:1] + nk[d][1:2]
        for d in dirs:
            h = sv[d] + inter_col[d] * qc[d]
            if accumulate:
                h_ref[0, pl.ds(r0[d], t), cols[d]] += h
            else:
                h_ref[0, pl.ds(r0[d], t), cols[d]] = h

    half = nc // 2
    if nc <= 16:
        for i in range(nc):
            both(i, i >= half)
    else:
        lax.fori_loop(0, half, lambda i, carry: (both(i, False), carry)[1], 0)
        lax.fori_loop(half, nc, lambda i, carry: (both(i, True), carry)[1], 0)

    if emit_state:
        for ch in chains:
            hh, dr = divmod(ch, 2)
            c_out[0, 0, dr, hh] = c_s[ch]
            n_out[0, hh, dr:dr + 1, :] = n_s[ch]
            m_out[0, hh, dr:dr + 1, :] = jnp.broadcast_to(m_final[ch], (1, 128))


def _mlstm(qkv, grow, init, emit_state):
    b, s, _ = qkv.shape
    nc = s // CHUNK
    has_init = init is not None
    hpb = N_HEADS if nc <= 2 else MLSTM_HEADS_PER_STEP
    nhb = N_HEADS // hpb
    width = hpb * HEAD_DIM
    in_specs = [pl.BlockSpec((1, s, width), lambda i, h: (i, 0, h)),
                pl.BlockSpec((1, s, width), lambda i, h: (i, 0, nhb + h)),
                pl.BlockSpec((1, s, width), lambda i, h: (i, 0, 2 * nhb + h)),
                pl.BlockSpec((1, hpb, 4, nc, CHUNK), lambda i, h: (i, h, 0, 0, 0))]
    args = [qkv, qkv, qkv, grow]
    if has_init:
        c0, n0, m0 = init
        in_specs += [pl.BlockSpec((1, 1, 2, hpb, HEAD_DIM, HEAD_DIM), lambda i, h: (i, 0, 0, h, 0, 0)),
                     pl.BlockSpec((1, hpb, 2, HEAD_DIM), lambda i, h: (i, h, 0, 0)),
                     pl.BlockSpec((1, hpb, 2, 1), lambda i, h: (i, h, 0, 0))]
        args += [c0, n0, m0]
    out_shape = [jax.ShapeDtypeStruct((b, s, D_MODEL), F32)]
    out_specs = [pl.BlockSpec((1, s, width), lambda i, h: (i, 0, h))]
    if emit_state:
        out_shape += [jax.ShapeDtypeStruct((b, 1, 2, N_HEADS, HEAD_DIM, HEAD_DIM), F32),
                      jax.ShapeDtypeStruct((b, N_HEADS, 2, HEAD_DIM), F32),
                      jax.ShapeDtypeStruct((b, N_HEADS, 2, 128), F32)]
        out_specs += [pl.BlockSpec((1, 1, 2, hpb, HEAD_DIM, HEAD_DIM), lambda i, h: (i, 0, 0, h, 0, 0)),
                      pl.BlockSpec((1, hpb, 2, HEAD_DIM), lambda i, h: (i, h, 0, 0)),
                      pl.BlockSpec((1, hpb, 2, 128), lambda i, h: (i, h, 0, 0))]
    return pl.pallas_call(
        functools.partial(_mlstm_kernel, seq=s, heads=hpb, has_init=has_init, emit_state=emit_state),
        grid=(b, nhb),
        in_specs=in_specs,
        out_specs=tuple(out_specs),
        out_shape=tuple(out_shape),
        scratch_shapes=[pltpu.VMEM((2 * hpb, HEAD_DIM, HEAD_DIM), F32), pltpu.VMEM((2 * hpb, 1, HEAD_DIM), F32),
                        pltpu.VMEM((2 * hpb, N_ROW_KINDS, nc, CHUNK), F32)],
        compiler_params=_params(("arbitrary", "arbitrary")),
        name=f"mlstm_{s}",
    )(*args)


def _hyena_kernel(x1_ref, x2_ref, v_ref, w1_ref, w2_ref, wv_ref, b1_ref, b2_ref, bv_ref, skip_ref,
                  g0_ref, g1_ref, fwd_ref, inv_ref, o_ref, zf_s, yf_s, *, seq, p, shift, kk0):
    nb = seq // p
    ng = g0_ref.shape[0]
    bb = x1_ref.shape[0]
    lanes = o_ref.shape[-1]
    sub = lax.broadcasted_iota(jnp.int32, (8, 1), 0)

    def conv_rows(src_ref, bi, r, w_ref, b_ref):
        rr = r * p
        up = src_ref[bi, rr - 8:rr, :][7:8, :] if rr > 0 else jnp.zeros((1, lanes), F32)
        dn = src_ref[bi, rr + p:rr + p + 8, :][0:1, :] if rr + p < seq else jnp.zeros((1, lanes), F32)
        cur = src_ref[bi, rr:rr + p, :]
        prev = pltpu.roll(cur, 1, 0)
        prev = jnp.concatenate([jnp.where(sub == 0, up, prev[0:8]), prev[8:]], axis=0)
        nxt = pltpu.roll(cur, p - 1, 0)
        nxt = jnp.concatenate([nxt[:p - 8], jnp.where(sub == 7, dn, nxt[p - 8:])], axis=0)
        return b_ref[...] + prev * w_ref[0:1, :] + cur * w_ref[1:2, :] + nxt * w_ref[2:3, :]

    fwd = fwd_ref[...]
    inv = inv_ref[...]
    terms = [[(j, i - j + shift - kk0) for j in range(nb) if 0 <= i - j + shift - kk0 < ng] for i in range(nb)]
    batch = range(bb)
    orders = ((g0_ref, x1_ref, w1_ref, b1_ref), (g1_ref, x2_ref, w2_ref, b2_ref))

    def forward_dft(bi, order, j, z):
        zf_s[bi, order, j] = _dot(fwd, z.astype(BF16))

    def spectral_mac(bi, order, i):
        g_ref = orders[order][0]
        for r in range(p // SPEC_ROWS):
            rr = r * SPEC_ROWS
            re = jnp.zeros((SPEC_ROWS, lanes), F32)
            im = jnp.zeros((SPEC_ROWS, lanes), F32)
            for j, kk in terms[i]:
                zre = zf_s[bi, order, j, rr:rr + SPEC_ROWS, :]
                zim = zf_s[bi, order, j, p + rr:p + rr + SPEC_ROWS, :]
                gre = g_ref[kk, rr:rr + SPEC_ROWS, :]
                gim = g_ref[kk, p + rr:p + rr + SPEC_ROWS, :]
                re = re + (zre * gre - zim * gim)
                im = im + (zre * gim + zim * gre)
            if r == 0:
                dc = jnp.zeros((1, lanes), F32)
                ny = jnp.zeros((1, lanes), F32)
                for j, kk in terms[i]:
                    dc = dc + zf_s[bi, order, j, 0:1, :] * g_ref[kk, 0:1, :]
                    ny = ny + zf_s[bi, order, j, p:p + 1, :] * g_ref[kk, p:p + 1, :]
                first = lax.broadcasted_iota(jnp.int32, (SPEC_ROWS, 1), 0) == 0
                re = jnp.where(first, dc, re)
                im = jnp.where(first, ny, im)
            yf_s[bi, i, rr:rr + SPEC_ROWS, :] = re
            yf_s[bi, i, p + rr:p + rr + SPEC_ROWS, :] = im

    def finish_block(bi, order, i, y):
        _, x_ref, w_ref, b_ref = orders[order]
        rows = slice(i * p, (i + 1) * p)
        z = conv_rows(x_ref, bi, i, w_ref, b_ref) * (y + skip_ref[order:order + 1, :] * o_ref[bi, rows, :])
        o_ref[bi, rows, :] = z
        if order + 1 < len(orders):
            forward_dft(bi, order + 1, i, z)

    for j in range(nb):
        for bi in batch:
            z = conv_rows(v_ref, bi, j, wv_ref, bv_ref)
            o_ref[bi, j * p:(j + 1) * p, :] = z
            forward_dft(bi, 0, j, z)
    for order in range(len(orders)):
        pending = {}
        for i in range(nb):
            for bi in batch:
                spectral_mac(bi, order, i)
            for bi in batch:
                pending[bi, i] = _dot(inv, yf_s[bi, i].astype(BF16))
            if i >= 1:
                for bi in batch:
                    finish_block(bi, order, i - 1, pending.pop((bi, i - 1)))
        for bi in batch:
            finish_block(bi, order, nb - 1, pending.pop((bi, nb - 1)))


def _hyena(uh, conv_w, conv_b, skip, spectra, fwd, inv, p, bb):
    b, s, _ = uh.shape
    dc = HYENA_LANES
    nct = D_MODEL // dc
    nb = s // p
    shift, kk0, ng = _spectra_layout(s, p)
    conv_b = conv_b.reshape(1, -1)

    def part(k):
        return pl.BlockSpec((bb, s, dc), lambda c, i, k=k: (i, 0, k * nct + c))

    def wpart(k, rows):
        return pl.BlockSpec((rows, dc), lambda c, i, k=k: (0, k * nct + c))

    def gpart(order):
        return pl.BlockSpec((ng, 2 * p, dc), lambda c, i, order=order: (0, 0, order * nct + c),
                            pipeline_mode=pl.Buffered(1))

    in_specs = [part(0), part(1), part(2), wpart(0, 3), wpart(1, 3), wpart(2, 3),
                wpart(0, 1), wpart(1, 1), wpart(2, 1),
                pl.BlockSpec((2, dc), lambda c, i: (0, c)),
                gpart(0), gpart(1), _const_spec(fwd.shape), _const_spec(inv.shape)]
    return pl.pallas_call(
        functools.partial(_hyena_kernel, seq=s, p=p, shift=shift, kk0=kk0),
        grid=(nct, b // bb),
        in_specs=in_specs,
        out_specs=pl.BlockSpec((bb, s, dc), lambda c, i: (i, 0, c)),
        out_shape=jax.ShapeDtypeStruct((b, s, D_MODEL), F32),
        scratch_shapes=[pltpu.VMEM((bb, 2, nb, 2 * p, dc), F32), pltpu.VMEM((bb, nb, 2 * p, dc), F32)],
        compiler_params=_params(("arbitrary", "arbitrary")),
        name=f"hyena_{s}",
    )(uh, uh, uh, conv_w, conv_w, conv_w, conv_b, conv_b, conv_b, skip, spectra, spectra, fwd, inv)


def _tail_kernel(*refs, has_pos):
    if has_pos:
        x_ref, pos_ref = refs[:2]
        refs = refs[2:]
    else:
        x_ref = refs[0]
        refs = refs[1:]
    (h_ref, o_ref, gm_ref, gh_ref, yh_ref, modb_ref, modc_ref, mnw_ref, n2w_ref, fnw_ref,
     wbm_ref, wbh_ref, wout_ref, w1_ref, b1_ref, w2_ref, b2_ref, y_ref) = refs
    bb, tm, _ = x_ref.shape
    rows = bb * tm
    tile = lambda ref: ref[...].reshape(rows, D_MODEL)
    x = tile(x_ref)
    if has_pos:
        x = x + pos_ref[...]
    g1 = modb_ref[0, :, 0:D_MODEL]
    sh2 = modb_ref[0, :, D_MODEL:2 * D_MODEL]
    sc2 = modc_ref[0, :, 0:D_MODEL]
    g2 = modc_ref[0, :, D_MODEL:2 * D_MODEL]
    h = tile(h_ref)
    heads = [_rms(h[:, hd * HEAD_DIM:(hd + 1) * HEAD_DIM]) for hd in range(N_HEADS)]
    hm = jnp.concatenate(heads, axis=-1) * mnw_ref[...] * _sigmoid(tile(o_ref).astype(F32))
    merged = (_sigmoid(tile(gm_ref).astype(F32)) * _dot(hm.astype(BF16), wbm_ref[...])
              + _sigmoid(tile(gh_ref).astype(F32)) * _dot(tile(yh_ref).astype(BF16), wbh_ref[...]))
    x1 = x + g1 * _dot(merged.astype(BF16), wout_ref[...])
    hn2 = ((_rms(x1) * n2w_ref[...]) * (1.0 + sc2) + sh2).astype(BF16)
    ff = b2_ref[...]
    for kc in range(D_FF // D_MODEL):
        sl = slice(kc * D_MODEL, (kc + 1) * D_MODEL)
        a = jnp.maximum(_dot(hn2, w1_ref[:, sl]) + b1_ref[:, sl], 0.0)
        ff = ff + _dot((a * a).astype(BF16), w2_ref[sl, :])
    x2 = x1 + g2 * ff
    y_ref[...] = (_rms(x2) * fnw_ref[...]).reshape(bb, tm, D_MODEL)


def _tail(x, pos, h, o, gmh, yh, mod3, mod_row, mnw, n2w, fnw, wbm, wbh, wout, w1, b1, w2, b2):
    b, s, _ = x.shape
    tm = min(TAIL_TOKEN_TILE, s)
    has_pos = pos is not None
    bb = 1 if has_pos else max(1, TAIL_TOKEN_TILE // s)
    tok = pl.BlockSpec((bb, tm, D_MODEL), lambda i, j: (i, j, 0))
    in_specs = [tok]
    args = [x]
    if has_pos:
        in_specs.append(pl.BlockSpec((tm, D_MODEL), lambda i, j: (j, 0)))
        args.append(pos)
    in_specs += [tok, tok, tok, pl.BlockSpec((bb, tm, D_MODEL), lambda i, j: (i, j, 1)), tok,
                 pl.BlockSpec((1, 1, 2 * D_MODEL), lambda i, j: (mod_row(i), 0, 1)),
                 pl.BlockSpec((1, 1, 2 * D_MODEL), lambda i, j: (mod_row(i), 0, 2))]
    args += [h, o, gmh, gmh, yh, mod3, mod3]
    consts = [mnw, n2w, fnw, wbm, wbh, wout, w1, b1, w2, b2]
    in_specs += [_const_spec(a.shape) for a in consts]
    args += consts
    return pl.pallas_call(
        functools.partial(_tail_kernel, has_pos=has_pos),
        grid=(b // bb, s // tm),
        in_specs=in_specs,
        out_specs=tok,
        out_shape=jax.ShapeDtypeStruct((b, s, D_MODEL), F32),
        compiler_params=_params(("arbitrary", "arbitrary")),
        name="tail_pos" if has_pos else "tail",
    )(*args)


def _gate_rows(gates_t):
    b, _, s = gates_t.shape
    return gates_t.reshape(b, N_HEADS, 4, s // CHUNK, CHUNK)


def kernel(x_prompt, x_sample, state_mlstm_C, state_mlstm_n, state_mlstm_m, c, c_ctx, w_ada, b_ada, norm1_w,
           w_in, b_in, hy_conv_w, hy_conv_b, filt_w1, filt_b1, filt_freq1, filt_w2, filt_b2, filt_freq2,
           filt_w3, hy_skip, mlstm_norm_w, w_br_m, w_br_h, w_out, norm2_w, w_mlp1, b_mlp1, w_mlp2, b_mlp2,
           final_norm_w):
    depth = w_ada.shape[0]
    assert depth == 1, "single-layer configuration"
    l = 0
    dec_b, dec_s, _ = x_sample.shape
    ctx_s = x_prompt.shape[1]

    n_rows = -(-(1 + dec_b) // 8) * 8
    cond = jnp.concatenate([c_ctx[None, :], c, jnp.zeros((n_rows - 1 - dec_b, D_MODEL), F32)], axis=0)
    mod = _modulation(cond, w_ada[l], b_ada[l])
    mod3 = mod.reshape(n_rows, 1, 6 * D_MODEL)

    wt32 = jnp.transpose(w_in[l])
    wt = wt32.astype(BF16)
    bias = b_in[l]
    ba = bias[:QKVO_COLS].reshape(1, -1)
    head_major = np.arange(N_GATES).reshape(4, N_HEADS).T.reshape(-1)
    wg32 = wt32[QKVO_COLS:QKVO_COLS + N_GATES][head_major]
    wg_hi = wg32.astype(BF16)
    wg = jnp.concatenate([wg_hi, (wg32 - wg_hi.astype(F32)).astype(BF16)], axis=0)
    bg = bias[QKVO_COLS:QKVO_COLS + N_GATES][head_major].reshape(-1, 1)
    br = bias[QKVO_COLS + N_GATES:].reshape(1, -1)
    n1w = norm1_w[l].reshape(1, -1)
    tail_w = (mlstm_norm_w[l].reshape(1, -1), norm2_w[l].reshape(1, -1), final_norm_w.reshape(1, -1),
              w_br_m[l].astype(BF16), w_br_h[l].astype(BF16), w_out[l].astype(BF16),
              w_mlp1[l].astype(BF16), b_mlp1[l].reshape(1, -1), w_mlp2[l].astype(BF16), b_mlp2[l].reshape(1, -1))
    filt = (filt_w1[l], filt_b1[l], filt_freq1[l], filt_w2[l], filt_b2[l], filt_freq2[l], filt_w3[l])

    pos = _pos_table(dec_s)

    def run(x, pos_tab, mod_row, init, emit_state, p, bb):
        s = x.shape[1]
        fwd32, inv32, sign = (jnp.asarray(a) for a in _dft_mats(p))
        spectra = _filter_spectra(s, p, *filt, fwd32, sign)
        qkv, o, gates, uh, gmh = _in_projection(x, pos_tab, mod3, mod_row, n1w, wt, ba, wg, bg, br)
        ml = _mlstm(qkv, _gate_rows(gates), init, emit_state)
        yh = _hyena(uh, hy_conv_w[l], hy_conv_b[l], hy_skip[l], spectra,
                    fwd32.astype(BF16), inv32.astype(BF16), p, bb)
        y = _tail(x, pos_tab, ml[0], o, gmh, yh, mod3, mod_row, *tail_w)
        return y, ml[1:]

    y_prompt, (st_c, st_n, st_m) = run(x_prompt, None, lambda i: 0, None, True, ctx_s, 4)
    init = (state_mlstm_C,
            jnp.transpose(state_mlstm_n[:, l], (0, 2, 1, 3)),
            jnp.transpose(state_mlstm_m[:, l], (0, 2, 1))[..., None])
    y_sample, _ = run(x_sample, pos, lambda i: i + 1, init, False, dec_s // 4, 1)

    new_state_n = jnp.transpose(st_n, (0, 2, 1, 3))[:, None]
    new_state_m = jnp.transpose(st_m[..., 0], (0, 2, 1))[:, None]
    return (y_prompt, y_sample, st_c, new_state_n, new_state_m)
```

```python
import functools
import math

import numpy as np
import jax
import jax.numpy as jnp
from jax import lax
from jax.experimental import pallas as pl
from jax.experimental.pallas import tpu as pltpu

F32 = jnp.float32
BF16 = jnp.bfloat16
HIGHEST = lax.Precision.HIGHEST

D_MODEL = 1024
N_HEADS = 4
HEAD_DIM = 256
D_FF = 4 * D_MODEL
GRID_W = 64
FILT_BANDS = 16
FILT_WIDTH = 64
HYENA_MIN_DECAY = math.log(1e-2) / 1.5
HYENA_MAX_DECAY = math.log(1e-2) / 0.3
CHUNK = 128
RMS_EPS = 1e-6
N_GATES = 4 * N_HEADS
QKVO_COLS = 4 * D_MODEL
REST_COLS = 5 * D_MODEL

V7X_VMEM_BYTES = 64 * 1024 * 1024
VMEM_LIMIT = V7X_VMEM_BYTES - 8 * 1024 * 1024

TOKEN_TILE = 512
TAIL_TOKEN_TILE = 512
HYENA_LANES = 256
FILTER_LANES = 256
SPEC_ROWS = 32
MLSTM_HEADS_PER_STEP = 2


def _params(sem):
    return pltpu.CompilerParams(dimension_semantics=sem, vmem_limit_bytes=VMEM_LIMIT)


def _const_spec(shape):
    nd = len(shape)
    return pl.BlockSpec(shape, lambda *_: (0,) * nd, pipeline_mode=pl.Buffered(1))


def _sigmoid(x):
    return 1.0 / (1.0 + jnp.exp(-x))


def _rms(x):
    return x * lax.rsqrt(jnp.mean(x * x, axis=-1, keepdims=True) + RMS_EPS)


def _dot(a, b):
    return jnp.dot(a, b, preferred_element_type=F32)


def _mod_kernel(c_ref, w_ref, b_ref, o_ref):
    c = c_ref[...]
    s = c * _sigmoid(c)
    rows = s.shape[0]
    s_hi = s.astype(BF16)
    s_lo = (s - s_hi.astype(F32)).astype(BF16)
    w = w_ref[...]
    w_hi = w.astype(BF16)
    w_lo = (w - w_hi.astype(F32)).astype(BF16)
    p = _dot(jnp.concatenate([s_hi, s_lo], axis=0), w_hi)
    o_ref[...] = p[0:rows] + (p[rows:2 * rows] + _dot(s_hi, w_lo)) + b_ref[...]


def _modulation(cond, w_ada, b_ada):
    rows = cond.shape[0]
    n = w_ada.shape[1]
    tn = 1536
    return pl.pallas_call(
        _mod_kernel,
        grid=(n // tn,),
        in_specs=[pl.BlockSpec((rows, D_MODEL), lambda j: (0, 0)),
                  pl.BlockSpec((D_MODEL, tn), lambda j: (0, j)),
                  pl.BlockSpec((1, tn), lambda j: (0, j))],
        out_specs=pl.BlockSpec((rows, tn), lambda j: (0, j)),
        out_shape=jax.ShapeDtypeStruct((rows, n), F32),
        compiler_params=_params(("arbitrary",)),
        name="modulation",
    )(cond, w_ada, b_ada.reshape(1, n))


def _pos_kernel(o_ref, *, rows):
    quarter = D_MODEL // 4
    half = D_MODEL // 2
    k = lax.broadcasted_iota(jnp.int32, (1, quarter), 1).astype(F32)
    omega = jnp.exp(k * (-math.log(10000.0) / quarter))

    def axis_embed(n):
        p = lax.broadcasted_iota(jnp.int32, (n, 1), 0).astype(F32)
        a = p * omega
        return jnp.concatenate([jnp.sin(a), jnp.cos(a)], axis=-1)

    er = axis_embed(rows)
    ec = axis_embed(GRID_W)
    o_ref[:, :, 0:half] = jnp.broadcast_to(er[:, None, :], (rows, GRID_W, half))
    o_ref[:, :, half:D_MODEL] = jnp.broadcast_to(ec[None, :, :], (rows, GRID_W, half))


def _pos_table(n_tokens):
    rows = n_tokens // GRID_W
    out = pl.pallas_call(
        functools.partial(_pos_kernel, rows=rows),
        out_shape=jax.ShapeDtypeStruct((rows, GRID_W, D_MODEL), F32),
        compiler_params=pltpu.CompilerParams(vmem_limit_bytes=VMEM_LIMIT),
        name="pos_table",
    )()
    return out.reshape(n_tokens, D_MODEL)


def _dft_mats(p):
    n = 2 * p
    idx = np.arange(p, dtype=np.float64)
    ang = 2.0 * np.pi * np.outer(idx, idx) / n
    alt = np.where(np.arange(p) % 2 == 0, 1.0, -1.0)
    fwd = np.zeros((n, p))
    fwd[:p] = np.cos(ang)
    fwd[p] = alt
    fwd[p + 1:] = -np.sin(ang[1:])
    inv = np.zeros((p, n))
    inv[:, :p] = 2.0 * np.cos(ang) / n
    inv[:, 0] = 1.0 / n
    inv[:, p] = alt / n
    inv[:, p + 1:] = -2.0 * np.sin(ang[:, 1:]) / n
    sign = np.concatenate([alt, alt])
    sign[p] = 1.0
    return fwd.astype(np.float32), inv.astype(np.float32), sign.astype(np.float32).reshape(n, 1)


def _filter_kernel(w1t_ref, w1c_ref, w1s_ref, b1_ref, fr1_ref, w2_ref, b2_ref, fr2_ref, w3_ref,
                   fhi_ref, sign_ref, g_ref, feat_ref, *, seq, p, kk0):
    nb = seq // p
    tn = w3_ref.shape[1]
    step = pl.program_id(0)
    idx = lax.broadcasted_iota(jnp.int32, (seq, 1), 0).astype(F32)

    @pl.when(step == 0)
    def _():
        pos = lax.broadcasted_iota(jnp.int32, (1, seq), 1).astype(F32)
        t = pos / float(seq - 1)
        bands = (lax.broadcasted_iota(jnp.int32, (FILT_BANDS, 1), 0) + 1).astype(F32)
        ang = ((2.0 * math.pi / seq) * pos) * bands
        pre = (w1t_ref[...] * t
               + jnp.dot(w1c_ref[...], jnp.cos(ang), precision=HIGHEST, preferred_element_type=F32)
               + jnp.dot(w1s_ref[...], jnp.sin(ang), precision=HIGHEST, preferred_element_type=F32)
               + b1_ref[...])
        h1 = jnp.sin(fr1_ref[...] * pre)
        h2 = jnp.sin(fr2_ref[...] * (jnp.dot(w2_ref[...], h1, precision=HIGHEST, preferred_element_type=F32)
                                     + b2_ref[...]))
        feat_ref[...] = h2

    tn_dims = (((0,), (0,)), ((), ()))
    feat = feat_ref[...]
    w3 = w3_ref[...]
    feat_hi = feat.astype(BF16)
    feat_lo = (feat - feat_hi.astype(F32)).astype(BF16)
    w3_hi = w3.astype(BF16)
    w3_lo = (w3 - w3_hi.astype(F32)).astype(BF16)
    hh = (lax.dot_general(feat_hi, w3_hi, tn_dims, preferred_element_type=F32)
          + (lax.dot_general(feat_lo, w3_hi, tn_dims, preferred_element_type=F32)
             + lax.dot_general(feat_hi, w3_lo, tn_dims, preferred_element_type=F32)))
    d0 = lax.rem(step * tn, D_MODEL)
    d = (d0 + lax.broadcasted_iota(jnp.int32, (1, tn), 1)).astype(F32)
    delta = jnp.abs(HYENA_MIN_DECAY + (HYENA_MAX_DECAY - HYENA_MIN_DECAY) * d / float(D_MODEL - 1))
    centre = seq // 2
    dist = jnp.abs(idx - float(centre)) / float(centre)
    hh = hh * jnp.exp(-dist * delta)
    hh = hh / (jnp.sum(jnp.abs(hh), axis=0, keepdims=True) + 1e-6)

    fhi = fhi_ref[...]
    off = (seq // 2) % p
    if off:
        zeros_hi = jnp.zeros((p - off, tn), F32)
        zeros_lo = jnp.zeros((off, tn), F32)
        blocks = [jnp.concatenate([zeros_hi, hh[0:off]], axis=0)]
        blocks += [hh[off + (k - 1) * p:off + k * p] for k in range(1, nb)]
        blocks += [jnp.concatenate([hh[off + (nb - 1) * p:seq], zeros_lo], axis=0)]
    else:
        blocks = [hh[k * p:(k + 1) * p] for k in range(nb)]
    spec = []
    for hb in blocks:
        hi = hb.astype(BF16)
        lo = (hb - hi.astype(F32)).astype(BF16)
        spec.append(_dot(fhi, hi) + _dot(fhi, lo))
    sign = sign_ref[...]
    for k in range(g_ref.shape[0]):
        kk = kk0 + k
        if kk == 0:
            g = spec[0]
        elif kk == len(spec):
            g = sign * spec[kk - 1]
        else:
            g = spec[kk] + sign * spec[kk - 1]
        g_ref[k] = g


def _spectra_layout(seq, p):
    nb = seq // p
    centre = seq // 2
    nblk = nb + (1 if centre % p else 0)
    shift = centre // p + (1 if centre % p else 0)
    kk_lo = max(shift - (nb - 1), 0)
    kk_hi = min(shift + (nb - 1), nblk)
    return shift, kk_lo, kk_hi - kk_lo + 1


def _filter_spectra(seq, p, w1, b1, fr1, w2, b2, fr2, w3, fwd32, sign):
    tn = FILTER_LANES
    ncols = w3.shape[1]
    _, kk0, ng = _spectra_layout(seq, p)
    fhi = fwd32.astype(BF16)
    w1t = w1.T
    args = (w1t[:, 0:1], w1t[:, 1:1 + FILT_BANDS], w1t[:, 1 + FILT_BANDS:], b1.reshape(-1, 1), fr1.reshape(-1, 1),
            w2.T, b2.reshape(-1, 1), fr2.reshape(-1, 1))
    in_specs = [_const_spec(a.shape) for a in args]
    in_specs += [pl.BlockSpec((FILT_WIDTH, tn), lambda j: (0, j)),
                 _const_spec(fhi.shape), _const_spec(sign.shape)]
    return pl.pallas_call(
        functools.partial(_filter_kernel, seq=seq, p=p, kk0=kk0),
        grid=(ncols // tn,),
        in_specs=in_specs,
        out_specs=pl.BlockSpec((ng, 2 * p, tn), lambda j: (0, 0, j)),
        out_shape=jax.ShapeDtypeStruct((ng, 2 * p, ncols), F32),
        scratch_shapes=[pltpu.VMEM((FILT_WIDTH, seq), F32)],
        compiler_params=_params(("arbitrary",)),
        name=f"filter_spectra_{seq}",
    )(*args, w3, fhi, sign)


def _inproj_kernel(*refs, has_pos):
    if has_pos:
        x_ref, pos_ref = refs[:2]
        refs = refs[2:]
    else:
        x_ref = refs[0]
        refs = refs[1:]
    (mod_ref, n1w_ref, wa_ref, ba_ref, wg_ref, bg_ref, wr_ref, br_ref,
     qkv_ref, o_ref, gates_ref, uh_ref, gmh_ref) = refs
    bb, tm, _ = x_ref.shape
    x = x_ref[...].reshape(bb * tm, D_MODEL)
    if has_pos:
        x = x + pos_ref[...]
    sh1 = mod_ref[0, :, 0:D_MODEL]
    sc1 = mod_ref[0, :, D_MODEL:2 * D_MODEL]
    hn = (_rms(x) * n1w_ref[...]) * (1.0 + sc1) + sh1
    hb = hn.astype(BF16)
    nt_dims = (((1,), (1,)), ((), ()))
    for c in range(4):
        sl = slice(c * D_MODEL, (c + 1) * D_MODEL)
        r = lax.dot_general(hb, wa_ref[sl, :], nt_dims, preferred_element_type=F32) + ba_ref[:, sl]
        if c < 3:
            qkv_ref[:, :, sl] = r.astype(BF16).reshape(bb, tm, D_MODEL)
        else:
            o_ref[...] = r.astype(BF16).reshape(bb, tm, D_MODEL)
    for c in range(5):
        sl = slice(c * D_MODEL, (c + 1) * D_MODEL)
        r = lax.dot_general(hb, wr_ref[sl, :], nt_dims, preferred_element_type=F32) + br_ref[:, sl]
        if c < 3:
            uh_ref[:, :, sl] = r.reshape(bb, tm, D_MODEL)
        else:
            gmh_ref[:, :, (c - 3) * D_MODEL:(c - 2) * D_MODEL] = r.astype(BF16).reshape(bb, tm, D_MODEL)
    p1 = lax.dot_general(wg_ref[...], hb, nt_dims, preferred_element_type=F32)
    gates = p1[0:N_GATES] + p1[N_GATES:2 * N_GATES] + bg_ref[...]
    for i in range(bb):
        gates_ref[i] = gates[:, i * tm:(i + 1) * tm]


def _in_projection(x, pos, mod3, mod_row, n1w, wt, ba, wg, bg, br):
    b, s, _ = x.shape
    tm = min(TOKEN_TILE, s)
    has_pos = pos is not None
    bb = 1 if has_pos else max(1, TOKEN_TILE // s)
    in_specs = [pl.BlockSpec((bb, tm, D_MODEL), lambda i, j: (i, j, 0))]
    args = [x]
    if has_pos:
        in_specs.append(pl.BlockSpec((tm, D_MODEL), lambda i, j: (j, 0)))
        args.append(pos)
    wa_spec = pl.BlockSpec((QKVO_COLS, D_MODEL), lambda i, j: (0, 0), pipeline_mode=pl.Buffered(1))
    wr_spec = pl.BlockSpec((pl.Element(REST_COLS), pl.Element(D_MODEL)), lambda i, j: (QKVO_COLS + N_GATES, 0),
                           pipeline_mode=pl.Buffered(1))
    in_specs += [pl.BlockSpec((1, 1, 2 * D_MODEL), lambda i, j: (mod_row(i), 0, 0)),
                 _const_spec(n1w.shape), wa_spec, _const_spec(ba.shape),
                 _const_spec(wg.shape), _const_spec(bg.shape), wr_spec, _const_spec(br.shape)]
    args += [mod3, n1w, wt, ba, wg, bg, wt, br]
    out_shape = (jax.ShapeDtypeStruct((b, s, 3 * D_MODEL), BF16),
                 jax.ShapeDtypeStruct((b, s, D_MODEL), BF16),
                 jax.ShapeDtypeStruct((b, N_GATES, s), F32),
                 jax.ShapeDtypeStruct((b, s, 3 * D_MODEL), F32),
                 jax.ShapeDtypeStruct((b, s, 2 * D_MODEL), BF16))
    out_specs = tuple(pl.BlockSpec((bb, N_GATES, tm), lambda i, j: (i, 0, j)) if k == 2 else
                      pl.BlockSpec((bb, tm, sh.shape[2]), lambda i, j: (i, j, 0)) for k, sh in enumerate(out_shape))
    return pl.pallas_call(
        functools.partial(_inproj_kernel, has_pos=has_pos),
        grid=(b // bb, s // tm),
        in_specs=in_specs,
        out_specs=out_specs,
        out_shape=out_shape,
        compiler_params=_params(("arbitrary", "arbitrary")),
        name="in_projection_pos" if has_pos else "in_projection",
    )(*args)


def _log_sigmoid(x):
    return jnp.minimum(x, 0.0) - jnp.log(1.0 + jnp.exp(-jnp.abs(x)))


def _split3(x):
    hi = x.astype(BF16).astype(F32)
    mid = (x - hi).astype(BF16).astype(F32)
    lo = (x - hi - mid).astype(BF16).astype(F32)
    return hi, mid, lo


ROW_E, ROW_B, ROW_M_INTER, ROW_W_STATE, ROW_DECAY, N_ROW_KINDS = 0, 3, 6, 7, 8, 9


def _mlstm_kernel(*refs, seq, heads, has_init, emit_state):
    q_ref, k_ref, v_ref, grow_ref = refs[:4]
    refs = refs[4:]
    if has_init:
        c0_ref, n0_ref, m0_ref = refs[:3]
        refs = refs[3:]
    h_ref = refs[0]
    refs = refs[1:]
    if emit_state:
        c_out, n_out, m_out = refs[:3]
        refs = refs[3:]
    c_s, n_s, rows_s = refs
    t = CHUNK
    nc = seq // t
    scale = HEAD_DIM ** -0.5

    r_io = lax.broadcasted_iota(jnp.int32, (t, t), 0)
    c_io = lax.broadcasted_iota(jnp.int32, (t, t), 1)
    chunk_id = lax.broadcasted_iota(jnp.int32, (nc, 1), 0)

    chains = range(2 * heads)

    m_final = []
    for ch in chains:
        hh, dr = divmod(ch, 2)
        if has_init:
            c_s[ch] = c0_ref[0, 0, dr, hh]
            n_s[ch] = n0_ref[0, hh, dr:dr + 1, :]
            m = m0_ref[0, hh, dr:dr + 1, :]
        else:
            c_s[ch] = jnp.zeros((HEAD_DIM, HEAD_DIM), F32)
            n_s[ch] = jnp.zeros((1, HEAD_DIM), F32)
            m = jnp.zeros((1, 1), F32)
        li = grow_ref[0, hh, 2 * dr]
        lf = _log_sigmoid(grow_ref[0, hh, 2 * dr + 1])
        tri = ((r_io <= c_io) if dr == 0 else (r_io >= c_io)).astype(BF16)
        b = sum(_dot(part.astype(BF16), tri) for part in _split3(lf))
        b_last = jnp.sum(lf, axis=1, keepdims=True)
        a = b_last - b + li
        a_max = jnp.max(a, axis=1, keepdims=True)
        m_before = jnp.zeros((nc, 1), F32)
        m_after = jnp.zeros((nc, 1), F32)
        for c in (range(nc) if dr == 0 else reversed(range(nc))):
            m_new = jnp.maximum(b_last[c:c + 1, :] + m, a_max[c:c + 1, :])
            m_before = jnp.where(chunk_id == c, m, m_before)
            m_after = jnp.where(chunk_id == c, m_new, m_after)
            m = m_new
        m_final.append(m)
        kinds = (*_split3(li - b), *_split3(b), b + m_before, jnp.exp(a - m_after),
                 jnp.broadcast_to(jnp.exp(b_last + m_before - m_after), (nc, t)))
        for idx, x in enumerate(kinds):
            rows_s[ch, idx] = x

    sub8 = lax.broadcasted_iota(jnp.int32, (8, t), 0)
    sub16 = lax.broadcasted_iota(jnp.int32, (16, 1), 0)

    def pick3(base, ch, c, lo, ones_lo, ones_hi):
        out = jnp.where((sub8 >= ones_lo) & (sub8 < ones_hi), 1.0, 0.0)
        for i in range(3):
            out = jnp.where(sub8 == lo + i, rows_s[ch, base + i, pl.ds(c, 1), :], out)
        return out

    def hi_lo_rows(x):
        hi = x.astype(BF16).astype(F32)
        lo = x - hi
        return jnp.where(sub16 == 0, hi, jnp.where(sub16 == 1, lo, 0.0)).astype(BF16)

    nt_dims = (((1,), (1,)), ((), ()))
    tn_dims = (((0,), (0,)), ((), ()))

    def both(i, accumulate):
        dirs = chains
        cs = [i if ch % 2 == 0 else nc - 1 - i for ch in chains]
        r0 = [c * t if isinstance(c, int) else pl.multiple_of(c * t, t) for c in cs]
        cols = [slice((ch // 2) * HEAD_DIM, (ch // 2 + 1) * HEAD_DIM) for ch in chains]
        q = [q_ref[0, pl.ds(r0[d], t), cols[d]] for d in dirs]
        k = [k_ref[0, pl.ds(r0[d], t), cols[d]] for d in dirs]
        v = [v_ref[0, pl.ds(r0[d], t), cols[d]] for d in dirs]
        row = lambda kind, d: rows_s[d, kind, pl.ds(cs[d], 1), :]
        d_t = [lax.dot_general(pick3(ROW_E, d, cs[d], 0, 3, 6), pick3(ROW_B, d, cs[d], 3, 0, 3), tn_dims,
                               preferred_element_type=F32) for d in dirs]
        d_t = [jnp.where((r_io <= c_io) if d % 2 == 0 else (r_io >= c_io), d_t[d], -jnp.inf) for d in dirs]
        m_inter = [row(ROW_M_INTER, d) for d in dirs]
        m_comb = [jnp.maximum(m_inter[d], jnp.max(d_t[d], axis=0, keepdims=True)) for d in dirs]
        pw = [jnp.exp(d_t[d] - m_comb[d]) for d in dirs]
        w_inter = [jnp.exp(m_inter[d] - m_comb[d]) for d in dirs]
        c_prev = [c_s[d] for d in dirs]
        n_prev = [n_s[d] for d in dirs]
        kq = [lax.dot_general(jnp.concatenate([k[d], hi_lo_rows(n_prev[d])], axis=0), q[d], nt_dims,
                              preferred_element_type=F32) for d in dirs]
        qc = [_dot(q[d], c_prev[d].astype(BF16)) for d in dirs]
        w_state = [row(ROW_W_STATE, d) for d in dirs]
        kw = [(k[d].T.astype(F32) * w_state[d]).astype(BF16) for d in dirs]
        sp = [kq[d][0:t] * pw[d] for d in dirs]
        den = [scale * (w_inter[d] * (kq[d][t:t + 1] + kq[d][t + 1:t + 2]) + jnp.sum(sp[d], axis=0, keepdims=True))
               for d in dirs]
        inv_den = [scale / jnp.maximum(jnp.abs(den[d]), jnp.exp(-m_comb[d])) for d in dirs]
        inter_col = [jnp.transpose(jnp.where(sub8 == 0, w_inter[d] * inv_den[d], 0.0))[:, 0:1] for d in dirs]
        sv = [lax.dot_general((sp[d] * inv_den[d]).astype(BF16), v[d], tn_dims, preferred_element_type=F32)
              for d in dirs]
        kv = [_dot(kw[d], v[d]) for d in dirs]
        nk = [_dot(hi_lo_rows(w_state[d]), k[d]) for d in dirs]
        decay = [row(ROW_DECAY, d)[:, 0:1] for d in dirs]
        for d in dirs:
            c_s[d] = decay[d] * c_prev[d] + kv[d]
            n_s[d] = decay[d] * n_prev[d] + nk[d][0:1] + nk[d][1:2]
        for d in dirs:
            h = sv[d] + inter_col[d] * qc[d]
            if accumulate:
                h_ref[0, pl.ds(r0[d], t), cols[d]] += h
            else:
                h_ref[0, pl.ds(r0[d], t), cols[d]] = h

    half = nc // 2
    if nc <= 16:
        for i in range(nc):
            both(i, i >= half)
    else:
        lax.fori_loop(0, half, lambda i, carry: (both(i, False), carry)[1], 0)
        lax.fori_loop(half, nc, lambda i, carry: (both(i, True), carry)[1], 0)

    if emit_state:
        for ch in chains:
            hh, dr = divmod(ch, 2)
            c_out[0, 0, dr, hh] = c_s[ch]
            n_out[0, hh, dr:dr + 1, :] = n_s[ch]
            m_out[0, hh, dr:dr + 1, :] = jnp.broadcast_to(m_final[ch], (1, 128))


def _mlstm(qkv, grow, init, emit_state):
    b, s, _ = qkv.shape
    nc = s // CHUNK
    has_init = init is not None
    hpb = N_HEADS if nc <= 2 else MLSTM_HEADS_PER_STEP
    nhb = N_HEADS // hpb
    width = hpb * HEAD_DIM
    in_specs = [pl.BlockSpec((1, s, width), lambda i, h: (i, 0, h)),
                pl.BlockSpec((1, s, width), lambda i, h: (i, 0, nhb + h)),
                pl.BlockSpec((1, s, width), lambda i, h: (i, 0, 2 * nhb + h)),
                pl.BlockSpec((1, hpb, 4, nc, CHUNK), lambda i, h: (i, h, 0, 0, 0))]
    args = [qkv, qkv, qkv, grow]
    if has_init:
        c0, n0, m0 = init
        in_specs += [pl.BlockSpec((1, 1, 2, hpb, HEAD_DIM, HEAD_DIM), lambda i, h: (i, 0, 0, h, 0, 0)),
                     pl.BlockSpec((1, hpb, 2, HEAD_DIM), lambda i, h: (i, h, 0, 0)),
                     pl.BlockSpec((1, hpb, 2, 1), lambda i, h: (i, h, 0, 0))]
        args += [c0, n0, m0]
    out_shape = [jax.ShapeDtypeStruct((b, s, D_MODEL), F32)]
    out_specs = [pl.BlockSpec((1, s, width), lambda i, h: (i, 0, h))]
    if emit_state:
        out_shape += [jax.ShapeDtypeStruct((b, 1, 2, N_HEADS, HEAD_DIM, HEAD_DIM), F32),
                      jax.ShapeDtypeStruct((b, N_HEADS, 2, HEAD_DIM), F32),
                      jax.ShapeDtypeStruct((b, N_HEADS, 2, 128), F32)]
        out_specs += [pl.BlockSpec((1, 1, 2, hpb, HEAD_DIM, HEAD_DIM), lambda i, h: (i, 0, 0, h, 0, 0)),
                      pl.BlockSpec((1, hpb, 2, HEAD_DIM), lambda i, h: (i, h, 0, 0)),
                      pl.BlockSpec((1, hpb, 2, 128), lambda i, h: (i, h, 0, 0))]
    return pl.pallas_call(
        functools.partial(_mlstm_kernel, seq=s, heads=hpb, has_init=has_init, emit_state=emit_state),
        grid=(b, nhb),
        in_specs=in_specs,
        out_specs=tuple(out_specs),
        out_shape=tuple(out_shape),
        scratch_shapes=[pltpu.VMEM((2 * hpb, HEAD_DIM, HEAD_DIM), F32), pltpu.VMEM((2 * hpb, 1, HEAD_DIM), F32),
                        pltpu.VMEM((2 * hpb, N_ROW_KINDS, nc, CHUNK), F32)],
        compiler_params=_params(("arbitrary", "arbitrary")),
        name=f"mlstm_{s}",
    )(*args)


def _hyena_kernel(x1_ref, x2_ref, v_ref, w1_ref, w2_ref, wv_ref, b1_ref, b2_ref, bv_ref, skip_ref,
                  g0_ref, g1_ref, fwd_ref, inv_ref, o_ref, zf_s, yf_s, *, seq, p, shift, kk0):
    nb = seq // p
    ng = g0_ref.shape[0]
    bb = x1_ref.shape[0]
    lanes = o_ref.shape[-1]
    sub = lax.broadcasted_iota(jnp.int32, (8, 1), 0)

    def conv_rows(src_ref, bi, r, w_ref, b_ref):
        rr = r * p
        up = src_ref[bi, rr - 8:rr, :][7:8, :] if rr > 0 else jnp.zeros((1, lanes), F32)
        dn = src_ref[bi, rr + p:rr + p + 8, :][0:1, :] if rr + p < seq else jnp.zeros((1, lanes), F32)
        cur = src_ref[bi, rr:rr + p, :]
        prev = pltpu.roll(cur, 1, 0)
        prev = jnp.concatenate([jnp.where(sub == 0, up, prev[0:8]), prev[8:]], axis=0)
        nxt = pltpu.roll(cur, p - 1, 0)
        nxt = jnp.concatenate([nxt[:p - 8], jnp.where(sub == 7, dn, nxt[p - 8:])], axis=0)
        return b_ref[...] + prev * w_ref[0:1, :] + cur * w_ref[1:2, :] + nxt * w_ref[2:3, :]

    fwd = fwd_ref[...]
    inv = inv_ref[...]
    terms = [[(j, i - j + shift - kk0) for j in range(nb) if 0 <= i - j + shift - kk0 < ng] for i in range(nb)]
    batch = range(bb)
    orders = ((g0_ref, x1_ref, w1_ref, b1_ref), (g1_ref, x2_ref, w2_ref, b2_ref))

    def forward_dft(bi, order, j, z):
        zf_s[bi, order, j] = _dot(fwd, z.astype(BF16))

    def spectral_mac(bi, order, i):
        g_ref = orders[order][0]
        for r in range(p // SPEC_ROWS):
            rr = r * SPEC_ROWS
            re = jnp.zeros((SPEC_ROWS, lanes), F32)
            im = jnp.zeros((SPEC_ROWS, lanes), F32)
            for j, kk in terms[i]:
                zre = zf_s[bi, order, j, rr:rr + SPEC_ROWS, :]
                zim = zf_s[bi, order, j, p + rr:p + rr + SPEC_ROWS, :]
                gre = g_ref[kk, rr:rr + SPEC_ROWS, :]
                gim = g_ref[kk, p + rr:p + rr + SPEC_ROWS, :]
                re = re + (zre * gre - zim * gim)
                im = im + (zre * gim + zim * gre)
            if r == 0:
                dc = jnp.zeros((1, lanes), F32)
                ny = jnp.zeros((1, lanes), F32)
                for j, kk in terms[i]:
                    dc = dc + zf_s[bi, order, j, 0:1, :] * g_ref[kk, 0:1, :]
                    ny = ny + zf_s[bi, order, j, p:p + 1, :] * g_ref[kk, p:p + 1, :]
                first = lax.broadcasted_iota(jnp.int32, (SPEC_ROWS, 1), 0) == 0
                re = jnp.where(first, dc, re)
                im = jnp.where(first, ny, im)
            yf_s[bi, i, rr:rr + SPEC_ROWS, :] = re
            yf_s[bi, i, p + rr:p + rr + SPEC_ROWS, :] = im

    def finish_block(bi, order, i, y):
        _, x_ref, w_ref, b_ref = orders[order]
        rows = slice(i * p, (i + 1) * p)
        z = conv_rows(x_ref, bi, i, w_ref, b_ref) * (y + skip_ref[order:order + 1, :] * o_ref[bi, rows, :])
        o_ref[bi, rows, :] = z
        if order + 1 < len(orders):
            forward_dft(bi, order + 1, i, z)

    for j in range(nb):
        for bi in batch:
            z = conv_rows(v_ref, bi, j, wv_ref, bv_ref)
            o_ref[bi, j * p:(j + 1) * p, :] = z
            forward_dft(bi, 0, j, z)
    for order in range(len(orders)):
        pending = {}
        for i in range(nb):
            for bi in batch:
                spectral_mac(bi, order, i)
            for bi in batch:
                pending[bi, i] = _dot(inv, yf_s[bi, i].astype(BF16))
            if i >= 1:
                for bi in batch:
                    finish_block(bi, order, i - 1, pending.pop((bi, i - 1)))
        for bi in batch:
            finish_block(bi, order, nb - 1, pending.pop((bi, nb - 1)))


def _hyena(uh, conv_w, conv_b, skip, spectra, fwd, inv, p, bb):
    b, s, _ = uh.shape
    dc = HYENA_LANES
    nct = D_MODEL // dc
    nb = s // p
    shift, kk0, ng = _spectra_layout(s, p)
    conv_b = conv_b.reshape(1, -1)

    def part(k):
        return pl.BlockSpec((bb, s, dc), lambda c, i, k=k: (i, 0, k * nct + c))

    def wpart(k, rows):
        return pl.BlockSpec((rows, dc), lambda c, i, k=k: (0, k * nct + c))

    def gpart(order):
        return pl.BlockSpec((ng, 2 * p, dc), lambda c, i, order=order: (0, 0, order * nct + c),
                            pipeline_mode=pl.Buffered(1))

    in_specs = [part(0), part(1), part(2), wpart(0, 3), wpart(1, 3), wpart(2, 3),
                wpart(0, 1), wpart(1, 1), wpart(2, 1),
                pl.BlockSpec((2, dc), lambda c, i: (0, c)),
                gpart(0), gpart(1), _const_spec(fwd.shape), _const_spec(inv.shape)]
    return pl.pallas_call(
        functools.partial(_hyena_kernel, seq=s, p=p, shift=shift, kk0=kk0),
        grid=(nct, b // bb),
        in_specs=in_specs,
        out_specs=pl.BlockSpec((bb, s, dc), lambda c, i: (i, 0, c)),
        out_shape=jax.ShapeDtypeStruct((b, s, D_MODEL), F32),
        scratch_shapes=[pltpu.VMEM((bb, 2, nb, 2 * p, dc), F32), pltpu.VMEM((bb, nb, 2 * p, dc), F32)],
        compiler_params=_params(("arbitrary", "arbitrary")),
        name=f"hyena_{s}",
    )(uh, uh, uh, conv_w, conv_w, conv_w, conv_b, conv_b, conv_b, skip, spectra, spectra, fwd, inv)


def _tail_kernel(*refs, has_pos):
    if has_pos:
        x_ref, pos_ref = refs[:2]
        refs = refs[2:]
    else:
        x_ref = refs[0]
        refs = refs[1:]
    (h_ref, o_ref, gm_ref, gh_ref, yh_ref, modb_ref, modc_ref, mnw_ref, n2w_ref, fnw_ref,
     wbm_ref, wbh_ref, wout_ref, w1_ref, b1_ref, w2_ref, b2_ref, y_ref) = refs
    bb, tm, _ = x_ref.shape
    rows = bb * tm
    tile = lambda ref: ref[...].reshape(rows, D_MODEL)
    x = tile(x_ref)
    if has_pos:
        x = x + pos_ref[...]
    g1 = modb_ref[0, :, 0:D_MODEL]
    sh2 = modb_ref[0, :, D_MODEL:2 * D_MODEL]
    sc2 = modc_ref[0, :, 0:D_MODEL]
    g2 = modc_ref[0, :, D_MODEL:2 * D_MODEL]
    h = tile(h_ref)
    heads = [_rms(h[:, hd * HEAD_DIM:(hd + 1) * HEAD_DIM]) for hd in range(N_HEADS)]
    hm = jnp.concatenate(heads, axis=-1) * mnw_ref[...] * _sigmoid(tile(o_ref).astype(F32))
    merged = (_sigmoid(tile(gm_ref).astype(F32)) * _dot(hm.astype(BF16), wbm_ref[...])
              + _sigmoid(tile(gh_ref).astype(F32)) * _dot(tile(yh_ref).astype(BF16), wbh_ref[...]))
    x1 = x + g1 * _dot(merged.astype(BF16), wout_ref[...])
    hn2 = ((_rms(x1) * n2w_ref[...]) * (1.0 + sc2) + sh2).astype(BF16)
    ff = b2_ref[...]
    for kc in range(D_FF // D_MODEL):
        sl = slice(kc * D_MODEL, (kc + 1) * D_MODEL)
        a = jnp.maximum(_dot(hn2, w1_ref[:, sl]) + b1_ref[:, sl], 0.0)
        ff = ff + _dot((a * a).astype(BF16), w2_ref[sl, :])
    x2 = x1 + g2 * ff
    y_ref[...] = (_rms(x2) * fnw_ref[...]).reshape(bb, tm, D_MODEL)


def _tail(x, pos, h, o, gmh, yh, mod3, mod_row, mnw, n2w, fnw, wbm, wbh, wout, w1, b1, w2, b2):
    b, s, _ = x.shape
    tm = min(TAIL_TOKEN_TILE, s)
    has_pos = pos is not None
    bb = 1 if has_pos else max(1, TAIL_TOKEN_TILE // s)
    tok = pl.BlockSpec((bb, tm, D_MODEL), lambda i, j: (i, j, 0))
    in_specs = [tok]
    args = [x]
    if has_pos:
        in_specs.append(pl.BlockSpec((tm, D_MODEL), lambda i, j: (j, 0)))
        args.append(pos)
    in_specs += [tok, tok, tok, pl.BlockSpec((bb, tm, D_MODEL), lambda i, j: (i, j, 1)), tok,
                 pl.BlockSpec((1, 1, 2 * D_MODEL), lambda i, j: (mod_row(i), 0, 1)),
                 pl.BlockSpec((1, 1, 2 * D_MODEL), lambda i, j: (mod_row(i), 0, 2))]
    args += [h, o, gmh, gmh, yh, mod3, mod3]
    consts = [mnw, n2w, fnw, wbm, wbh, wout, w1, b1, w2, b2]
    in_specs += [_const_spec(a.shape) for a in consts]
    args += consts
    return pl.pallas_call(
        functools.partial(_tail_kernel, has_pos=has_pos),
        grid=(b // bb, s // tm),
        in_specs=in_specs,
        out_specs=tok,
        out_shape=jax.ShapeDtypeStruct((b, s, D_MODEL), F32),
        compiler_params=_params(("arbitrary", "arbitrary")),
        name="tail_pos" if has_pos else "tail",
    )(*args)


def _gate_rows(gates_t):
    b, _, s = gates_t.shape
    return gates_t.reshape(b, N_HEADS, 4, s // CHUNK, CHUNK)


def kernel(x_prompt, x_sample, state_mlstm_C, state_mlstm_n, state_mlstm_m, c, c_ctx, w_ada, b_ada, norm1_w,
           w_in, b_in, hy_conv_w, hy_conv_b, filt_w1, filt_b1, filt_freq1, filt_w2, filt_b2, filt_freq2,
           filt_w3, hy_skip, mlstm_norm_w, w_br_m, w_br_h, w_out, norm2_w, w_mlp1, b_mlp1, w_mlp2, b_mlp2,
           final_norm_w):
    depth = w_ada.shape[0]
    assert depth == 1, "single-layer configuration"
    l = 0
    dec_b, dec_s, _ = x_sample.shape
    ctx_s = x_prompt.shape[1]

    n_rows = -(-(1 + dec_b) // 8) * 8
    cond = jnp.concatenate([c_ctx[None, :], c, jnp.zeros((n_rows - 1 - dec_b, D_MODEL), F32)], axis=0)
    mod = _modulation(cond, w_ada[l], b_ada[l])
    mod3 = mod.reshape(n_rows, 1, 6 * D_MODEL)

    wt32 = jnp.transpose(w_in[l])
    wt = wt32.astype(BF16)
    bias = b_in[l]
    ba = bias[:QKVO_COLS].reshape(1, -1)
    head_major = np.arange(N_GATES).reshape(4, N_HEADS).T.reshape(-1)
    wg32 = wt32[QKVO_COLS:QKVO_COLS + N_GATES][head_major]
    wg_hi = wg32.astype(BF16)
    wg = jnp.concatenate([wg_hi, (wg32 - wg_hi.astype(F32)).astype(BF16)], axis=0)
    bg = bias[QKVO_COLS:QKVO_COLS + N_GATES][head_major].reshape(-1, 1)
    br = bias[QKVO_COLS + N_GATES:].reshape(1, -1)
    n1w = norm1_w[l].reshape(1, -1)
    tail_w = (mlstm_norm_w[l].reshape(1, -1), norm2_w[l].reshape(1, -1), final_norm_w.reshape(1, -1),
              w_br_m[l].astype(BF16), w_br_h[l].astype(BF16), w_out[l].astype(BF16),
              w_mlp1[l].astype(BF16), b_mlp1[l].reshape(1, -1), w_mlp2[l].astype(BF16), b_mlp2[l].reshape(1, -1))
    filt = (filt_w1[l], filt_b1[l], filt_freq1[l], filt_w2[l], filt_b2[l], filt_freq2[l], filt_w3[l])

    pos = _pos_table(dec_s)

    def run(x, pos_tab, mod_row, init, emit_state, p, bb):
        s = x.shape[1]
        fwd32, inv32, sign = (jnp.asarray(a) for a in _dft_mats(p))
        spectra = _filter_spectra(s, p, *filt, fwd32, sign)
        qkv, o, gates, uh, gmh = _in_projection(x, pos_tab, mod3, mod_row, n1w, wt, ba, wg, bg, br)
        ml = _mlstm(qkv, _gate_rows(gates), init, emit_state)
        yh = _hyena(uh, hy_conv_w[l], hy_conv_b[l], hy_skip[l], spectra,
                    fwd32.astype(BF16), inv32.astype(BF16), p, bb)
        y = _tail(x, pos_tab, ml[0], o, gmh, yh, mod3, mod_row, *tail_w)
        return y, ml[1:]

    y_prompt, (st_c, st_n, st_m) = run(x_prompt, None, lambda i: 0, None, True, ctx_s, 8)
    init = (state_mlstm_C,
            jnp.transpose(state_mlstm_n[:, l], (0, 2, 1, 3)),
            jnp.transpose(state_mlstm_m[:, l], (0, 2, 1))[..., None])
    y_sample, _ = run(x_sample, pos, lambda i: i + 1, init, False, dec_s // 4, 1)

    new_state_n = jnp.transpose(st_n, (0, 2, 1, 3))[:, None]
    new_state_m = jnp.transpose(st_m[..., 0], (0, 2, 1))[:, None]
    return (y_prompt, y_sample, st_c, new_state_n, new_state_m)
```
